```python
import math
import jax
import jax.numpy as jnp
from jax import lax
import numpy as np

D_MODEL = 2048
BATCH = 2
SEQ = 4096
DEPTH = 4
DEC_BATCH = 32
DEC_SEQ = 8
PAST_LEN = 16384
PAGE_SIZE = 128

MIX_W = D_MODEL // 2
N_BRANCH = 3
GLA_HEADS = 4
GLA_DK = MIX_W // 2 // GLA_HEADS
GLA_DV = MIX_W // GLA_HEADS
GLA_RANK = 16
GLA_TAU = 16.0
GLA_CHUNK = 64
SWA_HEADS = 16
SWA_KV = 2
SWA_HD = MIX_W // SWA_HEADS
SWA_GROUP = SWA_HEADS // SWA_KV
WINDOW = 128
SWA_BLOCK = 128
N_BUCKETS = 32
MAX_DIST = 128
CONV_W = 3
N_MEM = 256
X_HEADS = 4
X_HD = 128
D_FF = ((8 * D_MODEL // 3 + 127) // 128) * 128
EPS = 1e-6
SPLIT_SIZES = (GLA_HEADS * GLA_DK, GLA_HEADS * GLA_DK, MIX_W, MIX_W, GLA_RANK,
               SWA_HEADS * SWA_HD, SWA_KV * SWA_HD, SWA_KV * SWA_HD,
               MIX_W, MIX_W, MIX_W, N_BRANCH * D_MODEL)
N_IN = sum(SPLIT_SIZES)

kernel_name = 'hybrid_gla_swa_shortconv_decoder_step'


def rmsnorm(x, g):
    xf = x.astype(jnp.float32)
    y = xf * lax.rsqrt(jnp.mean(xf * xf, axis=-1, keepdims=True) + EPS)
    return (y * g.astype(jnp.float32)).astype(x.dtype)


def split_columns(z):
    idx, acc = [], 0
    for s in SPLIT_SIZES[:-1]:
        acc += s
        idx.append(acc)
    return jnp.split(z, idx, axis=-1)


def causal_dwconv(u, prev, w):
    T = u.shape[1]
    up = jnp.concatenate([prev.astype(u.dtype), u], axis=1)
    out = up[:, 0:T] * w[0]
    for j in range(1, CONV_W):
        out = out + up[:, j:j + T] * w[j]
    return out, up[:, T:]


def t5_bucket(dist):
    n = jnp.maximum(dist, 0)
    max_exact = N_BUCKETS // 2
    nf = jnp.maximum(n, 1).astype(jnp.float32)
    large = max_exact + (jnp.log(nf / max_exact) / math.log(MAX_DIST / max_exact)
                         * (N_BUCKETS - max_exact)).astype(jnp.int32)
    large = jnp.minimum(large, N_BUCKETS - 1)
    return jnp.where(n < max_exact, n, large)


def gla_chunked(q, k, v, lg, s0):
    B, T, H, _ = q.shape
    L = min(GLA_CHUNK, T)
    nc = -(-T // L)
    pad = nc * L - T

    def chunks(a):
        a = a.astype(jnp.float32)
        if pad:
            a = jnp.pad(a, ((0, 0), (0, pad), (0, 0), (0, 0)))
        return a.reshape(B, nc, L, H, a.shape[-1]).transpose(1, 0, 3, 2, 4)

    causal = jnp.tril(jnp.ones((L, L), dtype=bool))[:, :, None]

    def step(S, inp):
        qc, kc, vc, gc = inp
        b = jnp.cumsum(gc, axis=2)
        o = jnp.einsum('bhtd,bhdv->bhtv', qc * jnp.exp(b), S)
        decay = jnp.exp(jnp.where(causal, b[:, :, :, None, :] - b[:, :, None, :, :], -jnp.inf))
        a = jnp.einsum('bhtd,bhsd,bhtsd->bhts', qc, kc, decay)
        o = o + jnp.einsum('bhts,bhsv->bhtv', a, vc)
        b_end = b[:, :, -1:, :]
        S = (jnp.exp(b_end[:, :, 0, :, None]) * S
             + jnp.einsum('bhsd,bhsv->bhdv', kc * jnp.exp(b_end - b), vc))
        return S, o

    S, o = lax.scan(step, s0.astype(jnp.float32), (chunks(q), chunks(k), chunks(v), chunks(lg)))
    o = o.transpose(1, 0, 3, 2, 4).reshape(B, nc * L, H, -1)[:, :T]
    return o.astype(v.dtype), S.astype(s0.dtype)


def sink_attention(q, k, v, dist, valid, sinks, rel_bias):
    s = jnp.einsum('bnqkgd,bnskd->bnkgqs', q, k).astype(jnp.float32) * (SWA_HD ** -0.5)
    bias = rel_bias[t5_bucket(dist)].astype(jnp.float32)
    bias = bias.reshape(dist.shape + (SWA_KV, SWA_GROUP)).transpose(2, 3, 0, 1)
    s = jnp.where(valid[None, :, None, None], s + bias, -jnp.inf)
    sink = sinks.astype(jnp.float32).reshape(SWA_KV, SWA_GROUP)[:, :, None, None]
    m = jnp.maximum(jnp.max(s, axis=-1, keepdims=True), sink)
    p = jnp.exp(s - m)
    p = p / (jnp.sum(p, axis=-1, keepdims=True) + jnp.exp(sink - m))
    return jnp.einsum('bnkgqs,bnskd->bnqkgd', p.astype(v.dtype), v)


def swa_prompt(q, k, v, sinks, rel_bias):
    B, T = q.shape[:2]
    nb = T // SWA_BLOCK
    qb = q.reshape(B, nb, SWA_BLOCK, SWA_KV, SWA_GROUP, SWA_HD)

    def band(a):
        prev = jnp.concatenate([jnp.zeros_like(a[:, :SWA_BLOCK]), a[:, :T - SWA_BLOCK]], axis=1)
        return jnp.concatenate([prev.reshape(B, nb, SWA_BLOCK, SWA_KV, SWA_HD),
                                a.reshape(B, nb, SWA_BLOCK, SWA_KV, SWA_HD)], axis=2)

    i = jnp.arange(SWA_BLOCK)[:, None]
    j = jnp.arange(2 * SWA_BLOCK)[None, :]
    dist = SWA_BLOCK + i - j
    kpos = (jnp.arange(nb)[:, None, None] - 1) * SWA_BLOCK + j[None]
    valid = ((dist >= 0) & (dist <= WINDOW))[None] & (kpos >= 0)
    o = sink_attention(qb, band(k), band(v), dist, valid, sinks, rel_bias)
    return o.reshape(B, T, MIX_W), k[:, T - WINDOW:], v[:, T - WINDOW:]


def swa_sample(q, k, v, k_buf, v_buf, sinks, rel_bias):
    B, T = q.shape[:2]
    W = k_buf.shape[1]
    kc = jnp.concatenate([k_buf.astype(k.dtype), k], axis=1)
    vc = jnp.concatenate([v_buf.astype(v.dtype), v], axis=1)
    dist = W + jnp.arange(T)[:, None] - jnp.arange(W + T)[None, :]
    valid = ((dist >= 0) & (dist <= WINDOW))[None]
    o = sink_attention(q.reshape(B, 1, T, SWA_KV, SWA_GROUP, SWA_HD), kc[:, None], vc[:, None],
                       dist, valid, sinks, rel_bias)
    return o.reshape(B, T, MIX_W), kc[:, T:], vc[:, T:]


def parallel_mixers(h, w, rel_bias, gla_s0, swa_buf, conv_prev):
    B, T, _ = h.shape
    gq, gk, gv, gr, glr, sq, sk, sv, cb, cc, ch, gates = split_columns(h @ w['w_in'])
    lg = jax.nn.log_sigmoid((glr @ w['gla_gate_up'] + w['gla_gate_b']).astype(jnp.float32)) / GLA_TAU
    o, gla_S = gla_chunked(gq.reshape(B, T, GLA_HEADS, GLA_DK) * (GLA_DK ** -0.5),
                           gk.reshape(B, T, GLA_HEADS, GLA_DK),
                           gv.reshape(B, T, GLA_HEADS, GLA_DV),
                           lg.reshape(B, T, GLA_HEADS, GLA_DK), gla_s0)
    br_a = rmsnorm(o, w['gla_norm']).reshape(B, T, MIX_W) * jax.nn.silu(gr)
    sq = sq.reshape(B, T, SWA_KV, SWA_GROUP, SWA_HD)
    sk = sk.reshape(B, T, SWA_KV, SWA_HD)
    sv = sv.reshape(B, T, SWA_KV, SWA_HD)
    if swa_buf is None:
        br_b, kb, vb = swa_prompt(sq, sk, sv, w['swa_sinks'], rel_bias)
    else:
        br_b, kb, vb = swa_sample(sq, sk, sv, swa_buf[0], swa_buf[1], w['swa_sinks'], rel_bias)
    u, conv_buf = causal_dwconv(cc * ch, conv_prev, w['conv_w'])
    br_c = cb * u
    g = jax.nn.sigmoid(gates).reshape(B, T, N_BRANCH, D_MODEL)
    wb = w['w_branch']
    merged = (g[:, :, 0] * (br_a @ wb[0]) + g[:, :, 1] * (br_b @ wb[1]) + g[:, :, 2] * (br_c @ wb[2]))
    return merged @ w['w_out'], (gla_S, kb, vb, conv_buf)


def cross_attention(h, mem_k, mem_v, wq, wo):
    B, T, _ = h.shape
    q = (h @ wq).reshape(B, T, X_HEADS, X_HD)
    s = jnp.einsum('bthd,bmhd->bhtm', q, mem_k.astype(q.dtype)).astype(jnp.float32) * (X_HD ** -0.5)
    p = jax.nn.softmax(s, axis=-1).astype(h.dtype)
    o = jnp.einsum('bhtm,bmhd->bthd', p, mem_v.astype(h.dtype)).reshape(B, T, X_HEADS * X_HD)
    return o @ wo


def conv_ffn(h, w_up, cw, cbias, w_down, prev):
    u, g = jnp.split(h @ w_up, 2, axis=-1)
    gc, buf = causal_dwconv(g, prev, cw)
    return (jax.nn.silu(gc + cbias) * u) @ w_down, buf


def trunk_layer(x, w, rel_bias, gla_s0, swa_buf, conv_prev, ffn_prev, mem_k, mem_v):
    mix, (gla_S, kb, vb, conv_buf) = parallel_mixers(rmsnorm(x, w['norm_mix']), w, rel_bias,
                                                      gla_s0, swa_buf, conv_prev)
    x = x + mix
    x = x + cross_attention(rmsnorm(x, w['norm_x']), mem_k, mem_v, w['wx_q'], w['wx_o'])
    f, ffn_buf = conv_ffn(rmsnorm(x, w['norm_ffn']), w['ffn_up'], w['ffn_conv_w'], w['ffn_conv_b'],
                          w['ffn_down'], ffn_prev)
    return x + f, (gla_S, kb, vb, conv_buf, ffn_buf)


def setup_inputs(seed: int = 0) -> dict:
    key = jax.random.key(seed)
    ks = jax.random.split(key, 40)
    f32 = jnp.float32

    def nrm(k, shape, scale):
        return jax.random.normal(k, shape, f32) * scale

    def gain(k, shape):
        return 1.0 + 0.01 * jax.random.normal(k, shape, f32)

    W = min(WINDOW, PAST_LEN)
    return {
        'x_prompt': nrm(ks[0], (BATCH, SEQ, D_MODEL), 1.0),
        'x_sample': nrm(ks[1], (DEC_BATCH, DEC_SEQ, D_MODEL), 1.0),
        'state_gla': nrm(ks[2], (DEPTH, DEC_BATCH, GLA_HEADS, GLA_DK, GLA_DV), 1.0),
        'cache_swa_k': nrm(ks[3], (DEPTH, DEC_BATCH, W, SWA_KV, SWA_HD), 1.0),
        'cache_swa_v': nrm(ks[4], (DEPTH, DEC_BATCH, W, SWA_KV, SWA_HD), 1.0),
        'state_conv': nrm(ks[5], (DEPTH, DEC_BATCH, CONV_W - 1, MIX_W), 1.0),
        'state_ffn': nrm(ks[6], (DEPTH, DEC_BATCH, CONV_W - 1, D_FF), 1.0),
        'cache_mem_k': nrm(ks[7], (DEPTH, DEC_BATCH, N_MEM, X_HEADS, X_HD), 1.0),
        'cache_mem_v': nrm(ks[8], (DEPTH, DEC_BATCH, N_MEM, X_HEADS, X_HD), 1.0),
        'mem_prompt': nrm(ks[9], (BATCH, N_MEM, D_MODEL), 1.0),
        'norm_mix': gain(ks[10], (DEPTH, D_MODEL)),
        'w_in': nrm(ks[11], (DEPTH, D_MODEL, N_IN), D_MODEL ** -0.5),
        'gla_gate_up': nrm(ks[12], (DEPTH, GLA_RANK, GLA_HEADS * GLA_DK), GLA_RANK ** -0.5),
        'gla_gate_b': nrm(ks[13], (DEPTH, GLA_HEADS * GLA_DK), 0.1),
        'gla_norm': gain(ks[14], (DEPTH, GLA_DV)),
        'swa_sinks': nrm(ks[15], (DEPTH, SWA_HEADS), 0.5),
        'rel_bias': nrm(ks[16], (N_BUCKETS, SWA_HEADS), 0.5),
        'conv_w': nrm(ks[17], (DEPTH, CONV_W, MIX_W), CONV_W ** -0.5),
        'w_branch': nrm(ks[18], (DEPTH, N_BRANCH, MIX_W, D_MODEL), MIX_W ** -0.5),
        'w_out': nrm(ks[19], (DEPTH, D_MODEL, D_MODEL), D_MODEL ** -0.5),
        'norm_x': gain(ks[20], (DEPTH, D_MODEL)),
        'wx_q': nrm(ks[21], (DEPTH, D_MODEL, X_HEADS * X_HD), D_MODEL ** -0.5),
        'wx_k': nrm(ks[22], (DEPTH, D_MODEL, X_HEADS * X_HD), D_MODEL ** -0.5),
        'wx_v': nrm(ks[23], (DEPTH, D_MODEL, X_HEADS * X_HD), D_MODEL ** -0.5),
        'wx_o': nrm(ks[24], (DEPTH, X_HEADS * X_HD, D_MODEL), (X_HEADS * X_HD) ** -0.5),
        'norm_ffn': gain(ks[25], (DEPTH, D_MODEL)),
        'ffn_up': nrm(ks[26], (DEPTH, D_MODEL, 2 * D_FF), D_MODEL ** -0.5),
        'ffn_conv_w': nrm(ks[27], (DEPTH, CONV_W, D_FF), CONV_W ** -0.5),
        'ffn_conv_b': nrm(ks[28], (DEPTH, D_FF), 0.01),
        'ffn_down': nrm(ks[29], (DEPTH, D_FF, D_MODEL), D_FF ** -0.5),
        'norm_final': gain(ks[30], (D_MODEL,)),
    }


def reference(x_prompt, x_sample, state_gla, cache_swa_k, cache_swa_v, state_conv, state_ffn,
              cache_mem_k, cache_mem_v, mem_prompt, norm_mix, w_in, gla_gate_up, gla_gate_b,
              gla_norm, swa_sinks, rel_bias, conv_w, w_branch, w_out, norm_x, wx_q, wx_k, wx_v,
              wx_o, norm_ffn, ffn_up, ffn_conv_w, ffn_conv_b, ffn_down, norm_final):
    Bp = x_prompt.shape[0]
    yp, ys = x_prompt, x_sample
    st_p, st_s = [], []
    for i in range(DEPTH):
        w = {'norm_mix': norm_mix[i], 'w_in': w_in[i], 'gla_gate_up': gla_gate_up[i],
             'gla_gate_b': gla_gate_b[i], 'gla_norm': gla_norm[i], 'swa_sinks': swa_sinks[i],
             'conv_w': conv_w[i], 'w_branch': w_branch[i], 'w_out': w_out[i], 'norm_x': norm_x[i],
             'wx_q': wx_q[i], 'wx_o': wx_o[i], 'norm_ffn': norm_ffn[i], 'ffn_up': ffn_up[i],
             'ffn_conv_w': ffn_conv_w[i], 'ffn_conv_b': ffn_conv_b[i], 'ffn_down': ffn_down[i]}
        mem_k = (mem_prompt @ wx_k[i]).reshape(Bp, -1, X_HEADS, X_HD)
        mem_v = (mem_prompt @ wx_v[i]).reshape(Bp, -1, X_HEADS, X_HD)
        zero_gla = jnp.zeros((Bp, GLA_HEADS, GLA_DK, GLA_DV), jnp.float32)
        zero_conv = jnp.zeros((Bp, CONV_W - 1, MIX_W), yp.dtype)
        zero_ffn = jnp.zeros((Bp, CONV_W - 1, D_FF), yp.dtype)
        yp, sp = trunk_layer(yp, w, rel_bias, zero_gla, None, zero_conv, zero_ffn, mem_k, mem_v)
        ys, ss = trunk_layer(ys, w, rel_bias, state_gla[i], (cache_swa_k[i], cache_swa_v[i]),
                             state_conv[i], state_ffn[i], cache_mem_k[i], cache_mem_v[i])
        st_p.append(sp + (mem_k, mem_v))
        st_s.append(ss)
    y_prompt = rmsnorm(yp, norm_final)
    y_sample = rmsnorm(ys, norm_final)

    def stack(lst, j):
        return jnp.stack([s[j] for s in lst])

    return (y_prompt, y_sample, stack(st_p, 0), stack(st_s, 0), stack(st_p, 1), stack(st_p, 2),
            stack(st_s, 1), stack(st_s, 2), stack(st_p, 3), stack(st_s, 3), stack(st_p, 4),
            stack(st_s, 4), stack(st_p, 5), stack(st_p, 6))
```

```python
import functools
import math

import jax
import jax.numpy as jnp
from jax import lax
from jax.experimental import pallas as pl
from jax.experimental.pallas import tpu as pltpu

F32 = jnp.float32
BF16 = jnp.bfloat16

EPS = 1e-6
GLA_TAU = 16.0
GLA_CHUNK = 64
GLA_SUB = 16
WINDOW = 128
N_BUCKETS = 32
MAX_DIST = 128
CONV_W = 3
NEG = -1e30

LANE = 128
SUBLANE = 8
VMEM_LIMIT = 56 * 1024 * 1024


def _cparams(sem):
    return pltpu.CompilerParams(dimension_semantics=sem, vmem_limit_bytes=VMEM_LIMIT)


def _tile(n, pref, unit):
    if n <= pref:
        return n
    t = (pref // unit) * unit
    while t > unit and n % t:
        t -= unit
    assert n % t == 0, (n, pref, unit)
    return t


def _mxu_dtype(rows):
    return BF16 if rows % 16 == 0 else F32


def _sigmoid(x):
    return 1.0 / (1.0 + jnp.exp(-x))


def _dot(a, b):
    return jnp.dot(a, b, preferred_element_type=F32)


def _dot_nt(a, b):
    return lax.dot_general(a, b, (((1,), (1,)), ((), ())), preferred_element_type=F32)


def _norm_mm_kernel(x_ref, g_ref, w_ref, o_ref, h_ref):
    @pl.when(pl.program_id(1) == 0)
    def _():
        x = x_ref[...].astype(F32)
        ms = jnp.mean(x * x, axis=-1, keepdims=True)
        h_ref[...] = (x * lax.rsqrt(ms + EPS) * g_ref[...]).astype(h_ref.dtype)

    o_ref[...] = _dot(h_ref[...], w_ref[...]).astype(o_ref.dtype)


def _plain_mm_kernel(x_ref, w_ref, o_ref, h_ref):
    @pl.when(pl.program_id(1) == 0)
    def _():
        h_ref[...] = x_ref[...].astype(h_ref.dtype)

    o_ref[...] = _dot(h_ref[...], w_ref[...]).astype(o_ref.dtype)


def _res_mm_kernel(a_ref, w_ref, r_ref, o_ref):
    o_ref[...] = r_ref[...] + _dot(a_ref[...].astype(w_ref.dtype), w_ref[...])


def norm_matmul(x, gain, w, out_dtype, tm=1024, tn=512, name="norm_mm"):
    M, K = x.shape
    N = w.shape[1]
    tm = _tile(M, tm, SUBLANE)
    tn = _tile(N, tn, LANE)
    grid = (M // tm, N // tn)
    x_spec = pl.BlockSpec((tm, K), lambda i, j: (i, 0))
    w_spec = pl.BlockSpec((K, tn), lambda i, j: (0, j))
    o_spec = pl.BlockSpec((tm, tn), lambda i, j: (i, j))
    scratch = [pltpu.VMEM((tm, K), BF16)]
    if gain is None:
        return pl.pallas_call(
            _plain_mm_kernel, grid=grid, in_specs=[x_spec, w_spec], out_specs=o_spec,
            out_shape=jax.ShapeDtypeStruct((M, N), out_dtype), scratch_shapes=scratch,
            compiler_params=_cparams(("parallel", "arbitrary")), name=name)(x, w)
    g_spec = pl.BlockSpec((1, K), lambda i, j: (0, 0))
    return pl.pallas_call(
        _norm_mm_kernel, grid=grid, in_specs=[x_spec, g_spec, w_spec], out_specs=o_spec,
        out_shape=jax.ShapeDtypeStruct((M, N), out_dtype), scratch_shapes=scratch,
        compiler_params=_cparams(("parallel", "arbitrary")), name=name)(x, gain.reshape(1, K), w)


def res_matmul(a, w, res, tm=1024, tn=512, name="res_mm"):
    M, K = a.shape
    N = w.shape[1]
    tm = _tile(M, tm, SUBLANE)
    tn = _tile(N, tn, LANE)
    return pl.pallas_call(
        _res_mm_kernel, grid=(M // tm, N // tn),
        in_specs=[pl.BlockSpec((tm, K), lambda i, j: (i, 0)),
                  pl.BlockSpec((K, tn), lambda i, j: (0, j)),
                  pl.BlockSpec((tm, tn), lambda i, j: (i, j))],
        out_specs=pl.BlockSpec((tm, tn), lambda i, j: (i, j)),
        out_shape=jax.ShapeDtypeStruct((M, N), F32),
        compiler_params=_cparams(("parallel", "arbitrary")), name=name)(a, w, res)


def _final_norm_kernel(x_ref, g_ref, o_ref):
    x = x_ref[...]
    ms = jnp.mean(x * x, axis=-1, keepdims=True)
    o_ref[...] = x * lax.rsqrt(ms + EPS) * g_ref[...]


def final_norm(x, gain, tm=512):
    M, K = x.shape
    tm = _tile(M, tm, SUBLANE)
    return pl.pallas_call(
        _final_norm_kernel, grid=(M // tm,),
        in_specs=[pl.BlockSpec((tm, K), lambda i: (i, 0)), pl.BlockSpec((1, K), lambda i: (0, 0))],
        out_specs=pl.BlockSpec((tm, K), lambda i: (i, 0)),
        out_shape=jax.ShapeDtypeStruct((M, K), F32),
        compiler_params=_cparams(("parallel",)), name="final_norm")(x, gain.reshape(1, K))


def _gla_kernel(q_ref, k_ref, v_ref, r_ref, glr_ref, up_ref, gb_ref, gn_ref, s0_ref,
                o_ref, st_ref, s_scr, *, lin, lp, n_chunks, scale):
    t = pl.program_id(2)
    dk = q_ref.shape[-1]
    dv = v_ref.shape[-1]
    c = min(GLA_SUB, lp)
    nsub = lp // c
    md = _mxu_dtype(lp)

    @pl.when(t == 0)
    def _():
        s_scr[...] = s0_ref[...]

    ones = jnp.ones((dk, LANE), md)
    row = lax.broadcasted_iota(jnp.int32, (lp, dk), 0)
    rsub = lax.broadcasted_iota(jnp.int32, (c, dk), 0)
    lsub = lax.broadcasted_iota(jnp.int32, (c, LANE), 1)

    def pad(x):
        if lin == lp:
            return x
        return jnp.concatenate([x, jnp.zeros((lp - lin, x.shape[1]), x.dtype)], axis=0)

    def lane_rows(x):
        if lp >= LANE:
            return x
        return jnp.concatenate([x, jnp.zeros((LANE - lp, x.shape[1]), x.dtype)], axis=0)

    def chunk(ci, carry):
        c0 = pl.multiple_of(ci * lin, lin)
        rows = pl.ds(c0, lin)
        q = pad(q_ref[rows, :].astype(F32)) * scale
        k = pad(k_ref[rows, :].astype(F32))
        v = pad(v_ref[rows, :].astype(F32))
        z = _dot(pad(glr_ref[rows, :].astype(F32)).astype(md), up_ref[...].astype(md)) + gb_ref[...]
        lg = -(jnp.maximum(-z, 0.0) + jnp.log1p(jnp.exp(-jnp.abs(z)))) / GLA_TAU
        if lin != lp:
            lg = jnp.where(row < lin, lg, 0.0)
        b = lg
        sh = 1
        while sh < lp:
            b = b + jnp.where(row >= sh, pltpu.roll(b, sh, 0), 0.0)
            sh *= 2
        st = s_scr[...]
        o = _dot_nt((q * jnp.exp(b)).astype(md), st.astype(md))
        a_rows = []
        for i in range(nsub):
            lo = i * c
            qb, kb, bb = q[lo:lo + c], k[lo:lo + c], b[lo:lo + c]
            ws = []
            for s in range(c):
                d = bb - bb[s:s + 1, :]
                ws.append(qb * kb[s:s + 1, :] * jnp.exp(jnp.where(rsub >= s, d, NEG)))
            rsum = _dot(jnp.concatenate(ws, axis=0).astype(md), ones)
            a_i = jnp.zeros((c, LANE), F32)
            for s in range(c):
                a_i = jnp.where(lsub == lo + s, rsum[s * c:(s + 1) * c], a_i)
            if i > 0:
                ref = b[lo - 1:lo, :]
                qt = qb * jnp.exp(bb - ref)
                kt = k * jnp.exp(jnp.where(row < lo, ref - b, NEG))
                a_i = a_i + _dot_nt(qt.astype(md), lane_rows(kt).astype(md))
            a_rows.append(a_i)
        a = jnp.concatenate(a_rows, axis=0)
        vk = lane_rows(v)
        o = o + _dot(a.astype(md), vk.astype(md))
        bend = b[lp - 1:lp, :]
        kd = lane_rows(k * jnp.exp(bend - b))
        s_scr[...] = st * jnp.exp(bend) + _dot(vk.T.astype(md), kd.astype(md))
        o = o[:lin]
        ms = jnp.mean(o * o, axis=-1, keepdims=True)
        y = o * lax.rsqrt(ms + EPS) * gn_ref[...]
        r = r_ref[rows, :].astype(F32)
        o_ref[rows, :] = (y * (r * _sigmoid(r))).astype(o_ref.dtype)
        return carry

    lax.fori_loop(0, n_chunks, chunk, 0)

    @pl.when(t == pl.num_programs(2) - 1)
    def _():
        st_ref[...] = s_scr[...]


def gla(z3, offs, gate_up, gate_b, gnorm, s0t, heads, out_dtype):
    B, T, _ = z3.shape
    dv, dk = s0t.shape[2], s0t.shape[3]
    lin = min(GLA_CHUNK, T)
    assert T % lin == 0
    lp = max(lin, 16)
    tt = _tile(T, 512, lin)
    n_chunks = tt // lin
    qo, ko, vo, ro, go = offs

    def col(off, width):
        return lambda b, h, t: (b, t, off // width + h)

    kern = functools.partial(_gla_kernel, lin=lin, lp=lp, n_chunks=n_chunks, scale=dk ** -0.5)
    return pl.pallas_call(
        kern, grid=(B, heads, T // tt),
        in_specs=[pl.BlockSpec((None, tt, dk), col(qo, dk)),
                  pl.BlockSpec((None, tt, dk), col(ko, dk)),
                  pl.BlockSpec((None, tt, dv), col(vo, dv)),
                  pl.BlockSpec((None, tt, dv), col(ro, dv)),
                  pl.BlockSpec((None, tt, LANE), lambda b, h, t: (b, t, go // LANE)),
                  pl.BlockSpec((LANE, dk), lambda b, h, t: (0, h)),
                  pl.BlockSpec((1, dk), lambda b, h, t: (0, h)),
                  pl.BlockSpec((1, dv), lambda b, h, t: (0, 0)),
                  pl.BlockSpec((None, None, dv, dk), lambda b, h, t: (b, h, 0, 0))],
        out_specs=[pl.BlockSpec((None, tt, dv), lambda b, h, t: (b, t, h)),
                   pl.BlockSpec((None, None, dv, dk), lambda b, h, t: (b, h, 0, 0))],
        out_shape=[jax.ShapeDtypeStruct((B, T, heads * dv), out_dtype),
                   jax.ShapeDtypeStruct((B, heads, dv, dk), F32)],
        scratch_shapes=[pltpu.VMEM((dv, dk), F32)],
        compiler_params=_cparams(("parallel", "parallel", "arbitrary")), name="gla",
    )(z3, z3, z3, z3, z3, gate_up, gate_b.reshape(1, -1), gnorm.reshape(1, -1), s0t)


def _swa_kernel(sink_ref, q_ref, kp_ref, kc_ref, vp_ref, vc_ref, bias_ref, o_ref, *,
                heads, group, hd, first_has_no_prev, scale):
    tq = q_ref.shape[0]
    p = kp_ref.shape[0]
    md = _mxu_dtype(tq)
    k = jnp.concatenate([kp_ref[...].astype(md), kc_ref[...].astype(md)], axis=0)
    v = jnp.concatenate([vp_ref[...].astype(md), vc_ref[...].astype(md)], axis=0)
    if first_has_no_prev:
        col = lax.broadcasted_iota(jnp.int32, (tq, p + tq), 1)
        pen = jnp.where((pl.program_id(1) == 0) & (col < p), NEG, 0.0)
    outs = []
    for h in range(heads):
        kv = h // group
        qh = q_ref[:, h * hd:(h + 1) * hd].astype(md)
        s = _dot_nt(qh, k[:, kv * hd:(kv + 1) * hd]) * scale + bias_ref[h]
        if first_has_no_prev:
            s = s + pen
        sink = sink_ref[h]
        m = jnp.maximum(jnp.max(s, axis=-1, keepdims=True), sink)
        e = jnp.exp(s - m)
        den = jnp.sum(e, axis=-1, keepdims=True) + jnp.exp(sink - m)
        outs.append(_dot(e.astype(md), v[:, kv * hd:(kv + 1) * hd]) / den)
    per = LANE // hd
    for j in range(heads // per):
        o_ref[:, j * LANE:(j + 1) * LANE] = jnp.concatenate(
            outs[j * per:(j + 1) * per], axis=-1).astype(o_ref.dtype)


def _t5_bucket(dist):
    n = jnp.maximum(dist, 0)
    max_exact = N_BUCKETS // 2
    nf = jnp.maximum(n, 1).astype(F32)
    large = max_exact + (jnp.log(nf / max_exact) / math.log(MAX_DIST / max_exact)
                         * (N_BUCKETS - max_exact)).astype(jnp.int32)
    large = jnp.minimum(large, N_BUCKETS - 1)
    return jnp.where(n < max_exact, n, large)


def _swa_bias_table(rel_bias, tq, p):
    i = jnp.arange(tq)[:, None]
    j = jnp.arange(p + tq)[None, :]
    dist = p + i - j
    valid = (dist >= 0) & (dist <= WINDOW)
    bias = rel_bias[_t5_bucket(dist)].astype(F32)
    return jnp.where(valid[None], jnp.transpose(bias, (2, 0, 1)), NEG)


def swa(z3, offs, prev_k, prev_v, sinks, bias, kv_heads, hd, tq, out_dtype):
    B, T, _ = z3.shape
    heads = sinks.shape[0]
    qw, kw = heads * hd, kv_heads * hd
    qo, ko, vo = offs
    nb = T // tq
    prompt = prev_k is None
    p = bias.shape[2] - tq
    if prompt:
        assert p == tq
        prev_specs = [pl.BlockSpec((None, p, kw), lambda b, n: (b, jnp.maximum(n - 1, 0), ko // kw)),
                      pl.BlockSpec((None, p, kw), lambda b, n: (b, jnp.maximum(n - 1, 0), vo // kw))]
        prev_k = prev_v = z3
    else:
        assert nb == 1
        prev_specs = [pl.BlockSpec((None, p, kw), lambda b, n: (b, 0, 0))] * 2
    kern = functools.partial(_swa_kernel, heads=heads, group=heads // kv_heads, hd=hd,
                             first_has_no_prev=prompt, scale=hd ** -0.5)
    return pl.pallas_call(
        kern, grid=(B, nb),
        in_specs=[pl.BlockSpec(memory_space=pltpu.SMEM),
                  pl.BlockSpec((None, tq, qw), lambda b, n: (b, n, qo // qw)),
                  prev_specs[0],
                  pl.BlockSpec((None, tq, kw), lambda b, n: (b, n, ko // kw)),
                  prev_specs[1],
                  pl.BlockSpec((None, tq, kw), lambda b, n: (b, n, vo // kw)),
                  pl.BlockSpec(bias.shape, lambda b, n: (0, 0, 0))],
        out_specs=pl.BlockSpec((None, tq, qw), lambda b, n: (b, n, 0)),
        out_shape=jax.ShapeDtypeStruct((B, T, qw), out_dtype),
        compiler_params=_cparams(("parallel", "arbitrary")), name="swa",
    )(sinks, z3, prev_k, z3, prev_v, z3, bias)


def _shifted(ext_ref, prev_ref, cur, tt):
    @pl.when(pl.program_id(2) == 0)
    def _():
        ext_ref[0:SUBLANE, :] = prev_ref[...]

    ext_ref[SUBLANE:SUBLANE + tt, :] = cur
    back1 = ext_ref[pl.ds(SUBLANE - 1, tt), :]
    back2 = ext_ref[pl.ds(SUBLANE - 2, tt), :]
    tail = cur[tt - SUBLANE:tt]
    ext_ref[0:SUBLANE, :] = tail
    return back1, back2, tail


def _conv_branch_kernel(cb_ref, cc_ref, ch_ref, prev_ref, w_ref, o_ref, st_ref, ext_ref):
    tt = cc_ref.shape[0]
    u = cc_ref[...].astype(F32) * ch_ref[...].astype(F32)
    u1, u2, tail = _shifted(ext_ref, prev_ref, u, tt)
    y = u2 * w_ref[0:1, :] + u1 * w_ref[1:2, :] + u * w_ref[2:3, :]
    o_ref[...] = (cb_ref[...].astype(F32) * y).astype(o_ref.dtype)
    st_ref[...] = tail


def _ffn_act_kernel(u_ref, g_ref, prev_ref, w_ref, b_ref, o_ref, st_ref, ext_ref):
    tt = g_ref.shape[0]
    g = g_ref[...].astype(F32)
    g1, g2, tail = _shifted(ext_ref, prev_ref, g, tt)
    gc = g2 * w_ref[0:1, :] + g1 * w_ref[1:2, :] + g * w_ref[2:3, :] + b_ref[...]
    o_ref[...] = (gc * _sigmoid(gc) * u_ref[...].astype(F32)).astype(o_ref.dtype)
    st_ref[...] = tail


def _conv_call(kern, z3, col_offs, prev8, small, width, out_dtype, name, tc=512, tt=1024):
    B, T, _ = z3.shape
    tc = _tile(width, tc, LANE)
    tt = _tile(T, tt, SUBLANE)
    assert T >= SUBLANE
    z_specs = [pl.BlockSpec((None, tt, tc), (lambda b, c, t, o=o: (b, t, o // tc + c))) for o in col_offs]
    small_specs = [pl.BlockSpec((s.shape[0], tc), lambda b, c, t: (0, c)) for s in small]
    st_spec = pl.BlockSpec((None, SUBLANE, tc), lambda b, c, t: (b, 0, c))
    return pl.pallas_call(
        kern, grid=(B, width // tc, T // tt),
        in_specs=z_specs + [st_spec] + small_specs,
        out_specs=[pl.BlockSpec((None, tt, tc), lambda b, c, t: (b, t, c)), st_spec],
        out_shape=[jax.ShapeDtypeStruct((B, T, width), out_dtype),
                   jax.ShapeDtypeStruct((B, SUBLANE, width), F32)],
        scratch_shapes=[pltpu.VMEM((tt + SUBLANE, tc), F32)],
        compiler_params=_cparams(("parallel", "parallel", "arbitrary")), name=name,
    )(*([z3] * len(col_offs)), prev8, *small)


def _merge_kernel(a_ref, b_ref, c_ref, wb_ref, g0_ref, g1_ref, g2_ref, o_ref):
    md = wb_ref.dtype
    acc = _sigmoid(g0_ref[...].astype(F32)) * _dot(a_ref[...].astype(md), wb_ref[0])
    acc = acc + _sigmoid(g1_ref[...].astype(F32)) * _dot(b_ref[...].astype(md), wb_ref[1])
    acc = acc + _sigmoid(g2_ref[...].astype(F32)) * _dot(c_ref[...].astype(md), wb_ref[2])
    o_ref[...] = acc.astype(o_ref.dtype)


def merge(br_a, br_b, br_c, wb, z, gate_off, out_dtype, tm=1024, tn=512):
    M, mix = br_a.shape
    d = wb.shape[2]
    tm = _tile(M, tm, SUBLANE)
    tn = _tile(d, tn, LANE)
    br_spec = pl.BlockSpec((tm, mix), lambda i, j: (i, 0))
    g_specs = [pl.BlockSpec((tm, tn), (lambda i, j, o=(gate_off + n * d) // tn: (i, o + j))) for n in range(3)]
    return pl.pallas_call(
        _merge_kernel, grid=(M // tm, d // tn),
        in_specs=[br_spec, br_spec, br_spec, pl.BlockSpec((3, mix, tn), lambda i, j: (0, 0, j))] + g_specs,
        out_specs=pl.BlockSpec((tm, tn), lambda i, j: (i, j)),
        out_shape=jax.ShapeDtypeStruct((M, d), out_dtype),
        compiler_params=_cparams(("parallel", "arbitrary")), name="merge",
    )(br_a, br_b, br_c, wb, z, z, z)


def _xattn_kernel(q_ref, k_ref, v_ref, o_ref, *, heads, hd, scale):
    md = _mxu_dtype(q_ref.shape[0])
    for h in range(heads):
        sl = slice(h * hd, (h + 1) * hd)
        s = _dot_nt(q_ref[:, sl].astype(md), k_ref[:, sl].astype(md)) * scale
        e = jnp.exp(s - jnp.max(s, axis=-1, keepdims=True))
        den = jnp.sum(e, axis=-1, keepdims=True)
        o_ref[:, sl] = (_dot(e.astype(md), v_ref[:, sl].astype(md)) / den).astype(o_ref.dtype)


def xattn(q3, mem_k, mem_v, heads, out_dtype, tq=512):
    B, T, w = q3.shape
    nm = mem_k.shape[1]
    hd = w // heads
    tq = _tile(T, tq, SUBLANE)
    kern = functools.partial(_xattn_kernel, heads=heads, hd=hd, scale=hd ** -0.5)
    m_spec = pl.BlockSpec((None, nm, w), lambda b, t: (b, 0, 0))
    return pl.pallas_call(
        kern, grid=(B, T // tq),
        in_specs=[pl.BlockSpec((None, tq, w), lambda b, t: (b, t, 0)), m_spec, m_spec],
        out_specs=pl.BlockSpec((None, tq, w), lambda b, t: (b, t, 0)),
        out_shape=jax.ShapeDtypeStruct((B, T, w), out_dtype),
        compiler_params=_cparams(("parallel", "arbitrary")), name="xattn",
    )(q3, mem_k, mem_v)


def _round_up(n, m):
    return -(-n // m) * m


def _trunk_layer(x, B, T, wts, lay, gla_s0t, swa_prev, conv_prev8, ffn_prev8, mem_k, mem_v, swa_bias, act_dtype):
    M, D = x.shape
    mix = lay["mix"]
    z = norm_matmul(x, wts["norm_mix"], wts["w_in"], act_dtype, name="in_proj")
    z3 = z.reshape(B, T, -1)
    br_a, gla_st = gla(z3, (lay["gq"], lay["gk"], lay["gv"], lay["gr"], lay["glr"]), wts["gate_up"],
                       wts["gate_b"], wts["gla_norm"], gla_s0t, lay["gla_heads"], act_dtype)
    pk, pv = (None, None) if swa_prev is None else swa_prev
    br_b = swa(z3, (lay["sq"], lay["sk"], lay["sv"]), pk, pv, wts["sinks"], swa_bias,
               lay["swa_kv"], lay["swa_hd"], min(T, WINDOW), act_dtype)
    br_c, conv_st = _conv_call(_conv_branch_kernel, z3, (lay["cb"], lay["cc"], lay["ch"]), conv_prev8,
                               [wts["conv_w"]], mix, act_dtype, "conv_branch")
    merged = merge(br_a.reshape(M, mix), br_b.reshape(M, mix), br_c.reshape(M, mix), wts["w_branch"],
                   z, lay["gates"], act_dtype)
    x = res_matmul(merged, wts["w_out"], x, name="out_proj")
    q = norm_matmul(x, wts["norm_x"], wts["wx_q"], act_dtype, name="xq_proj")
    xo = xattn(q.reshape(B, T, -1), mem_k, mem_v, lay["x_heads"], act_dtype)
    x = res_matmul(xo.reshape(M, -1), wts["wx_o"], x, name="xo_proj")
    dffp = lay["dffp"]
    zf = norm_matmul(x, wts["norm_ffn"], wts["ffn_up"], act_dtype, name="ffn_up")
    act, ffn_st = _conv_call(_ffn_act_kernel, zf.reshape(B, T, -1), (0, dffp), ffn_prev8,
                             [wts["ffn_conv_w"], wts["ffn_conv_b"]], dffp, act_dtype, "ffn_act")
    x = res_matmul(act.reshape(M, dffp), wts["ffn_down"], x, tm=512, name="ffn_down")
    k_new = z3[:, :, lay["sk"]:lay["sk"] + lay["swa_kv"] * lay["swa_hd"]]
    v_new = z3[:, :, lay["sv"]:lay["sv"] + lay["swa_kv"] * lay["swa_hd"]]
    return x, (gla_st, k_new, v_new, conv_st, ffn_st)


def kernel(x_prompt, x_sample, state_gla, cache_swa_k, cache_swa_v, state_conv, state_ffn, cache_mem_k, cache_mem_v, mem_prompt, norm_mix, w_in, gla_gate_up, gla_gate_b, gla_norm, swa_sinks, rel_bias, conv_w, w_branch, w_out, norm_x, wx_q, wx_k, wx_v, wx_o, norm_ffn, ffn_up, ffn_conv_w, ffn_conv_b, ffn_down, norm_final):
    Bp, Tp, D = x_prompt.shape
    Bs, Ts, _ = x_sample.shape
    depth = w_in.shape[0]
    gla_heads, gla_dk, gla_dv = state_gla.shape[2:]
    swa_kv, swa_hd = cache_swa_k.shape[3:]
    swa_heads = swa_sinks.shape[1]
    x_heads, x_hd = cache_mem_k.shape[3:]
    n_mem = mem_prompt.shape[1]
    mix = conv_w.shape[2]
    dff = ffn_down.shape[1]
    rank = gla_gate_up.shape[1]
    gqk = gla_heads * gla_dk
    sqw, skw = swa_heads * swa_hd, swa_kv * swa_hd
    win = cache_swa_k.shape[2]
    assert win == WINDOW and mix == gla_heads * gla_dv == sqw

    tn = 512
    a_w = 2 * gqk + 2 * mix
    s_w = sqw + 2 * skw
    s_pad = _round_up(a_w + s_w + LANE, tn) - (a_w + s_w + LANE)
    lay = dict(mix=mix, gla_heads=gla_heads, swa_kv=swa_kv, swa_hd=swa_hd, x_heads=x_heads,
               gq=0, gk=gqk, gv=2 * gqk, gr=2 * gqk + mix,
               sq=a_w, sk=a_w + sqw, sv=a_w + sqw + skw, glr=a_w + s_w)
    c0 = a_w + s_w + LANE + s_pad
    lay.update(cb=c0, cc=c0 + mix, ch=c0 + 2 * mix, gates=c0 + 3 * mix)
    dffp = _round_up(dff, tn)
    lay["dffp"] = dffp
    o_glr = a_w
    o_swa = a_w + rank
    o_rest = o_swa + s_w

    def prep_layer(i):
        w = w_in[i]
        zpad = jnp.zeros((D, LANE - rank + s_pad), w.dtype)
        w_in_r = jnp.concatenate([w[:, :a_w], w[:, o_swa:o_rest], w[:, o_glr:o_swa], zpad, w[:, o_rest:]],
                                 axis=1).astype(BF16)
        up = ffn_up[i]
        fpad = jnp.zeros((D, dffp - dff), up.dtype)
        ffn_up_r = jnp.concatenate([up[:, :dff], fpad, up[:, dff:], fpad], axis=1).astype(BF16)
        return dict(
            norm_mix=norm_mix[i], w_in=w_in_r,
            gate_up=jnp.pad(gla_gate_up[i], ((0, LANE - rank), (0, 0))), gate_b=gla_gate_b[i],
            gla_norm=gla_norm[i], sinks=swa_sinks[i], conv_w=conv_w[i],
            w_branch=w_branch[i].astype(BF16), w_out=w_out[i].astype(BF16),
            norm_x=norm_x[i], wx_q=wx_q[i].astype(BF16),
            wx_kv=jnp.concatenate([wx_k[i], wx_v[i]], axis=1).astype(BF16), wx_o=wx_o[i].astype(BF16),
            norm_ffn=norm_ffn[i], ffn_up=ffn_up_r,
            ffn_conv_w=jnp.pad(ffn_conv_w[i], ((0, 0), (0, dffp - dff))),
            ffn_conv_b=jnp.pad(ffn_conv_b[i], (0, dffp - dff)).reshape(1, dffp),
            ffn_down=jnp.pad(ffn_down[i], ((0, dffp - dff), (0, 0))).astype(BF16))

    bias_p = _swa_bias_table(rel_bias, min(Tp, WINDOW), WINDOW)
    bias_s = _swa_bias_table(rel_bias, Ts, win)
    act_p = BF16
    act_s = BF16 if Ts % 16 == 0 else F32

    def first8(st):
        return jnp.pad(st, ((0, 0), (SUBLANE - (CONV_W - 1), 0), (0, 0)))

    xp = x_prompt.reshape(Bp * Tp, D)
    xs = x_sample.reshape(Bs * Ts, D)
    mem2 = mem_prompt.reshape(Bp * n_mem, D)
    xw = x_heads * x_hd
    st_p, st_s = [], []
    for i in range(depth):
        wts = prep_layer(i)
        mem_kv = norm_matmul(mem2, None, wts["wx_kv"], F32, name="mem_kv").reshape(Bp, n_mem, 2 * xw)
        mem_k, mem_v = mem_kv[:, :, :xw], mem_kv[:, :, xw:]
        xp, sp = _trunk_layer(
            xp, Bp, Tp, wts, lay, jnp.zeros((Bp, gla_heads, gla_dv, gla_dk), F32), None,
            jnp.zeros((Bp, SUBLANE, mix), F32), jnp.zeros((Bp, SUBLANE, dffp), F32),
            mem_k, mem_v, bias_p, act_p)
        xs, ss = _trunk_layer(
            xs, Bs, Ts, wts, lay, jnp.swapaxes(state_gla[i], -1, -2),
            (cache_swa_k[i].reshape(Bs, win, skw), cache_swa_v[i].reshape(Bs, win, skw)),
            first8(state_conv[i]), first8(jnp.pad(state_ffn[i], ((0, 0), (0, 0), (0, dffp - dff)))),
            cache_mem_k[i].reshape(Bs, n_mem, xw), cache_mem_v[i].reshape(Bs, n_mem, xw), bias_s, act_s)
        st_p.append(dict(
            gla=jnp.swapaxes(sp[0], -1, -2),
            k=sp[1][:, Tp - WINDOW:].astype(F32).reshape(Bp, WINDOW, swa_kv, swa_hd),
            v=sp[2][:, Tp - WINDOW:].astype(F32).reshape(Bp, WINDOW, swa_kv, swa_hd),
            conv=sp[3][:, SUBLANE - (CONV_W - 1):], ffn=sp[4][:, SUBLANE - (CONV_W - 1):, :dff],
            mem_k=mem_k.reshape(Bp, n_mem, x_heads, x_hd), mem_v=mem_v.reshape(Bp, n_mem, x_heads, x_hd)))
        kc = jnp.concatenate([cache_swa_k[i].reshape(Bs, win, skw), ss[1].astype(F32)], axis=1)[:, Ts:]
        vc = jnp.concatenate([cache_swa_v[i].reshape(Bs, win, skw), ss[2].astype(F32)], axis=1)[:, Ts:]
        st_s.append(dict(
            gla=jnp.swapaxes(ss[0], -1, -2),
            k=kc.reshape(Bs, win, swa_kv, swa_hd), v=vc.reshape(Bs, win, swa_kv, swa_hd),
            conv=ss[3][:, SUBLANE - (CONV_W - 1):], ffn=ss[4][:, SUBLANE - (CONV_W - 1):, :dff]))
    y_prompt = final_norm(xp, norm_final).reshape(Bp, Tp, D)
    y_sample = final_norm(xs, norm_final).reshape(Bs, Ts, D)

    def stack(lst, key):
        return jnp.stack([s[key] for s in lst])

    return (y_prompt, y_sample, stack(st_p, "gla"), stack(st_s, "gla"), stack(st_p, "k"), stack(st_p, "v"),
            stack(st_s, "k"), stack(st_s, "v"), stack(st_p, "conv"), stack(st_s, "conv"),
            stack(st_p, "ffn"), stack(st_s, "ffn"), stack(st_p, "mem_k"), stack(st_p, "mem_v"))
```

```python
import functools
import math

import jax
import jax.numpy as jnp
from jax import lax
from jax.experimental import pallas as pl
from jax.experimental.pallas import tpu as pltpu

F32 = jnp.float32
BF16 = jnp.bfloat16

EPS = 1e-6
GLA_TAU = 16.0
GLA_CHUNK = 64
GLA_SUB = 16
WINDOW = 128
N_BUCKETS = 32
MAX_DIST = 128
CONV_W = 3
NEG = -1e30

LANE = 128
SUBLANE = 8
VMEM_LIMIT = 56 * 1024 * 1024


def _cparams(sem):
    return pltpu.CompilerParams(dimension_semantics=sem, vmem_limit_bytes=VMEM_LIMIT)


def _tile(n, pref, unit):
    if n <= pref:
        return n
    t = (pref // unit) * unit
    while t > unit and n % t:
        t -= unit
    assert n % t == 0, (n, pref, unit)
    return t


def _mxu_dtype(rows):
    return BF16 if rows % 16 == 0 else F32


def _sigmoid(x):
    return 1.0 / (1.0 + jnp.exp(-x))


def _dot(a, b):
    return jnp.dot(a, b, preferred_element_type=F32)


def _dot_nt(a, b):
    return lax.dot_general(a, b, (((1,), (1,)), ((), ())), preferred_element_type=F32)


def _rmsnorm_bf16(x_ref, g_ref):
    x = x_ref[...].astype(F32)
    ms = jnp.mean(x * x, axis=-1, keepdims=True)
    return (x * lax.rsqrt(ms + EPS) * g_ref[...]).astype(BF16)


def _w_spec(w, layer, tn, col0=0):
    k = w.shape[-2]
    if w.ndim == 3:
        return pl.BlockSpec((None, k, tn), lambda i, j: (layer, 0, col0 + j))
    return pl.BlockSpec((k, tn), lambda i, j: (0, col0 + j))


def _norm_mm_kernel(x_ref, g_ref, w_ref, o_ref, h_ref):
    @pl.when(pl.program_id(1) == 0)
    def _():
        h_ref[...] = _rmsnorm_bf16(x_ref, g_ref)

    o_ref[...] = _dot(h_ref[...], w_ref[...].astype(BF16)).astype(o_ref.dtype)


def _plain_mm_kernel(x_ref, w_ref, o_ref, h_ref):
    @pl.when(pl.program_id(1) == 0)
    def _():
        h_ref[...] = x_ref[...].astype(BF16)

    o_ref[...] = _dot(h_ref[...], w_ref[...].astype(BF16)).astype(o_ref.dtype)


def _res_mm_kernel(a_ref, w_ref, r_ref, o_ref):
    o_ref[...] = r_ref[...] + _dot(a_ref[...].astype(BF16), w_ref[...].astype(BF16))


def norm_matmul(x, gain, w, out_dtype, layer=0, n=None, tm=1024, tn=512, name="norm_mm"):
    M, K = x.shape
    N = w.shape[-1] if n is None else n
    tm = _tile(M, tm, SUBLANE)
    tn = _tile(N, tn, LANE)
    grid = (M // tm, N // tn)
    x_spec = pl.BlockSpec((tm, K), lambda i, j: (i, 0))
    o_spec = pl.BlockSpec((tm, tn), lambda i, j: (i, j))
    common = dict(grid=grid, out_specs=o_spec, out_shape=jax.ShapeDtypeStruct((M, N), out_dtype),
                  scratch_shapes=[pltpu.VMEM((tm, K), BF16)],
                  compiler_params=_cparams(("parallel", "arbitrary")), name=name)
    if gain is None:
        return pl.pallas_call(_plain_mm_kernel, in_specs=[x_spec, _w_spec(w, layer, tn)], **common)(x, w)
    g_spec = pl.BlockSpec((1, K), lambda i, j: (0, 0))
    return pl.pallas_call(_norm_mm_kernel, in_specs=[x_spec, g_spec, _w_spec(w, layer, tn)],
                          **common)(x, gain.reshape(1, K), w)


def res_matmul(a, w, res, layer=0, tm=1024, tn=512, name="res_mm"):
    M, K = a.shape
    N = w.shape[-1]
    tm = _tile(M, tm, SUBLANE)
    tn = _tile(N, tn, LANE)
    return pl.pallas_call(
        _res_mm_kernel, grid=(M // tm, N // tn),
        in_specs=[pl.BlockSpec((tm, K), lambda i, j: (i, 0)), _w_spec(w, layer, tn),
                  pl.BlockSpec((tm, tn), lambda i, j: (i, j))],
        out_specs=pl.BlockSpec((tm, tn), lambda i, j: (i, j)),
        out_shape=jax.ShapeDtypeStruct((M, N), F32),
        compiler_params=_cparams(("parallel", "arbitrary")), name=name)(a, w, res)


def _final_norm_kernel(x_ref, g_ref, o_ref):
    x = x_ref[...]
    ms = jnp.mean(x * x, axis=-1, keepdims=True)
    o_ref[...] = x * lax.rsqrt(ms + EPS) * g_ref[...]


def final_norm(x, gain, tm=512):
    M, K = x.shape
    tm = _tile(M, tm, SUBLANE)
    return pl.pallas_call(
        _final_norm_kernel, grid=(M // tm,),
        in_specs=[pl.BlockSpec((tm, K), lambda i: (i, 0)), pl.BlockSpec((1, K), lambda i: (0, 0))],
        out_specs=pl.BlockSpec((tm, K), lambda i: (i, 0)),
        out_shape=jax.ShapeDtypeStruct((M, K), F32),
        compiler_params=_cparams(("parallel",)), name="final_norm")(x, gain.reshape(1, K))


def _gla_kernel(*refs, lin, lp, n_chunks, scale, has_s0):
    if has_s0:
        q_ref, k_ref, v_ref, r_ref, glr_ref, up_ref, gb_ref, gn_ref, s0_ref, o_ref, st_ref, s_scr = refs
    else:
        q_ref, k_ref, v_ref, r_ref, glr_ref, up_ref, gb_ref, gn_ref, o_ref, st_ref, s_scr = refs
    t = pl.program_id(2)
    dk = q_ref.shape[-1]
    dv = v_ref.shape[-1]
    rank = up_ref.shape[0]
    c = min(GLA_SUB, lp)
    nsub = lp // c
    md = _mxu_dtype(lp)

    @pl.when(t == 0)
    def _():
        if has_s0:
            s_scr[...] = s0_ref[...].T
        else:
            s_scr[...] = jnp.zeros_like(s_scr)

    ones = jnp.ones((dk, LANE), md)
    row = lax.broadcasted_iota(jnp.int32, (lp, dk), 0)
    rsub = lax.broadcasted_iota(jnp.int32, (c, dk), 0)
    lsub = lax.broadcasted_iota(jnp.int32, (c, LANE), 1)
    up = jnp.concatenate([up_ref[...], jnp.zeros((LANE - rank, dk), F32)], axis=0).astype(md)

    def pad(x):
        if lin == lp:
            return x
        return jnp.concatenate([x, jnp.zeros((lp - lin, x.shape[1]), x.dtype)], axis=0)

    def lane_rows(x):
        if lp >= LANE:
            return x
        return jnp.concatenate([x, jnp.zeros((LANE - lp, x.shape[1]), x.dtype)], axis=0)

    def chunk(ci, carry):
        c0 = pl.multiple_of(ci * lin, lin)
        rows = pl.ds(c0, lin)
        q = pad(q_ref[rows, :].astype(F32)) * scale
        k = pad(k_ref[rows, :].astype(F32))
        v = pad(v_ref[rows, :].astype(F32))
        z = _dot(pad(glr_ref[rows, :].astype(F32)).astype(md), up) + gb_ref[...]
        lg = -(jnp.maximum(-z, 0.0) + jnp.log1p(jnp.exp(-jnp.abs(z)))) / GLA_TAU
        if lin != lp:
            lg = jnp.where(row < lin, lg, 0.0)
        b = lg
        sh = 1
        while sh < lp:
            b = b + jnp.where(row >= sh, pltpu.roll(b, sh, 0), 0.0)
            sh *= 2
        st = s_scr[...]
        o = _dot_nt((q * jnp.exp(b)).astype(md), st.astype(md))
        a_rows = []
        for i in range(nsub):
            lo = i * c
            qb, kb, bb = q[lo:lo + c], k[lo:lo + c], b[lo:lo + c]
            ws = []
            for s in range(c):
                d = bb - bb[s:s + 1, :]
                ws.append(qb * kb[s:s + 1, :] * jnp.exp(jnp.where(rsub >= s, d, NEG)))
            rsum = _dot(jnp.concatenate(ws, axis=0).astype(md), ones)
            a_i = jnp.zeros((c, LANE), F32)
            for s in range(c):
                a_i = jnp.where(lsub == lo + s, rsum[s * c:(s + 1) * c], a_i)
            if i > 0:
                ref = b[lo - 1:lo, :]
                qt = qb * jnp.exp(bb - ref)
                kt = k * jnp.exp(jnp.where(row < lo, ref - b, NEG))
                a_i = a_i + _dot_nt(qt.astype(md), lane_rows(kt).astype(md))
            a_rows.append(a_i)
        a = jnp.concatenate(a_rows, axis=0)
        vk = lane_rows(v)
        o = o + _dot(a.astype(md), vk.astype(md))
        bend = b[lp - 1:lp, :]
        kd = lane_rows(k * jnp.exp(bend - b))
        s_scr[...] = st * jnp.exp(bend) + _dot(vk.T.astype(md), kd.astype(md))
        o = o[:lin]
        ms = jnp.mean(o * o, axis=-1, keepdims=True)
        y = o * lax.rsqrt(ms + EPS) * gn_ref[...]
        r = r_ref[rows, :].astype(F32)
        o_ref[rows, :] = (y * (r * _sigmoid(r))).astype(o_ref.dtype)
        return carry

    lax.fori_loop(0, n_chunks, chunk, 0)

    @pl.when(t == pl.num_programs(2) - 1)
    def _():
        st_ref[...] = s_scr[...].T


def gla(zq, offs, zg, glr_off, gate_up, gate_b, gnorm, s0, layer, heads, dk, dv, out_dtype):
    B, T, _ = zq.shape
    depth, rank, _ = gate_up.shape
    lin = min(GLA_CHUNK, T)
    assert T % lin == 0
    lp = max(lin, 16)
    tt = _tile(T, 512, lin)
    n_chunks = tt // lin
    qo, ko, vo, ro = offs

    def col(off, width):
        return lambda b, h, t: (b, t, off // width + h)

    in_specs = [pl.BlockSpec((None, tt, dk), col(qo, dk)),
                pl.BlockSpec((None, tt, dk), col(ko, dk)),
                pl.BlockSpec((None, tt, dv), col(vo, dv)),
                pl.BlockSpec((None, tt, dv), col(ro, dv)),
                pl.BlockSpec((None, tt, LANE), lambda b, h, t: (b, t, glr_off // LANE)),
                pl.BlockSpec((None, rank, dk), lambda b, h, t: (layer, 0, h)),
                pl.BlockSpec((None, 1, dk), lambda b, h, t: (layer, 0, h)),
                pl.BlockSpec((None, 1, dv), lambda b, h, t: (layer, 0, 0))]
    args = [zq, zq, zq, zq, zg, gate_up, gate_b.reshape(depth, 1, -1), gnorm.reshape(depth, 1, -1)]
    if s0 is not None:
        in_specs.append(pl.BlockSpec((None, None, None, dk, dv), lambda b, h, t: (layer, b, h, 0, 0)))
        args.append(s0)
    kern = functools.partial(_gla_kernel, lin=lin, lp=lp, n_chunks=n_chunks, scale=dk ** -0.5,
                             has_s0=s0 is not None)
    return pl.pallas_call(
        kern, grid=(B, heads, T // tt), in_specs=in_specs,
        out_specs=[pl.BlockSpec((None, tt, dv), lambda b, h, t: (b, t, h)),
                   pl.BlockSpec((None, None, dk, dv), lambda b, h, t: (b, h, 0, 0))],
        out_shape=[jax.ShapeDtypeStruct((B, T, heads * dv), out_dtype),
                   jax.ShapeDtypeStruct((B, heads, dk, dv), F32)],
        scratch_shapes=[pltpu.VMEM((dv, dk), F32)],
        compiler_params=_cparams(("parallel", "parallel", "arbitrary")), name="gla",
    )(*args)


def _swa_kernel(sink_ref, q_ref, kp_ref, kc_ref, vp_ref, vc_ref, bias_ref, o_ref, *,
                layer, heads, group, hd, first_has_no_prev, scale):
    tq = q_ref.shape[0]
    p = kp_ref.shape[0]
    md = _mxu_dtype(tq)
    k = jnp.concatenate([kp_ref[...].astype(md), kc_ref[...].astype(md)], axis=0)
    v = jnp.concatenate([vp_ref[...].astype(md), vc_ref[...].astype(md)], axis=0)
    if first_has_no_prev:
        col = lax.broadcasted_iota(jnp.int32, (tq, p + tq), 1)
        pen = jnp.where((pl.program_id(1) == 0) & (col < p), NEG, 0.0)
    outs = []
    for h in range(heads):
        kv = h // group
        qh = q_ref[:, h * hd:(h + 1) * hd].astype(md)
        s = _dot_nt(qh, k[:, kv * hd:(kv + 1) * hd]) * scale + bias_ref[h]
        if first_has_no_prev:
            s = s + pen
        sink = sink_ref[layer, h]
        m = jnp.maximum(jnp.max(s, axis=-1, keepdims=True), sink)
        e = jnp.exp(s - m)
        den = jnp.sum(e, axis=-1, keepdims=True) + jnp.exp(sink - m)
        outs.append(_dot(e.astype(md), v[:, kv * hd:(kv + 1) * hd]) / den)
    per = LANE // hd
    for j in range(heads // per):
        o_ref[:, j * LANE:(j + 1) * LANE] = jnp.concatenate(
            outs[j * per:(j + 1) * per], axis=-1).astype(o_ref.dtype)


def _t5_bucket(dist):
    n = jnp.maximum(dist, 0)
    max_exact = N_BUCKETS // 2
    nf = jnp.maximum(n, 1).astype(F32)
    large = max_exact + (jnp.log(nf / max_exact) / math.log(MAX_DIST / max_exact)
                         * (N_BUCKETS - max_exact)).astype(jnp.int32)
    large = jnp.minimum(large, N_BUCKETS - 1)
    return jnp.where(n < max_exact, n, large)


def _swa_bias_table(rel_bias, tq, p):
    i = jnp.arange(tq)[:, None]
    j = jnp.arange(p + tq)[None, :]
    dist = p + i - j
    valid = (dist >= 0) & (dist <= WINDOW)
    bias = rel_bias[_t5_bucket(dist)].astype(F32)
    return jnp.where(valid[None], jnp.transpose(bias, (2, 0, 1)), NEG)


def swa(z3, offs, prev_k, prev_v, sinks, bias, layer, kv_heads, hd, tq, out_dtype):
    B, T, _ = z3.shape
    heads = sinks.shape[1]
    qw, kw = heads * hd, kv_heads * hd
    qo, ko, vo = offs
    nb = T // tq
    prompt = prev_k is None
    p = bias.shape[2] - tq
    if prompt:
        assert p == tq
        prev_specs = [pl.BlockSpec((None, p, kw), lambda b, n: (b, jnp.maximum(n - 1, 0), ko // kw)),
                      pl.BlockSpec((None, p, kw), lambda b, n: (b, jnp.maximum(n - 1, 0), vo // kw))]
        prev_k = prev_v = z3
    else:
        assert nb == 1
        prev_specs = [pl.BlockSpec((None, None, p, kw), lambda b, n: (layer, b, 0, 0))] * 2
    kern = functools.partial(_swa_kernel, layer=layer, heads=heads, group=heads // kv_heads, hd=hd,
                             first_has_no_prev=prompt, scale=hd ** -0.5)
    return pl.pallas_call(
        kern, grid=(B, nb),
        in_specs=[pl.BlockSpec(memory_space=pltpu.SMEM),
                  pl.BlockSpec((None, tq, qw), lambda b, n: (b, n, qo // qw)),
                  prev_specs[0],
                  pl.BlockSpec((None, tq, kw), lambda b, n: (b, n, ko // kw)),
                  prev_specs[1],
                  pl.BlockSpec((None, tq, kw), lambda b, n: (b, n, vo // kw)),
                  pl.BlockSpec(bias.shape, lambda b, n: (0, 0, 0))],
        out_specs=pl.BlockSpec((None, tq, qw), lambda b, n: (b, n, 0)),
        out_shape=jax.ShapeDtypeStruct((B, T, qw), out_dtype),
        compiler_params=_cparams(("parallel", "arbitrary")), name="swa",
    )(sinks, z3, prev_k, z3, prev_v, z3, bias)


def _conv_branch_kernel(cb_ref, cc_ref, ch_ref, prev_ref, w_ref, o_ref, st_ref, ext_ref):
    tt = cc_ref.shape[0]
    u = cc_ref[...].astype(F32) * ch_ref[...].astype(F32)

    @pl.when(pl.program_id(2) == 0)
    def _():
        ext_ref[0:SUBLANE, :] = prev_ref[...]

    ext_ref[SUBLANE:SUBLANE + tt, :] = u
    u1 = ext_ref[pl.ds(SUBLANE - 1, tt), :]
    u2 = ext_ref[pl.ds(SUBLANE - 2, tt), :]
    tail = u[tt - SUBLANE:tt]
    ext_ref[0:SUBLANE, :] = tail
    y = u2 * w_ref[0:1, :] + u1 * w_ref[1:2, :] + u * w_ref[2:3, :]
    o_ref[...] = (cb_ref[...].astype(F32) * y).astype(o_ref.dtype)
    st_ref[...] = tail


def conv_branch(z3, col_offs, prev8, conv_w, layer, out_dtype, tc=512, tt=1024):
    B, T, _ = z3.shape
    width = conv_w.shape[2]
    tc = _tile(width, tc, LANE)
    tt = _tile(T, tt, SUBLANE)
    assert T >= SUBLANE
    z_specs = [pl.BlockSpec((None, tt, tc), (lambda b, c, t, o=o: (b, t, o // tc + c))) for o in col_offs]
    st_spec = pl.BlockSpec((None, SUBLANE, tc), lambda b, c, t: (b, 0, c))
    return pl.pallas_call(
        _conv_branch_kernel, grid=(B, width // tc, T // tt),
        in_specs=z_specs + [st_spec, pl.BlockSpec((None, CONV_W, tc), lambda b, c, t: (layer, 0, c))],
        out_specs=[pl.BlockSpec((None, tt, tc), lambda b, c, t: (b, t, c)), st_spec],
        out_shape=[jax.ShapeDtypeStruct((B, T, width), out_dtype),
                   jax.ShapeDtypeStruct((B, SUBLANE, width), F32)],
        scratch_shapes=[pltpu.VMEM((tt + SUBLANE, tc), F32)],
        compiler_params=_cparams(("parallel", "parallel", "arbitrary")), name="conv_branch",
    )(z3, z3, z3, prev8, conv_w)


def _ffn_up_kernel(x_ref, gain_ref, wu_ref, wg_ref, prev_ref, cw_ref, cb_ref, act_ref, st_ref,
                   h_scr, ext_scr, carry_scr, *, seq_len):
    i, j = pl.program_id(0), pl.program_id(1)
    tm, tn = act_ref.shape

    @pl.when(j == 0)
    def _():
        h_scr[...] = _rmsnorm_bf16(x_ref, gain_ref)

    h = h_scr[...]
    u = _dot(h, wu_ref[...])
    g = _dot(h, wg_ref[...])
    ext_scr[SUBLANE:SUBLANE + tm, :] = g
    if tm <= seq_len:
        blocks_per_seq = seq_len // tm
        starts = (i % blocks_per_seq) == 0

        @pl.when(starts)
        def _():
            ext_scr[0:SUBLANE, :] = prev_ref[0]

        @pl.when(jnp.logical_not(starts))
        def _():
            ext_scr[0:SUBLANE, :] = carry_scr[j]

        g1 = ext_scr[pl.ds(SUBLANE - 1, tm), :]
        g2 = ext_scr[pl.ds(SUBLANE - 2, tm), :]
        tail = g[tm - SUBLANE:tm]
        carry_scr[j] = tail
        st_ref[0] = tail
    else:
        nseq = tm // seq_len
        ext_scr[0:SUBLANE, :] = jnp.zeros((SUBLANE, tn), F32)
        f1 = ext_scr[pl.ds(SUBLANE - 1, tm), :]
        f2 = ext_scr[pl.ds(SUBLANE - 2, tm), :]
        tix = lax.rem(lax.broadcasted_iota(jnp.int32, (tm, tn), 0), seq_len)
        s1 = jnp.broadcast_to(prev_ref[:, SUBLANE - 1:SUBLANE, :], (nseq, seq_len, tn)).reshape(tm, tn)
        s0 = jnp.broadcast_to(prev_ref[:, SUBLANE - 2:SUBLANE - 1, :], (nseq, seq_len, tn)).reshape(tm, tn)
        g1 = jnp.where(tix == 0, s1, f1)
        g2 = jnp.where(tix == 0, s0, jnp.where(tix == 1, s1, f2))
        st_ref[...] = g.reshape(nseq, seq_len, tn)[:, seq_len - SUBLANE:, :]
    gc = g2 * cw_ref[0:1, :] + g1 * cw_ref[1:2, :] + g * cw_ref[2:3, :] + cb_ref[...]
    act_ref[...] = (gc * _sigmoid(gc) * u).astype(act_ref.dtype)


def ffn_first_half(x, gain, w_up, prev8, conv_w, conv_b, B, T, dff, out_dtype, tm=1024, tn=512):
    M, K = x.shape
    dffp = w_up.shape[1] // 2
    tn = _tile(dffp, tn, LANE)
    nj = dffp // tn
    if T >= tm:
        tm = _tile(T, tm, SUBLANE)
        nseq = 1
        prev_map = lambda i, j: (i // (T // tm), 0, j)
    else:
        assert T % SUBLANE == 0 and T >= SUBLANE
        nseq = _tile(B, max(tm // T, 1), 1)
        tm = nseq * T
        prev_map = lambda i, j: (i, 0, j)
    n_st = (M // tm) * nseq
    act, st = pl.pallas_call(
        functools.partial(_ffn_up_kernel, seq_len=T), grid=(M // tm, nj),
        in_specs=[pl.BlockSpec((tm, K), lambda i, j: (i, 0)),
                  pl.BlockSpec((1, K), lambda i, j: (0, 0)),
                  pl.BlockSpec((K, tn), lambda i, j: (0, j)),
                  pl.BlockSpec((K, tn), lambda i, j: (0, nj + j)),
                  pl.BlockSpec((nseq, SUBLANE, tn), prev_map),
                  pl.BlockSpec((CONV_W, tn), lambda i, j: (0, j)),
                  pl.BlockSpec((1, tn), lambda i, j: (0, j))],
        out_specs=[pl.BlockSpec((tm, tn), lambda i, j: (i, j)),
                   pl.BlockSpec((nseq, SUBLANE, tn), lambda i, j: (i, 0, j))],
        out_shape=[jax.ShapeDtypeStruct((M, dff), out_dtype),
                   jax.ShapeDtypeStruct((n_st, SUBLANE, dffp), F32)],
        scratch_shapes=[pltpu.VMEM((tm, K), BF16), pltpu.VMEM((tm + SUBLANE, tn), F32),
                        pltpu.VMEM((nj, SUBLANE, tn), F32)],
        compiler_params=_cparams(("arbitrary", "arbitrary")), name="ffn_up",
    )(x, gain.reshape(1, K), w_up, w_up, prev8, conv_w, conv_b)
    return act, st.reshape(B, n_st // B, SUBLANE, dffp)[:, -1]


def _merge_kernel(a_ref, b_ref, c_ref, wb_ref, g0_ref, g1_ref, g2_ref, o_ref):
    def term(g_ref, x_ref, n):
        return _sigmoid(g_ref[...].astype(F32)) * _dot(x_ref[...].astype(BF16), wb_ref[n].astype(BF16))

    acc = term(g0_ref, a_ref, 0) + term(g1_ref, b_ref, 1)
    o_ref[...] = (acc + term(g2_ref, c_ref, 2)).astype(o_ref.dtype)


def merge(br_a, br_b, br_c, wb, layer, z, gate_off, out_dtype, tm=1024, tn=512):
    M, mix = br_a.shape
    d = wb.shape[3]
    tm = _tile(M, tm, SUBLANE)
    tn = _tile(d, tn, LANE)
    br_spec = pl.BlockSpec((tm, mix), lambda i, j: (i, 0))
    g_specs = [pl.BlockSpec((tm, tn), (lambda i, j, o=(gate_off + n * d) // tn: (i, o + j))) for n in range(3)]
    return pl.pallas_call(
        _merge_kernel, grid=(M // tm, d // tn),
        in_specs=[br_spec, br_spec, br_spec,
                  pl.BlockSpec((None, 3, mix, tn), lambda i, j: (layer, 0, 0, j))] + g_specs,
        out_specs=pl.BlockSpec((tm, tn), lambda i, j: (i, j)),
        out_shape=jax.ShapeDtypeStruct((M, d), out_dtype),
        compiler_params=_cparams(("parallel", "arbitrary")), name="merge",
    )(br_a, br_b, br_c, wb, z, z, z)


def _xattn_kernel(q_ref, k_ref, v_ref, o_ref, *, heads, hd, scale):
    md = _mxu_dtype(q_ref.shape[0])
    for h in range(heads):
        sl = slice(h * hd, (h + 1) * hd)
        s = _dot_nt(q_ref[:, sl].astype(md), k_ref[:, sl].astype(md)) * scale
        e = jnp.exp(s - jnp.max(s, axis=-1, keepdims=True))
        den = jnp.sum(e, axis=-1, keepdims=True)
        o_ref[:, sl] = (_dot(e.astype(md), v_ref[:, sl].astype(md)) / den).astype(o_ref.dtype)


def xattn(q3, mem_k, mem_v, layer, heads, out_dtype, tq=512):
    B, T, w = q3.shape
    nm = mem_k.shape[-2]
    hd = w // heads
    tq = _tile(T, tq, SUBLANE)
    kern = functools.partial(_xattn_kernel, heads=heads, hd=hd, scale=hd ** -0.5)
    if mem_k.ndim == 4:
        m_spec = pl.BlockSpec((None, None, nm, w), lambda b, t: (layer, b, 0, 0))
    else:
        m_spec = pl.BlockSpec((None, nm, w), lambda b, t: (b, 0, 0))
    return pl.pallas_call(
        kern, grid=(B, T // tq),
        in_specs=[pl.BlockSpec((None, tq, w), lambda b, t: (b, t, 0)), m_spec, m_spec],
        out_specs=pl.BlockSpec((None, tq, w), lambda b, t: (b, t, 0)),
        out_shape=jax.ShapeDtypeStruct((B, T, w), out_dtype),
        compiler_params=_cparams(("parallel", "arbitrary")), name="xattn",
    )(q3, mem_k, mem_v)


def _round_up(n, m):
    return -(-n // m) * m


def kernel(x_prompt, x_sample, state_gla, cache_swa_k, cache_swa_v, state_conv, state_ffn, cache_mem_k, cache_mem_v, mem_prompt, norm_mix, w_in, gla_gate_up, gla_gate_b, gla_norm, swa_sinks, rel_bias, conv_w, w_branch, w_out, norm_x, wx_q, wx_k, wx_v, wx_o, norm_ffn, ffn_up, ffn_conv_w, ffn_conv_b, ffn_down, norm_final):
    Bp, Tp, D = x_prompt.shape
    Bs, Ts, _ = x_sample.shape
    depth = w_in.shape[0]
    gla_heads, gla_dk, gla_dv = state_gla.shape[2:]
    swa_kv, swa_hd = cache_swa_k.shape[3:]
    swa_heads = swa_sinks.shape[1]
    x_heads, x_hd = cache_mem_k.shape[3:]
    n_mem = mem_prompt.shape[1]
    mix = conv_w.shape[2]
    dff = ffn_down.shape[1]
    rank = gla_gate_up.shape[1]
    gqk = gla_heads * gla_dk
    sqw, skw = swa_heads * swa_hd, swa_kv * swa_hd
    win = cache_swa_k.shape[2]
    xw = x_heads * x_hd
    assert win == WINDOW and mix == gla_heads * gla_dv == sqw

    tn = 512
    a_w = 2 * gqk + 2 * mix
    s_w = sqw + 2 * skw
    s_pad = _round_up(s_w + LANE, tn) - (s_w + LANE)
    a_offs = (0, gqk, 2 * gqk, 2 * gqk + mix)
    sq, sk, sv, glr = 0, sqw, sqw + skw, s_w
    cb = s_w + LANE + s_pad
    cc, ch, gates = cb + mix, cb + 2 * mix, cb + 3 * mix
    o_glr, o_swa = a_w, a_w + rank
    o_rest = o_swa + s_w
    dffp = _round_up(dff, tn)

    def regroup_in(w):
        parts = [w[:, o_swa:o_rest], w[:, o_glr:o_swa], jnp.zeros((D, LANE - rank + s_pad), w.dtype), w[:, o_rest:]]
        return jnp.concatenate([p.astype(BF16) for p in parts], axis=1)

    def regroup_up(w):
        zpad = jnp.zeros((D, dffp - dff), BF16)
        return jnp.concatenate([w[:, :dff].astype(BF16), zpad, w[:, dff:].astype(BF16), zpad], axis=1)

    bias_p = _swa_bias_table(rel_bias, min(Tp, WINDOW), WINDOW)
    bias_s = _swa_bias_table(rel_bias, Ts, win)
    cache_k4 = cache_swa_k.reshape(depth, Bs, win, skw)
    cache_v4 = cache_swa_v.reshape(depth, Bs, win, skw)
    cmem_k4 = cache_mem_k.reshape(depth, Bs, n_mem, xw)
    cmem_v4 = cache_mem_v.reshape(depth, Bs, n_mem, xw)
    mem2 = mem_prompt.reshape(Bp * n_mem, D)

    def first8(st):
        return jnp.pad(st, ((0, 0), (SUBLANE - (CONV_W - 1), 0), (0, 0)))

    def trunk_layer(i, x, B, T, w_in_b, w_up_r, cw_ffn, cb_ffn, s0, swa_prev, conv_prev8, ffn_prev8,
                    mem_k, mem_v, swa_bias, act_dtype):
        M = B * T
        za = norm_matmul(x, norm_mix[i], w_in, act_dtype, layer=i, n=a_w, name="in_proj_a")
        zb = norm_matmul(x, norm_mix[i], w_in_b, act_dtype, name="in_proj_b")
        za3, zb3 = za.reshape(B, T, -1), zb.reshape(B, T, -1)
        br_a, gla_st = gla(za3, a_offs, zb3, glr, gla_gate_up, gla_gate_b, gla_norm, s0, i,
                           gla_heads, gla_dk, gla_dv, act_dtype)
        pk, pv = (None, None) if swa_prev is None else swa_prev
        br_b = swa(zb3, (sq, sk, sv), pk, pv, swa_sinks, swa_bias, i, swa_kv, swa_hd, min(T, WINDOW), act_dtype)
        br_c, conv_st = conv_branch(zb3, (cb, cc, ch), conv_prev8, conv_w, i, act_dtype)
        merged = merge(br_a.reshape(M, mix), br_b.reshape(M, mix), br_c.reshape(M, mix), w_branch, i,
                       zb, gates, act_dtype)
        x = res_matmul(merged, w_out, x, layer=i, name="out_proj")
        q = norm_matmul(x, norm_x[i], wx_q, act_dtype, layer=i, name="xq_proj")
        xo = xattn(q.reshape(B, T, xw), mem_k, mem_v, i, x_heads, act_dtype)
        x = res_matmul(xo.reshape(M, xw), wx_o, x, layer=i, name="xo_proj")
        act, ffn_st = ffn_first_half(x, norm_ffn[i], w_up_r, ffn_prev8, cw_ffn, cb_ffn, B, T, dff, BF16)
        x = res_matmul(act, ffn_down, x, layer=i, tm=1024, tn=256, name="ffn_down")
        k_new = zb3[:, T - min(T, WINDOW):, sk:sk + skw].astype(F32)
        v_new = zb3[:, T - min(T, WINDOW):, sv:sv + skw].astype(F32)
        return x, (gla_st, k_new, v_new, conv_st[:, SUBLANE - (CONV_W - 1):],
                   ffn_st[:, SUBLANE - (CONV_W - 1):, :dff])

    xp = x_prompt.reshape(Bp * Tp, D)
    xs = x_sample.reshape(Bs * Ts, D)
    act_p = BF16
    act_s = BF16 if Ts % 16 == 0 else F32
    st_p, st_s = [], []
    for i in range(depth):
        w_in_b = regroup_in(w_in[i])
        w_up_r = regroup_up(ffn_up[i])
        cw_ffn = jnp.pad(ffn_conv_w[i], ((0, 0), (0, dffp - dff)))
        cb_ffn = jnp.pad(ffn_conv_b[i], (0, dffp - dff)).reshape(1, dffp)
        mem_k = norm_matmul(mem2, None, wx_k, F32, layer=i, name="mem_k").reshape(Bp, n_mem, xw)
        mem_v = norm_matmul(mem2, None, wx_v, F32, layer=i, name="mem_v").reshape(Bp, n_mem, xw)
        xp, sp = trunk_layer(i, xp, Bp, Tp, w_in_b, w_up_r, cw_ffn, cb_ffn, None, None,
                             jnp.zeros((Bp, SUBLANE, mix), F32), jnp.zeros((Bp, SUBLANE, dffp), F32),
                             mem_k, mem_v, bias_p, act_p)
        xs, ss = trunk_layer(i, xs, Bs, Ts, w_in_b, w_up_r, cw_ffn, cb_ffn, state_gla, (cache_k4, cache_v4),
                             first8(state_conv[i]),
                             first8(jnp.pad(state_ffn[i], ((0, 0), (0, 0), (0, dffp - dff)))),
                             cmem_k4, cmem_v4, bias_s, act_s)
        st_p.append(dict(gla=sp[0], k=sp[1].reshape(Bp, WINDOW, swa_kv, swa_hd),
                         v=sp[2].reshape(Bp, WINDOW, swa_kv, swa_hd), conv=sp[3], ffn=sp[4],
                         mem_k=mem_k.reshape(Bp, n_mem, x_heads, x_hd),
                         mem_v=mem_v.reshape(Bp, n_mem, x_heads, x_hd)))
        kc = jnp.concatenate([cache_k4[i], ss[1]], axis=1)[:, Ts:]
        vc = jnp.concatenate([cache_v4[i], ss[2]], axis=1)[:, Ts:]
        st_s.append(dict(gla=ss[0], k=kc.reshape(Bs, win, swa_kv, swa_hd), v=vc.reshape(Bs, win, swa_kv, swa_hd),
                         conv=ss[3], ffn=ss[4]))
    y_prompt = final_norm(xp, norm_final).reshape(Bp, Tp, D)
    y_sample = final_norm(xs, norm_final).reshape(Bs, Ts, D)

    def stack(lst, key):
        return jnp.stack([s[key] for s in lst])

    return (y_prompt, y_sample, stack(st_p, "gla"), stack(st_s, "gla"), stack(st_p, "k"), stack(st_p, "v"),
            stack(st_s, "k"), stack(st_s, "v"), stack(st_p, "conv"), stack(st_s, "conv"),
            stack(st_p, "ffn"), stack(st_s, "ffn"), stack(st_p, "mem_k"), stack(st_p, "mem_v"))
```

```python
import functools
import math

import jax
import jax.numpy as jnp
from jax import lax
from jax.experimental import pallas as pl
from jax.experimental.pallas import tpu as pltpu

F32 = jnp.float32
BF16 = jnp.bfloat16

EPS = 1e-6
GLA_TAU = 16.0
GLA_CHUNK = 64
GLA_SUB = 16
WINDOW = 128
N_BUCKETS = 32
MAX_DIST = 128
CONV_W = 3
NEG = -1e30

LANE = 128
SUBLANE = 8
MXU_COLS = 256
VMEM_LIMIT = 56 * 1024 * 1024


def _cparams(sem):
    return pltpu.CompilerParams(dimension_semantics=sem, vmem_limit_bytes=VMEM_LIMIT)


def _tile(n, pref, unit):
    if n <= pref:
        return n
    t = (pref // unit) * unit
    while t > unit and n % t:
        t -= unit
    assert n % t == 0, (n, pref, unit)
    return t


def _mxu_dtype(rows):
    return BF16 if rows % 16 == 0 else F32


def _sigmoid(x):
    return 1.0 / (1.0 + jnp.exp(-x))


def _dot(a, b):
    return jnp.dot(a, b, preferred_element_type=F32)


def _dot_nt(a, b):
    return lax.dot_general(a, b, (((1,), (1,)), ((), ())), preferred_element_type=F32)


def _col_chunks(n):
    w = MXU_COLS if n % MXU_COLS == 0 else n
    return [(c, w) for c in range(0, n, w)]


def _pad_rows(x, rows):
    if x.shape[0] == rows:
        return x
    return jnp.concatenate([x, jnp.zeros((rows - x.shape[0], x.shape[1]), x.dtype)], axis=0)


def _norm_kernel(x_ref, g_ref, o_ref):
    x = x_ref[...]
    ms = jnp.mean(x * x, axis=-1, keepdims=True)
    o_ref[...] = (x * lax.rsqrt(ms + EPS) * g_ref[...]).astype(o_ref.dtype)


def rmsnorm(x, gain, out_dtype, tm=512):
    M, K = x.shape
    tm = _tile(M, tm, SUBLANE)
    return pl.pallas_call(
        _norm_kernel, grid=(M // tm,),
        in_specs=[pl.BlockSpec((tm, K), lambda i: (i, 0)), pl.BlockSpec((1, K), lambda i: (0, 0))],
        out_specs=pl.BlockSpec((tm, K), lambda i: (i, 0)),
        out_shape=jax.ShapeDtypeStruct((M, K), out_dtype),
        compiler_params=_cparams(("parallel",)), name="rmsnorm")(x, gain.reshape(1, K))


def _mm_kernel(a_ref, w_ref, o_ref):
    a = a_ref[...].astype(BF16)
    for c, w in _col_chunks(o_ref.shape[1]):
        o_ref[:, c:c + w] = _dot(a, w_ref[:, c:c + w].astype(BF16)).astype(o_ref.dtype)


def _mm_nt_kernel(a_ref, wt_ref, o_ref):
    a = a_ref[...]
    for c, w in _col_chunks(o_ref.shape[1]):
        o_ref[:, c:c + w] = _dot_nt(a, wt_ref[c:c + w, :].astype(BF16)).astype(o_ref.dtype)


def _mm_nt_gate_kernel(a_ref, wt_ref, wglr_ref, up_ref, gb_ref, o_ref, lg_ref):
    _mm_nt_kernel(a_ref, wt_ref, o_ref)

    @pl.when(pl.program_id(1) == 0)
    def _():
        glr = _dot_nt(a_ref[...], _pad_rows(wglr_ref[...], LANE).astype(BF16))
        z = _dot(glr.astype(BF16), _pad_rows(up_ref[...], LANE).astype(BF16)) + gb_ref[...]
        lg_ref[...] = -(jnp.maximum(-z, 0.0) + jnp.log1p(jnp.exp(-jnp.abs(z)))) / GLA_TAU


def _res_mm_kernel(a_ref, w_ref, r_ref, o_ref):
    a = a_ref[...].astype(BF16)
    for c, w in _col_chunks(o_ref.shape[1]):
        o_ref[:, c:c + w] = r_ref[:, c:c + w] + _dot(a, w_ref[:, c:c + w].astype(BF16))


def matmul(a, w, layer, out_dtype, tm=2048, tn=512, name="mm"):
    M, K = a.shape
    N = w.shape[-1]
    tm = _tile(M, tm, SUBLANE)
    tn = _tile(N, tn, LANE)
    return pl.pallas_call(
        _mm_kernel, grid=(M // tm, N // tn),
        in_specs=[pl.BlockSpec((tm, K), lambda i, j: (i, 0)),
                  pl.BlockSpec((None, K, tn), lambda i, j: (layer, 0, j))],
        out_specs=pl.BlockSpec((tm, tn), lambda i, j: (i, j)),
        out_shape=jax.ShapeDtypeStruct((M, N), out_dtype),
        compiler_params=_cparams(("parallel", "arbitrary")), name=name)(a, w)


def _wt_spec(layer, row0, tn, K):
    return pl.BlockSpec((None, pl.Element(tn), pl.Element(K)),
                        lambda i, j: (layer, pl.multiple_of(row0 + j * tn, SUBLANE), 0))


def matmul_nt(a, wt, layer, row0, n, out_dtype, tm=2048, tn=512, name="mm_nt"):
    M, K = a.shape
    assert row0 % SUBLANE == 0
    tm = _tile(M, tm, SUBLANE)
    tn = _tile(n, tn, LANE)
    return pl.pallas_call(
        _mm_nt_kernel, grid=(M // tm, n // tn),
        in_specs=[pl.BlockSpec((tm, K), lambda i, j: (i, 0)), _wt_spec(layer, row0, tn, K)],
        out_specs=pl.BlockSpec((tm, tn), lambda i, j: (i, j)),
        out_shape=jax.ShapeDtypeStruct((M, n), out_dtype),
        compiler_params=_cparams(("parallel", "arbitrary")), name=name)(a, wt)


def matmul_nt_gate(a, wt, layer, n, glr_row0, gate_up, gate_b, out_dtype, tm=2048, tn=512, name="mm_nt_gate"):
    M, K = a.shape
    depth, rank, gw = gate_up.shape
    assert glr_row0 % SUBLANE == 0 and rank % SUBLANE == 0
    tm = _tile(M, tm, SUBLANE)
    tn = _tile(n, tn, LANE)
    return pl.pallas_call(
        _mm_nt_gate_kernel, grid=(M // tm, n // tn),
        in_specs=[pl.BlockSpec((tm, K), lambda i, j: (i, 0)), _wt_spec(layer, 0, tn, K),
                  pl.BlockSpec((None, pl.Element(rank), pl.Element(K)), lambda i, j: (layer, glr_row0, 0)),
                  pl.BlockSpec((None, rank, gw), lambda i, j: (layer, 0, 0)),
                  pl.BlockSpec((None, 1, gw), lambda i, j: (layer, 0, 0))],
        out_specs=[pl.BlockSpec((tm, tn), lambda i, j: (i, j)), pl.BlockSpec((tm, gw), lambda i, j: (i, 0))],
        out_shape=[jax.ShapeDtypeStruct((M, n), out_dtype), jax.ShapeDtypeStruct((M, gw), F32)],
        compiler_params=_cparams(("parallel", "arbitrary")), name=name,
    )(a, wt, wt, gate_up, gate_b.reshape(depth, 1, gw))


def res_matmul(a, w, res, layer, tm=2048, tn=512, name="res_mm"):
    M, K = a.shape
    N = w.shape[-1]
    tm = _tile(M, tm, SUBLANE)
    tn = _tile(N, tn, LANE)
    return pl.pallas_call(
        _res_mm_kernel, grid=(M // tm, N // tn),
        in_specs=[pl.BlockSpec((tm, K), lambda i, j: (i, 0)),
                  pl.BlockSpec((None, K, tn), lambda i, j: (layer, 0, j)),
                  pl.BlockSpec((tm, tn), lambda i, j: (i, j))],
        out_specs=pl.BlockSpec((tm, tn), lambda i, j: (i, j)),
        out_shape=jax.ShapeDtypeStruct((M, N), F32),
        compiler_params=_cparams(("parallel", "arbitrary")), name=name)(a, w, res)


def _gla_chunk(q, k, v, lg, st, md, lp, consts):
    row, rsub, lsub, ones = consts
    c = min(GLA_SUB, lp)
    b = lg
    sh = 1
    while sh < lp:
        b = b + jnp.where(row >= sh, pltpu.roll(b, sh, 0), 0.0)
        sh *= 2
    o = _dot_nt((q * jnp.exp(b)).astype(md), st.astype(md))
    a_rows = []
    for i in range(lp // c):
        lo = i * c
        qb, kb, bb = q[lo:lo + c], k[lo:lo + c], b[lo:lo + c]
        ws = []
        for s in range(c):
            d = bb - bb[s:s + 1, :]
            ws.append(qb * kb[s:s + 1, :] * jnp.exp(jnp.where(rsub >= s, d, NEG)))
        rsum = _dot(jnp.concatenate(ws, axis=0).astype(md), ones)
        a_i = jnp.zeros((c, LANE), F32)
        for s in range(c):
            a_i = jnp.where(lsub == lo + s, rsum[s * c:(s + 1) * c], a_i)
        if i > 0:
            ref = b[lo - 1:lo, :]
            qt = qb * jnp.exp(bb - ref)
            kt = k * jnp.exp(jnp.where(row < lo, ref - b, NEG))
            a_i = a_i + _dot_nt(qt.astype(md), _pad_rows(kt, LANE).astype(md))
        a_rows.append(a_i)
    a = jnp.concatenate(a_rows, axis=0)
    vk = _pad_rows(v, LANE)
    o = o + _dot(a.astype(md), vk.astype(md))
    bend = b[lp - 1:lp, :]
    kd = _pad_rows(k * jnp.exp(bend - b), LANE)
    st = st * jnp.exp(bend) + _dot(vk.T.astype(md), kd.astype(md))
    return o, st


def _gla_kernel(*refs, lin, lp, n_chunks, heads, scale, has_s0):
    if has_s0:
        q_ref, k_ref, v_ref, r_ref, lg_ref, gn_ref, s0_ref, o_ref, st_ref, s_scr = refs
    else:
        q_ref, k_ref, v_ref, r_ref, lg_ref, gn_ref, o_ref, st_ref, s_scr = refs
    t = pl.program_id(1)
    dk = q_ref.shape[-1] // heads
    dv = v_ref.shape[-1] // heads
    c = min(GLA_SUB, lp)
    md = _mxu_dtype(lp)

    @pl.when(t == 0)
    def _():
        for h in range(heads):
            s_scr[h] = s0_ref[h].T if has_s0 else jnp.zeros((dv, dk), F32)

    row = lax.broadcasted_iota(jnp.int32, (lp, dk), 0)
    consts = (row, lax.broadcasted_iota(jnp.int32, (c, dk), 0),
              lax.broadcasted_iota(jnp.int32, (c, LANE), 1), jnp.ones((dk, LANE), md))

    def chunk(ci, carry):
        rows = pl.ds(pl.multiple_of(ci * lin, lin), lin)
        for h in range(heads):
            kcols, vcols = slice(h * dk, (h + 1) * dk), slice(h * dv, (h + 1) * dv)
            q = _pad_rows(q_ref[rows, kcols].astype(F32), lp) * scale
            k = _pad_rows(k_ref[rows, kcols].astype(F32), lp)
            v = _pad_rows(v_ref[rows, vcols].astype(F32), lp)
            lg = _pad_rows(lg_ref[rows, kcols], lp)
            o, st = _gla_chunk(q, k, v, lg, s_scr[h], md, lp, consts)
            s_scr[h] = st
            o = o[:lin]
            ms = jnp.mean(o * o, axis=-1, keepdims=True)
            y = o * lax.rsqrt(ms + EPS) * gn_ref[...]
            r = r_ref[rows, vcols].astype(F32)
            o_ref[rows, vcols] = (y * (r * _sigmoid(r))).astype(o_ref.dtype)
        return carry

    lax.fori_loop(0, n_chunks, chunk, 0)

    @pl.when(t == pl.num_programs(1) - 1)
    def _():
        for h in range(heads):
            st_ref[h] = s_scr[h].T


def gla(zg, lg, gnorm, s0, layer, heads, dk, dv, out_dtype):
    B, T, _ = zg.shape
    depth = gnorm.shape[0]
    qw, vw = heads * dk, heads * dv
    assert vw % qw == 0
    lin = min(GLA_CHUNK, T)
    assert T % lin == 0
    lp = max(lin, 16)
    tt = _tile(T, 512, lin)
    in_specs = [pl.BlockSpec((None, tt, qw), lambda b, t: (b, t, 0)),
                pl.BlockSpec((None, tt, qw), lambda b, t: (b, t, 1)),
                pl.BlockSpec((None, tt, vw), lambda b, t: (b, t, 2 * qw // vw)),
                pl.BlockSpec((None, tt, vw), lambda b, t: (b, t, 2 * qw // vw + 1)),
                pl.BlockSpec((None, tt, qw), lambda b, t: (b, t, 0)),
                pl.BlockSpec((None, 1, dv), lambda b, t: (layer, 0, 0))]
    args = [zg, zg, zg, zg, lg, gnorm.reshape(depth, 1, dv)]
    if s0 is not None:
        in_specs.append(pl.BlockSpec((None, None, heads, dk, dv), lambda b, t: (layer, b, 0, 0, 0)))
        args.append(s0)
    kern = functools.partial(_gla_kernel, lin=lin, lp=lp, n_chunks=tt // lin, heads=heads, scale=dk ** -0.5,
                             has_s0=s0 is not None)
    return pl.pallas_call(
        kern, grid=(B, T // tt), in_specs=in_specs,
        out_specs=[pl.BlockSpec((None, tt, vw), lambda b, t: (b, t, 0)),
                   pl.BlockSpec((None, heads, dk, dv), lambda b, t: (b, 0, 0, 0))],
        out_shape=[jax.ShapeDtypeStruct((B, T, vw), out_dtype),
                   jax.ShapeDtypeStruct((B, heads, dk, dv), F32)],
        scratch_shapes=[pltpu.VMEM((heads, dv, dk), F32)],
        compiler_params=_cparams(("parallel", "arbitrary")), name="gla",
    )(*args)


def _swa_kernel(sink_ref, q_ref, kp_ref, kc_ref, vp_ref, vc_ref, bias_ref, o_ref, *,
                layer, heads, group, hd, scale):
    tq = q_ref.shape[0]
    md = _mxu_dtype(tq)
    k = jnp.concatenate([kp_ref[...].astype(md), kc_ref[...].astype(md)], axis=0)
    v = jnp.concatenate([vp_ref[...].astype(md), vc_ref[...].astype(md)], axis=0)
    outs = []
    for h in range(heads):
        kv = h // group
        qh = (q_ref[:, h * hd:(h + 1) * hd].astype(F32) * scale).astype(md)
        s = _dot_nt(qh, k[:, kv * hd:(kv + 1) * hd]) + bias_ref[h]
        sink = sink_ref[layer, h]
        m = jnp.maximum(jnp.max(s, axis=-1, keepdims=True), sink)
        e = jnp.exp(s - m)
        den = jnp.sum(e, axis=-1, keepdims=True) + jnp.exp(sink - m)
        outs.append(_dot(e.astype(md), v[:, kv * hd:(kv + 1) * hd]) / den)
    per = LANE // hd
    for j in range(heads // per):
        o_ref[:, j * LANE:(j + 1) * LANE] = jnp.concatenate(
            outs[j * per:(j + 1) * per], axis=-1).astype(o_ref.dtype)


def _t5_bucket(dist):
    n = jnp.maximum(dist, 0)
    max_exact = N_BUCKETS // 2
    nf = jnp.maximum(n, 1).astype(F32)
    large = max_exact + (jnp.log(nf / max_exact) / math.log(MAX_DIST / max_exact)
                         * (N_BUCKETS - max_exact)).astype(jnp.int32)
    large = jnp.minimum(large, N_BUCKETS - 1)
    return jnp.where(n < max_exact, n, large)


def _swa_bias_table(rel_bias, tq, p, with_first):
    i = jnp.arange(tq)[:, None]
    j = jnp.arange(p + tq)[None, :]
    dist = p + i - j
    valid = (dist >= 0) & (dist <= WINDOW)
    onehot = (_t5_bucket(dist)[:, :, None] == jnp.arange(N_BUCKETS)).astype(F32)
    bias = jnp.einsum("ijb,bh->hij", onehot, rel_bias.astype(F32), precision=lax.Precision.HIGHEST)
    regular = jnp.where(valid[None], bias, NEG)
    if not with_first:
        return regular[None]
    return jnp.stack([jnp.where((valid & (j >= p))[None], bias, NEG), regular])


def swa(z3, prev_k, prev_v, sinks, bias, layer, kv_heads, hd, tq, out_dtype):
    B, T, _ = z3.shape
    heads = sinks.shape[1]
    qw, kw = heads * hd, kv_heads * hd
    assert qw % kw == 0
    ko, vo = qw // kw, qw // kw + 1
    nb = T // tq
    prompt = prev_k is None
    p = bias.shape[3] - tq
    if prompt:
        assert p == tq and bias.shape[0] == 2
        prev_specs = [pl.BlockSpec((None, p, kw), lambda b, n: (b, jnp.maximum(n - 1, 0), ko)),
                      pl.BlockSpec((None, p, kw), lambda b, n: (b, jnp.maximum(n - 1, 0), vo))]
        prev_k = prev_v = z3
        bias_spec = pl.BlockSpec((None,) + bias.shape[1:], lambda b, n: (jnp.minimum(n, 1), 0, 0, 0))
    else:
        assert nb == 1 and bias.shape[0] == 1
        prev_specs = [pl.BlockSpec((None, None, p, kw), lambda b, n: (layer, b, 0, 0))] * 2
        bias_spec = pl.BlockSpec((None,) + bias.shape[1:], lambda b, n: (0, 0, 0, 0))
    kern = functools.partial(_swa_kernel, layer=layer, heads=heads, group=heads // kv_heads, hd=hd,
                             scale=hd ** -0.5)
    return pl.pallas_call(
        kern, grid=(B, nb),
        in_specs=[pl.BlockSpec(memory_space=pltpu.SMEM),
                  pl.BlockSpec((None, tq, qw), lambda b, n: (b, n, 0)),
                  prev_specs[0],
                  pl.BlockSpec((None, tq, kw), lambda b, n: (b, n, ko)),
                  prev_specs[1],
                  pl.BlockSpec((None, tq, kw), lambda b, n: (b, n, vo)),
                  bias_spec],
        out_specs=pl.BlockSpec((None, tq, qw), lambda b, n: (b, n, 0)),
        out_shape=jax.ShapeDtypeStruct((B, T, qw), out_dtype),
        compiler_params=_cparams(("parallel", "arbitrary")), name="swa",
    )(sinks, z3, prev_k, z3, prev_v, z3, bias)


def _conv_branch_kernel(cb_ref, cc_ref, ch_ref, prev_ref, w_ref, o_ref, st_ref, ext_ref):
    tt = cc_ref.shape[0]
    u = cc_ref[...].astype(F32) * ch_ref[...].astype(F32)

    @pl.when(pl.program_id(2) == 0)
    def _():
        ext_ref[0:SUBLANE, :] = prev_ref[...]

    ext_ref[SUBLANE:SUBLANE + tt, :] = u
    u1 = ext_ref[pl.ds(SUBLANE - 1, tt), :]
    u2 = ext_ref[pl.ds(SUBLANE - 2, tt), :]
    tail = u[tt - SUBLANE:tt]
    ext_ref[0:SUBLANE, :] = tail
    y = u2 * w_ref[0:1, :] + u1 * w_ref[1:2, :] + u * w_ref[2:3, :]
    o_ref[...] = (cb_ref[...].astype(F32) * y).astype(o_ref.dtype)
    st_ref[...] = tail


def conv_branch(z3, prev8, conv_w, layer, out_dtype, tc=512, tt=1024):
    B, T, _ = z3.shape
    width = conv_w.shape[2]
    tc = _tile(width, tc, LANE)
    tt = _tile(T, tt, SUBLANE)
    assert T >= SUBLANE
    nc = width // tc
    z_specs = [pl.BlockSpec((None, tt, tc), (lambda b, c, t, o=n * nc: (b, t, o + c))) for n in range(3)]
    st_spec = pl.BlockSpec((None, SUBLANE, tc), lambda b, c, t: (b, 0, c))
    return pl.pallas_call(
        _conv_branch_kernel, grid=(B, nc, T // tt),
        in_specs=z_specs + [st_spec, pl.BlockSpec((None, CONV_W, tc), lambda b, c, t: (layer, 0, c))],
        out_specs=[pl.BlockSpec((None, tt, tc), lambda b, c, t: (b, t, c)), st_spec],
        out_shape=[jax.ShapeDtypeStruct((B, T, width), out_dtype),
                   jax.ShapeDtypeStruct((B, SUBLANE, width), F32)],
        scratch_shapes=[pltpu.VMEM((tt + SUBLANE, tc), F32)],
        compiler_params=_cparams(("parallel", "parallel", "arbitrary")), name="conv_branch",
    )(z3, z3, z3, prev8, conv_w)


def _ffn_up_kernel(h_ref, wu_ref, wg_ref, prev_ref, cw_ref, cb_ref, act_ref, st_ref,
                   ext_scr, carry_scr, *, seq_len):
    i, j = pl.program_id(0), pl.program_id(1)
    tm, tn = act_ref.shape
    h = h_ref[...]
    single = tm <= seq_len
    if single:
        starts = (i % (seq_len // tm)) == 0

        @pl.when(starts)
        def _():
            ext_scr[0:SUBLANE, :] = prev_ref[0]

        @pl.when(jnp.logical_not(starts))
        def _():
            ext_scr[0:SUBLANE, :] = carry_scr[j]
    else:
        nseq = tm // seq_len
        ext_scr[0:SUBLANE, :] = jnp.zeros((SUBLANE, tn), F32)
    for c, w in _col_chunks(tn):
        cols = slice(c, c + w)
        u = _dot(h, wu_ref[:, cols])
        g = _dot(h, wg_ref[:, cols])
        ext_scr[SUBLANE:SUBLANE + tm, cols] = g
        g1 = ext_scr[pl.ds(SUBLANE - 1, tm), cols]
        g2 = ext_scr[pl.ds(SUBLANE - 2, tm), cols]
        if single:
            tail = g[tm - SUBLANE:tm]
            carry_scr[j, :, cols] = tail
            st_ref[0, :, cols] = tail
        else:
            tix = lax.rem(lax.broadcasted_iota(jnp.int32, (tm, w), 0), seq_len)
            s1 = jnp.broadcast_to(prev_ref[:, SUBLANE - 1:SUBLANE, cols], (nseq, seq_len, w)).reshape(tm, w)
            s0 = jnp.broadcast_to(prev_ref[:, SUBLANE - 2:SUBLANE - 1, cols], (nseq, seq_len, w)).reshape(tm, w)
            g1 = jnp.where(tix == 0, s1, g1)
            g2 = jnp.where(tix == 0, s0, jnp.where(tix == 1, s1, g2))
            st_ref[:, :, cols] = g.reshape(nseq, seq_len, w)[:, seq_len - SUBLANE:, :]
        gc = g2 * cw_ref[0:1, cols] + g1 * cw_ref[1:2, cols] + g * cw_ref[2:3, cols] + cb_ref[:, cols]
        act_ref[:, cols] = (gc * _sigmoid(gc) * u).astype(act_ref.dtype)


def ffn_first_half(h, w_up, prev8, conv_w, conv_b, B, T, dff, out_dtype, tm=2048, tn=512):
    M, K = h.shape
    dffp = w_up.shape[1] // 2
    tn = _tile(dffp, tn, LANE)
    nj = dffp // tn
    if T >= tm:
        tm = _tile(T, tm, SUBLANE)
        nseq = 1
        prev_map = lambda i, j: (i // (T // tm), 0, j)
    else:
        assert T % SUBLANE == 0 and T >= SUBLANE
        nseq = _tile(B, max(tm // T, 1), 1)
        tm = nseq * T
        prev_map = lambda i, j: (i, 0, j)
    n_st = (M // tm) * nseq
    act, st = pl.pallas_call(
        functools.partial(_ffn_up_kernel, seq_len=T), grid=(M // tm, nj),
        in_specs=[pl.BlockSpec((tm, K), lambda i, j: (i, 0)),
                  pl.BlockSpec((K, tn), lambda i, j: (0, j)),
                  pl.BlockSpec((K, tn), lambda i, j: (0, nj + j)),
                  pl.BlockSpec((nseq, SUBLANE, tn), prev_map),
                  pl.BlockSpec((CONV_W, tn), lambda i, j: (0, j)),
                  pl.BlockSpec((1, tn), lambda i, j: (0, j))],
        out_specs=[pl.BlockSpec((tm, tn), lambda i, j: (i, j)),
                   pl.BlockSpec((nseq, SUBLANE, tn), lambda i, j: (i, 0, j))],
        out_shape=[jax.ShapeDtypeStruct((M, dff), out_dtype),
                   jax.ShapeDtypeStruct((n_st, SUBLANE, dffp), F32)],
        scratch_shapes=[pltpu.VMEM((tm + SUBLANE, tn), F32), pltpu.VMEM((nj, SUBLANE, tn), F32)],
        compiler_params=_cparams(("arbitrary", "arbitrary")), name="ffn_up",
    )(h, w_up, w_up, prev8, conv_w, conv_b)
    return act, st.reshape(B, n_st // B, SUBLANE, dffp)[:, -1]


def _merge_kernel(a_ref, b_ref, c_ref, wb_ref, g0_ref, g1_ref, g2_ref, o_ref):
    xs = [r[...].astype(BF16) for r in (a_ref, b_ref, c_ref)]
    gs = (g0_ref, g1_ref, g2_ref)
    for c, w in _col_chunks(o_ref.shape[1]):
        cols = slice(c, c + w)

        def term(n):
            return _sigmoid(gs[n][:, cols].astype(F32)) * _dot(xs[n], wb_ref[n, :, cols].astype(BF16))

        o_ref[:, cols] = ((term(0) + term(1)) + term(2)).astype(o_ref.dtype)


def merge(br_a, br_b, br_c, wb, layer, zg, out_dtype, tm=1024, tn=512):
    M, mix = br_a.shape
    d = wb.shape[3]
    tm = _tile(M, tm, SUBLANE)
    tn = _tile(d, tn, LANE)
    br_spec = pl.BlockSpec((tm, mix), lambda i, j: (i, 0))
    g_specs = [pl.BlockSpec((tm, tn), (lambda i, j, o=n * d // tn: (i, o + j))) for n in range(3)]
    return pl.pallas_call(
        _merge_kernel, grid=(M // tm, d // tn),
        in_specs=[br_spec, br_spec, br_spec,
                  pl.BlockSpec((None, 3, mix, tn), lambda i, j: (layer, 0, 0, j))] + g_specs,
        out_specs=pl.BlockSpec((tm, tn), lambda i, j: (i, j)),
        out_shape=jax.ShapeDtypeStruct((M, d), out_dtype),
        compiler_params=_cparams(("parallel", "arbitrary")), name="merge",
    )(br_a, br_b, br_c, wb, zg, zg, zg)


def _xattn_kernel(q_ref, k_ref, v_ref, o_ref, *, heads, hd, scale):
    md = _mxu_dtype(q_ref.shape[0])
    for h in range(heads):
        sl = slice(h * hd, (h + 1) * hd)
        s = _dot_nt(q_ref[:, sl].astype(md), k_ref[:, sl].astype(md)) * scale
        e = jnp.exp(s - jnp.max(s, axis=-1, keepdims=True))
        den = jnp.sum(e, axis=-1, keepdims=True)
        o_ref[:, sl] = (_dot(e.astype(md), v_ref[:, sl].astype(md)) / den).astype(o_ref.dtype)


def xattn(q3, mem_k, mem_v, layer, heads, out_dtype, tq=512):
    B, T, w = q3.shape
    nm = mem_k.shape[-2]
    hd = w // heads
    tq = _tile(T, tq, SUBLANE)
    kern = functools.partial(_xattn_kernel, heads=heads, hd=hd, scale=hd ** -0.5)
    if mem_k.ndim == 4:
        m_spec = pl.BlockSpec((None, None, nm, w), lambda b, t: (layer, b, 0, 0))
    else:
        m_spec = pl.BlockSpec((None, nm, w), lambda b, t: (b, 0, 0))
    return pl.pallas_call(
        kern, grid=(B, T // tq),
        in_specs=[pl.BlockSpec((None, tq, w), lambda b, t: (b, t, 0)), m_spec, m_spec],
        out_specs=pl.BlockSpec((None, tq, w), lambda b, t: (b, t, 0)),
        out_shape=jax.ShapeDtypeStruct((B, T, w), out_dtype),
        compiler_params=_cparams(("parallel", "arbitrary")), name="xattn",
    )(q3, mem_k, mem_v)


def _round_up(n, m):
    return -(-n // m) * m


def kernel(x_prompt, x_sample, state_gla, cache_swa_k, cache_swa_v, state_conv, state_ffn, cache_mem_k, cache_mem_v, mem_prompt, norm_mix, w_in, gla_gate_up, gla_gate_b, gla_norm, swa_sinks, rel_bias, conv_w, w_branch, w_out, norm_x, wx_q, wx_k, wx_v, wx_o, norm_ffn, ffn_up, ffn_conv_w, ffn_conv_b, ffn_down, norm_final):
    Bp, Tp, D = x_prompt.shape
    Bs, Ts, _ = x_sample.shape
    depth = w_in.shape[0]
    gla_heads, gla_dk, gla_dv = state_gla.shape[2:]
    swa_kv, swa_hd = cache_swa_k.shape[3:]
    swa_heads = swa_sinks.shape[1]
    x_heads, x_hd = cache_mem_k.shape[3:]
    n_mem = mem_prompt.shape[1]
    mix = conv_w.shape[2]
    dff = ffn_down.shape[1]
    rank = gla_gate_up.shape[1]
    gqk = gla_heads * gla_dk
    sqw, skw = swa_heads * swa_hd, swa_kv * swa_hd
    win = cache_swa_k.shape[2]
    xw = x_heads * x_hd
    assert win == WINDOW and mix == gla_heads * gla_dv == sqw

    w_in_t = jnp.swapaxes(w_in, 1, 2)
    a_w = 2 * gqk + 2 * mix
    s_w = sqw + 2 * skw
    o_glr, o_swa = a_w, a_w + rank
    o_conv = o_swa + s_w
    o_gates = o_conv + 3 * mix
    dffp = _round_up(dff, 512)

    def regroup_up(w):
        zpad = jnp.zeros((D, dffp - dff), BF16)
        return jnp.concatenate([w[:, :dff].astype(BF16), zpad, w[:, dff:].astype(BF16), zpad], axis=1)

    bias_p = _swa_bias_table(rel_bias, min(Tp, WINDOW), WINDOW, True)
    bias_s = _swa_bias_table(rel_bias, Ts, win, False)
    cache_k4 = cache_swa_k.reshape(depth, Bs, win, skw)
    cache_v4 = cache_swa_v.reshape(depth, Bs, win, skw)
    cmem_k4 = cache_mem_k.reshape(depth, Bs, n_mem, xw)
    cmem_v4 = cache_mem_v.reshape(depth, Bs, n_mem, xw)
    mem2 = mem_prompt.reshape(Bp * n_mem, D)

    def first8(st):
        return jnp.pad(st, ((0, 0), (SUBLANE - (CONV_W - 1), 0), (0, 0)))

    def trunk_layer(i, x, B, T, w_up_r, cw_ffn, cb_ffn, s0, swa_prev, conv_prev8, ffn_prev8,
                    mem_k, mem_v, swa_bias, act_dtype):
        M = B * T
        h = rmsnorm(x, norm_mix[i], BF16)
        zg, lg = matmul_nt_gate(h, w_in_t, i, a_w, o_glr, gla_gate_up, gla_gate_b, act_dtype, name="in_proj_gla")
        zs = matmul_nt(h, w_in_t, i, o_swa, s_w, act_dtype, tn=s_w // 2, name="in_proj_swa")
        zc = matmul_nt(h, w_in_t, i, o_conv, 3 * mix, act_dtype, name="in_proj_conv")
        zt = matmul_nt(h, w_in_t, i, o_gates, 3 * D, act_dtype, name="in_proj_gates")
        zs3 = zs.reshape(B, T, s_w)
        br_a, gla_st = gla(zg.reshape(B, T, a_w), lg.reshape(B, T, gqk), gla_norm, s0, i,
                           gla_heads, gla_dk, gla_dv, act_dtype)
        pk, pv = (None, None) if swa_prev is None else swa_prev
        br_b = swa(zs3, pk, pv, swa_sinks, swa_bias, i, swa_kv, swa_hd, min(T, WINDOW), act_dtype)
        br_c, conv_st = conv_branch(zc.reshape(B, T, 3 * mix), conv_prev8, conv_w, i, act_dtype)
        merged = merge(br_a.reshape(M, mix), br_b.reshape(M, mix), br_c.reshape(M, mix), w_branch, i, zt, BF16)
        x = res_matmul(merged, w_out, x, i, name="out_proj")
        q = matmul(rmsnorm(x, norm_x[i], BF16), wx_q, i, act_dtype, name="xq_proj")
        xo = xattn(q.reshape(B, T, xw), mem_k, mem_v, i, x_heads, act_dtype)
        x = res_matmul(xo.reshape(M, xw), wx_o, x, i, name="xo_proj")
        act, ffn_st = ffn_first_half(rmsnorm(x, norm_ffn[i], BF16), w_up_r, ffn_prev8, cw_ffn, cb_ffn,
                                     B, T, dff, BF16)
        x = res_matmul(act, ffn_down, x, i, tm=1024, tn=256, name="ffn_down")
        k_new = zs3[:, T - min(T, WINDOW):, sqw:sqw + skw].astype(F32)
        v_new = zs3[:, T - min(T, WINDOW):, sqw + skw:].astype(F32)
        return x, (gla_st, k_new, v_new, conv_st[:, SUBLANE - (CONV_W - 1):],
                   ffn_st[:, SUBLANE - (CONV_W - 1):, :dff])

    xp = x_prompt.reshape(Bp * Tp, D)
    xs = x_sample.reshape(Bs * Ts, D)
    act_p = BF16
    act_s = BF16 if Ts % 16 == 0 else F32
    st_p, st_s = [], []
    for i in range(depth):
        w_up_r = regroup_up(ffn_up[i])
        cw_ffn = jnp.pad(ffn_conv_w[i], ((0, 0), (0, dffp - dff)))
        cb_ffn = jnp.pad(ffn_conv_b[i], (0, dffp - dff)).reshape(1, dffp)
        mem_k = matmul(mem2, wx_k, i, F32, name="mem_k").reshape(Bp, n_mem, xw)
        mem_v = matmul(mem2, wx_v, i, F32, name="mem_v").reshape(Bp, n_mem, xw)
        xp, sp = trunk_layer(i, xp, Bp, Tp, w_up_r, cw_ffn, cb_ffn, None, None,
                             jnp.zeros((Bp, SUBLANE, mix), F32), jnp.zeros((Bp, SUBLANE, dffp), F32),
                             mem_k, mem_v, bias_p, act_p)
        xs, ss = trunk_layer(i, xs, Bs, Ts, w_up_r, cw_ffn, cb_ffn, state_gla, (cache_k4, cache_v4),
                             first8(state_conv[i]),
                             first8(jnp.pad(state_ffn[i], ((0, 0), (0, 0), (0, dffp - dff)))),
                             cmem_k4, cmem_v4, bias_s, act_s)
        st_p.append(dict(gla=sp[0], k=sp[1].reshape(Bp, WINDOW, swa_kv, swa_hd),
                         v=sp[2].reshape(Bp, WINDOW, swa_kv, swa_hd), conv=sp[3], ffn=sp[4],
                         mem_k=mem_k.reshape(Bp, n_mem, x_heads, x_hd),
                         mem_v=mem_v.reshape(Bp, n_mem, x_heads, x_hd)))
        kc = jnp.concatenate([cache_k4[i], ss[1]], axis=1)[:, Ts:]
        vc = jnp.concatenate([cache_v4[i], ss[2]], axis=1)[:, Ts:]
        st_s.append(dict(gla=ss[0], k=kc.reshape(Bs, win, swa_kv, swa_hd), v=vc.reshape(Bs, win, swa_kv, swa_hd),
                         conv=ss[3], ffn=ss[4]))
    y_prompt = rmsnorm(xp, norm_final, F32).reshape(Bp, Tp, D)
    y_sample = rmsnorm(xs, norm_final, F32).reshape(Bs, Ts, D)

    def stack(lst, key):
        return jnp.stack([s[key] for s in lst])

    return (y_prompt, y_sample, stack(st_p, "gla"), stack(st_s, "gla"), stack(st_p, "k"), stack(st_p, "v"),
            stack(st_s, "k"), stack(st_s, "v"), stack(st_p, "conv"), stack(st_s, "conv"),
            stack(st_p, "ffn"), stack(st_s, "ffn"), stack(st_p, "mem_k"), stack(st_p, "mem_v"))
```

```python
import functools
import math

import jax
import jax.numpy as jnp
from jax import lax
from jax.experimental import pallas as pl
from jax.experimental.pallas import tpu as pltpu

F32 = jnp.float32
BF16 = jnp.bfloat16

EPS = 1e-6
GLA_TAU = 16.0
GLA_CHUNK = 64
GLA_SUB = 16
WINDOW = 128
N_BUCKETS = 32
MAX_DIST = 128
CONV_W = 3
NEG = -1e30

LANE = 128
SUBLANE = 8
MXU_COLS = 256
VMEM_LIMIT = 56 * 1024 * 1024


def _cparams(sem):
    return pltpu.CompilerParams(dimension_semantics=sem, vmem_limit_bytes=VMEM_LIMIT)


def _tile(n, pref, unit):
    if n <= pref:
        return n
    t = (pref // unit) * unit
    while t > unit and n % t:
        t -= unit
    assert n % t == 0, (n, pref, unit)
    return t


def _mxu_dtype(rows):
    return BF16 if rows % 16 == 0 else F32


def _sigmoid(x):
    return 1.0 / (1.0 + jnp.exp(-x))


def _dot(a, b):
    return jnp.dot(a, b, preferred_element_type=F32)


def _dot_nt(a, b):
    return lax.dot_general(a, b, (((1,), (1,)), ((), ())), preferred_element_type=F32)


def _col_chunks(n):
    w = MXU_COLS if n % MXU_COLS == 0 else n
    return [(c, w) for c in range(0, n, w)]


def _pad_rows(x, rows):
    if x.shape[0] == rows:
        return x
    return jnp.concatenate([x, jnp.zeros((rows - x.shape[0], x.shape[1]), x.dtype)], axis=0)


def _norm_kernel(x_ref, g_ref, o_ref):
    x = x_ref[...]
    ms = jnp.mean(x * x, axis=-1, keepdims=True)
    o_ref[...] = (x * lax.rsqrt(ms + EPS) * g_ref[...]).astype(o_ref.dtype)


def rmsnorm(x, gain, out_dtype, tm=512):
    M, K = x.shape
    tm = _tile(M, tm, SUBLANE)
    return pl.pallas_call(
        _norm_kernel, grid=(M // tm,),
        in_specs=[pl.BlockSpec((tm, K), lambda i: (i, 0)), pl.BlockSpec((1, K), lambda i: (0, 0))],
        out_specs=pl.BlockSpec((tm, K), lambda i: (i, 0)),
        out_shape=jax.ShapeDtypeStruct((M, K), out_dtype),
        compiler_params=_cparams(("parallel",)), name="rmsnorm")(x, gain.reshape(1, K))


def _mm_kernel(a_ref, w_ref, o_ref):
    a = a_ref[...].astype(BF16)
    for c, w in _col_chunks(o_ref.shape[1]):
        o_ref[:, c:c + w] = _dot(a, w_ref[:, c:c + w].astype(BF16)).astype(o_ref.dtype)


def _mm_nt_kernel(a_ref, wt_ref, o_ref):
    a = a_ref[...]
    for c, w in _col_chunks(o_ref.shape[1]):
        o_ref[:, c:c + w] = _dot_nt(a, wt_ref[c:c + w, :].astype(BF16)).astype(o_ref.dtype)


def _mm_nt_gate_kernel(a_ref, wt_ref, wglr_ref, up_ref, gb_ref, o_ref, lg_ref):
    _mm_nt_kernel(a_ref, wt_ref, o_ref)

    @pl.when(pl.program_id(1) == 0)
    def _():
        glr = _dot_nt(a_ref[...], _pad_rows(wglr_ref[...], LANE).astype(BF16))
        z = _dot(glr.astype(BF16), _pad_rows(up_ref[...], LANE).astype(BF16)) + gb_ref[...]
        lg_ref[...] = -(jnp.maximum(-z, 0.0) + jnp.log1p(jnp.exp(-jnp.abs(z)))) / GLA_TAU


def _res_mm_kernel(a_ref, w_ref, r_ref, o_ref):
    a = a_ref[...].astype(BF16)
    for c, w in _col_chunks(o_ref.shape[1]):
        o_ref[:, c:c + w] = r_ref[:, c:c + w] + _dot(a, w_ref[:, c:c + w].astype(BF16))


def matmul(a, w, layer, out_dtype, tm=2048, tn=512, name="mm"):
    M, K = a.shape
    N = w.shape[-1]
    tm = _tile(M, tm, SUBLANE)
    tn = _tile(N, tn, LANE)
    return pl.pallas_call(
        _mm_kernel, grid=(M // tm, N // tn),
        in_specs=[pl.BlockSpec((tm, K), lambda i, j: (i, 0)),
                  pl.BlockSpec((None, K, tn), lambda i, j: (layer, 0, j))],
        out_specs=pl.BlockSpec((tm, tn), lambda i, j: (i, j)),
        out_shape=jax.ShapeDtypeStruct((M, N), out_dtype),
        compiler_params=_cparams(("parallel", "arbitrary")), name=name)(a, w)


def _wt_spec(layer, row0, tn, K):
    return pl.BlockSpec((None, pl.Element(tn), pl.Element(K)),
                        lambda i, j: (layer, pl.multiple_of(row0 + j * tn, SUBLANE), 0))


def matmul_nt(a, wt, layer, row0, n, out_dtype, tm=2048, tn=512, name="mm_nt"):
    M, K = a.shape
    assert row0 % SUBLANE == 0
    tm = _tile(M, tm, SUBLANE)
    tn = _tile(n, tn, LANE)
    return pl.pallas_call(
        _mm_nt_kernel, grid=(M // tm, n // tn),
        in_specs=[pl.BlockSpec((tm, K), lambda i, j: (i, 0)), _wt_spec(layer, row0, tn, K)],
        out_specs=pl.BlockSpec((tm, tn), lambda i, j: (i, j)),
        out_shape=jax.ShapeDtypeStruct((M, n), out_dtype),
        compiler_params=_cparams(("parallel", "arbitrary")), name=name)(a, wt)


def matmul_nt_gate(a, wt, layer, n, glr_row0, gate_up, gate_b, out_dtype, tm=2048, tn=512, name="mm_nt_gate"):
    M, K = a.shape
    depth, rank, gw = gate_up.shape
    assert glr_row0 % SUBLANE == 0 and rank % SUBLANE == 0
    tm = _tile(M, tm, SUBLANE)
    tn = _tile(n, tn, LANE)
    return pl.pallas_call(
        _mm_nt_gate_kernel, grid=(M // tm, n // tn),
        in_specs=[pl.BlockSpec((tm, K), lambda i, j: (i, 0)), _wt_spec(layer, 0, tn, K),
                  pl.BlockSpec((None, pl.Element(rank), pl.Element(K)), lambda i, j: (layer, glr_row0, 0)),
                  pl.BlockSpec((None, rank, gw), lambda i, j: (layer, 0, 0)),
                  pl.BlockSpec((None, 1, gw), lambda i, j: (layer, 0, 0))],
        out_specs=[pl.BlockSpec((tm, tn), lambda i, j: (i, j)), pl.BlockSpec((tm, gw), lambda i, j: (i, 0))],
        out_shape=[jax.ShapeDtypeStruct((M, n), out_dtype), jax.ShapeDtypeStruct((M, gw), F32)],
        compiler_params=_cparams(("parallel", "arbitrary")), name=name,
    )(a, wt, wt, gate_up, gate_b.reshape(depth, 1, gw))


def res_matmul(a, w, res, layer, tm=2048, tn=512, name="res_mm"):
    M, K = a.shape
    N = w.shape[-1]
    tm = _tile(M, tm, SUBLANE)
    tn = _tile(N, tn, LANE)
    return pl.pallas_call(
        _res_mm_kernel, grid=(M // tm, N // tn),
        in_specs=[pl.BlockSpec((tm, K), lambda i, j: (i, 0)),
                  pl.BlockSpec((None, K, tn), lambda i, j: (layer, 0, j)),
                  pl.BlockSpec((tm, tn), lambda i, j: (i, j))],
        out_specs=pl.BlockSpec((tm, tn), lambda i, j: (i, j)),
        out_shape=jax.ShapeDtypeStruct((M, N), F32),
        compiler_params=_cparams(("parallel", "arbitrary")), name=name)(a, w, res)


def _gla_chunk(q, k, v, lg, st, md, lp, consts):
    row, rsub, lsub, ones = consts
    c = min(GLA_SUB, lp)
    b = lg
    sh = 1
    while sh < lp:
        b = b + jnp.where(row >= sh, pltpu.roll(b, sh, 0), 0.0)
        sh *= 2
    o = _dot_nt((q * jnp.exp(b)).astype(md), st.astype(md))
    a_rows = []
    for i in range(lp // c):
        lo = i * c
        qb, kb, bb = q[lo:lo + c], k[lo:lo + c], b[lo:lo + c]
        ws = []
        for s in range(c):
            d = bb - bb[s:s + 1, :]
            ws.append(qb * kb[s:s + 1, :] * jnp.exp(jnp.where(rsub >= s, d, NEG)))
        rsum = _dot(jnp.concatenate(ws, axis=0).astype(md), ones)
        a_i = jnp.zeros((c, LANE), F32)
        for s in range(c):
            a_i = jnp.where(lsub == lo + s, rsum[s * c:(s + 1) * c], a_i)
        if i > 0:
            ref = b[lo - 1:lo, :]
            qt = qb * jnp.exp(bb - ref)
            kt = k * jnp.exp(jnp.where(row < lo, ref - b, NEG))
            a_i = a_i + _dot_nt(qt.astype(md), _pad_rows(kt, LANE).astype(md))
        a_rows.append(a_i)
    a = jnp.concatenate(a_rows, axis=0)
    vk = _pad_rows(v, LANE)
    o = o + _dot(a.astype(md), vk.astype(md))
    bend = b[lp - 1:lp, :]
    kd = _pad_rows(k * jnp.exp(bend - b), LANE)
    st = st * jnp.exp(bend) + _dot(vk.T.astype(md), kd.astype(md))
    return o, st


def _gla_kernel(*refs, lin, lp, n_chunks, heads, scale, has_s0):
    if has_s0:
        q_ref, k_ref, v_ref, r_ref, lg_ref, gn_ref, s0_ref, o_ref, st_ref, s_scr = refs
    else:
        q_ref, k_ref, v_ref, r_ref, lg_ref, gn_ref, o_ref, st_ref, s_scr = refs
    t = pl.program_id(1)
    bb = q_ref.shape[0]
    dk = q_ref.shape[-1] // heads
    dv = v_ref.shape[-1] // heads
    c = min(GLA_SUB, lp)
    md = _mxu_dtype(lp)
    pairs = [(s, h) for s in range(bb) for h in range(heads)]

    @pl.when(t == 0)
    def _():
        for s, h in pairs:
            s_scr[s, h] = s0_ref[s, h].T if has_s0 else jnp.zeros((dv, dk), F32)

    row = lax.broadcasted_iota(jnp.int32, (lp, dk), 0)
    consts = (row, lax.broadcasted_iota(jnp.int32, (c, dk), 0),
              lax.broadcasted_iota(jnp.int32, (c, LANE), 1), jnp.ones((dk, LANE), md))

    def chunk(ci, carry):
        rows = pl.ds(pl.multiple_of(ci * lin, lin), lin)
        for s, h in pairs:
            kcols, vcols = slice(h * dk, (h + 1) * dk), slice(h * dv, (h + 1) * dv)
            q = _pad_rows(q_ref[s, rows, kcols].astype(F32), lp) * scale
            k = _pad_rows(k_ref[s, rows, kcols].astype(F32), lp)
            v = _pad_rows(v_ref[s, rows, vcols].astype(F32), lp)
            lg = _pad_rows(lg_ref[s, rows, kcols], lp)
            o, st = _gla_chunk(q, k, v, lg, s_scr[s, h], md, lp, consts)
            s_scr[s, h] = st
            o = o[:lin]
            ms = jnp.mean(o * o, axis=-1, keepdims=True)
            y = o * lax.rsqrt(ms + EPS) * gn_ref[...]
            r = r_ref[s, rows, vcols].astype(F32)
            o_ref[s, rows, vcols] = (y * (r * _sigmoid(r))).astype(o_ref.dtype)
        return carry

    lax.fori_loop(0, n_chunks, chunk, 0)

    @pl.when(t == pl.num_programs(1) - 1)
    def _():
        for s, h in pairs:
            st_ref[s, h] = s_scr[s, h].T


def gla(zg, lg, gnorm, s0, layer, heads, dk, dv, out_dtype, seqs_per_step=1):
    B, T, _ = zg.shape
    depth = gnorm.shape[0]
    qw, vw = heads * dk, heads * dv
    assert vw % qw == 0
    lin = min(GLA_CHUNK, T)
    assert T % lin == 0
    lp = max(lin, 16)
    tt = _tile(T, 512, lin)
    bb = _tile(B, seqs_per_step, 1)
    in_specs = [pl.BlockSpec((bb, tt, qw), lambda b, t: (b, t, 0)),
                pl.BlockSpec((bb, tt, qw), lambda b, t: (b, t, 1)),
                pl.BlockSpec((bb, tt, vw), lambda b, t: (b, t, 2 * qw // vw)),
                pl.BlockSpec((bb, tt, vw), lambda b, t: (b, t, 2 * qw // vw + 1)),
                pl.BlockSpec((bb, tt, qw), lambda b, t: (b, t, 0)),
                pl.BlockSpec((None, 1, dv), lambda b, t: (layer, 0, 0))]
    args = [zg, zg, zg, zg, lg, gnorm.reshape(depth, 1, dv)]
    if s0 is not None:
        in_specs.append(pl.BlockSpec((None, bb, heads, dk, dv), lambda b, t: (layer, b, 0, 0, 0)))
        args.append(s0)
    kern = functools.partial(_gla_kernel, lin=lin, lp=lp, n_chunks=tt // lin, heads=heads, scale=dk ** -0.5,
                             has_s0=s0 is not None)
    return pl.pallas_call(
        kern, grid=(B // bb, T // tt), in_specs=in_specs,
        out_specs=[pl.BlockSpec((bb, tt, vw), lambda b, t: (b, t, 0)),
                   pl.BlockSpec((bb, heads, dk, dv), lambda b, t: (b, 0, 0, 0))],
        out_shape=[jax.ShapeDtypeStruct((B, T, vw), out_dtype),
                   jax.ShapeDtypeStruct((B, heads, dk, dv), F32)],
        scratch_shapes=[pltpu.VMEM((bb, heads, dv, dk), F32)],
        compiler_params=_cparams(("parallel", "arbitrary")), name="gla",
    )(*args)


def _swa_kernel(sink_ref, q_ref, kp_ref, kc_ref, vp_ref, vc_ref, bias_ref, o_ref, *,
                layer, heads, group, hd, scale):
    tq = q_ref.shape[0]
    md = _mxu_dtype(tq)
    k = jnp.concatenate([kp_ref[...].astype(md), kc_ref[...].astype(md)], axis=0)
    v = jnp.concatenate([vp_ref[...].astype(md), vc_ref[...].astype(md)], axis=0)
    outs = []
    for h in range(heads):
        kv = h // group
        qh = (q_ref[:, h * hd:(h + 1) * hd].astype(F32) * scale).astype(md)
        s = _dot_nt(qh, k[:, kv * hd:(kv + 1) * hd]) + bias_ref[h]
        sink = sink_ref[layer, h]
        m = jnp.maximum(jnp.max(s, axis=-1, keepdims=True), sink)
        e = jnp.exp(s - m)
        den = jnp.sum(e, axis=-1, keepdims=True) + jnp.exp(sink - m)
        outs.append(_dot(e.astype(md), v[:, kv * hd:(kv + 1) * hd]) / den)
    per = LANE // hd
    for j in range(heads // per):
        o_ref[:, j * LANE:(j + 1) * LANE] = jnp.concatenate(
            outs[j * per:(j + 1) * per], axis=-1).astype(o_ref.dtype)


def _t5_bucket(dist):
    n = jnp.maximum(dist, 0)
    max_exact = N_BUCKETS // 2
    nf = jnp.maximum(n, 1).astype(F32)
    large = max_exact + (jnp.log(nf / max_exact) / math.log(MAX_DIST / max_exact)
                         * (N_BUCKETS - max_exact)).astype(jnp.int32)
    large = jnp.minimum(large, N_BUCKETS - 1)
    return jnp.where(n < max_exact, n, large)


def _swa_bias_table(rel_bias, tq, p, with_first):
    i = jnp.arange(tq)[:, None]
    j = jnp.arange(p + tq)[None, :]
    dist = p + i - j
    valid = (dist >= 0) & (dist <= WINDOW)
    onehot = (_t5_bucket(dist)[:, :, None] == jnp.arange(N_BUCKETS)).astype(F32)
    bias = jnp.einsum("ijb,bh->hij", onehot, rel_bias.astype(F32), precision=lax.Precision.HIGHEST)
    regular = jnp.where(valid[None], bias, NEG)
    if not with_first:
        return regular[None]
    return jnp.stack([jnp.where((valid & (j >= p))[None], bias, NEG), regular])


def swa(z3, prev_k, prev_v, sinks, bias, layer, kv_heads, hd, tq, out_dtype):
    B, T, _ = z3.shape
    heads = sinks.shape[1]
    qw, kw = heads * hd, kv_heads * hd
    assert qw % kw == 0
    ko, vo = qw // kw, qw // kw + 1
    nb = T // tq
    prompt = prev_k is None
    p = bias.shape[3] - tq
    if prompt:
        assert p == tq and bias.shape[0] == 2
        prev_specs = [pl.BlockSpec((None, p, kw), lambda b, n: (b, jnp.maximum(n - 1, 0), ko)),
                      pl.BlockSpec((None, p, kw), lambda b, n: (b, jnp.maximum(n - 1, 0), vo))]
        prev_k = prev_v = z3
        bias_spec = pl.BlockSpec((None,) + bias.shape[1:], lambda b, n: (jnp.minimum(n, 1), 0, 0, 0))
    else:
        assert nb == 1 and bias.shape[0] == 1
        prev_specs = [pl.BlockSpec((None, None, p, kw), lambda b, n: (layer, b, 0, 0))] * 2
        bias_spec = pl.BlockSpec((None,) + bias.shape[1:], lambda b, n: (0, 0, 0, 0))
    kern = functools.partial(_swa_kernel, layer=layer, heads=heads, group=heads // kv_heads, hd=hd,
                             scale=hd ** -0.5)
    return pl.pallas_call(
        kern, grid=(B, nb),
        in_specs=[pl.BlockSpec(memory_space=pltpu.SMEM),
                  pl.BlockSpec((None, tq, qw), lambda b, n: (b, n, 0)),
                  prev_specs[0],
                  pl.BlockSpec((None, tq, kw), lambda b, n: (b, n, ko)),
                  prev_specs[1],
                  pl.BlockSpec((None, tq, kw), lambda b, n: (b, n, vo)),
                  bias_spec],
        out_specs=pl.BlockSpec((None, tq, qw), lambda b, n: (b, n, 0)),
        out_shape=jax.ShapeDtypeStruct((B, T, qw), out_dtype),
        compiler_params=_cparams(("parallel", "arbitrary")), name="swa",
    )(sinks, z3, prev_k, z3, prev_v, z3, bias)


def _swa_decode_kernel(q_ref, kc_ref, kn_ref, vc_ref, vn_ref, tab_ref, o_ref, *, kv_heads, group, hd, scale):
    bb, T, _ = q_ref.shape
    kw = kc_ref.shape[-1]
    npad = tab_ref.shape[-1] - kc_ref.shape[1] - T
    md = _mxu_dtype(group * T)
    per = LANE // hd
    for s in range(bb):
        zpad = jnp.zeros((npad, kw), F32)
        k = jnp.concatenate([kc_ref[s].astype(F32), kn_ref[s].astype(F32), zpad], axis=0).astype(md)
        v = jnp.concatenate([vc_ref[s].astype(F32), vn_ref[s].astype(F32), zpad], axis=0).astype(md)
        outs = []
        for g in range(kv_heads):
            qs = jnp.concatenate([q_ref[s, :, (g * group + j) * hd:(g * group + j + 1) * hd].astype(F32)
                                  for j in range(group)], axis=0) * scale
            sc = _dot_nt(qs.astype(md), k[:, g * hd:(g + 1) * hd]) + tab_ref[g]
            e = jnp.exp(sc - jnp.max(sc, axis=-1, keepdims=True))
            den = jnp.sum(e, axis=-1, keepdims=True)
            o = _dot(e.astype(md), v[:, g * hd:(g + 1) * hd]) / den
            outs += [o[j * T:(j + 1) * T] for j in range(group)]
        for j in range(len(outs) // per):
            o_ref[s, :, j * LANE:(j + 1) * LANE] = jnp.concatenate(
                outs[j * per:(j + 1) * per], axis=-1).astype(o_ref.dtype)


def _swa_decode_table(rel_bias, sinks, T, p, kv_heads):
    depth, heads = sinks.shape
    group = heads // kv_heads
    base = _swa_bias_table(rel_bias, T, p, False)[0].reshape(kv_heads, group * T, p + T)
    npad = _round_up(p + T + 1, 16) - (p + T)
    sink_col = jnp.repeat(sinks.astype(F32).reshape(depth, kv_heads, group), T, axis=2)[..., None]
    parts = [jnp.broadcast_to(base[None], (depth,) + base.shape), sink_col,
             jnp.full((depth, kv_heads, group * T, npad - 1), NEG, F32)]
    return jnp.concatenate(parts, axis=-1)


def swa_decode(z3, cache_k, cache_v, table, layer, kv_heads, hd, out_dtype, seqs_per_step=8):
    B, T, zw = z3.shape
    kw = kv_heads * hd
    qw = zw - 2 * kw
    p = cache_k.shape[2]
    bb = _tile(B, seqs_per_step, 1)
    kern = functools.partial(_swa_decode_kernel, kv_heads=kv_heads, group=qw // kw, hd=hd, scale=hd ** -0.5)
    c_spec = pl.BlockSpec((None, bb, p, kw), lambda b: (layer, b, 0, 0))
    return pl.pallas_call(
        kern, grid=(B // bb,),
        in_specs=[pl.BlockSpec((bb, T, qw), lambda b: (b, 0, 0)),
                  c_spec, pl.BlockSpec((bb, T, kw), lambda b: (b, 0, qw // kw)),
                  c_spec, pl.BlockSpec((bb, T, kw), lambda b: (b, 0, qw // kw + 1)),
                  pl.BlockSpec((None,) + table.shape[1:], lambda b: (layer, 0, 0, 0))],
        out_specs=pl.BlockSpec((bb, T, qw), lambda b: (b, 0, 0)),
        out_shape=jax.ShapeDtypeStruct((B, T, qw), out_dtype),
        compiler_params=_cparams(("parallel",)), name="swa_decode",
    )(z3, cache_k, z3, cache_v, z3, table)


def _earlier_rows(x, i, j, cols, prev_ref, st_ref, carry_scr, seq_len):
    tm, w = x.shape
    if tm <= seq_len:
        @pl.when((i % (seq_len // tm)) == 0)
        def _():
            carry_scr[j, :, cols] = prev_ref[0, :, cols]

        before = carry_scr[j, :, cols]
        row8 = lax.broadcasted_iota(jnp.int32, (SUBLANE, w), 0)

        def back(s):
            xr = pltpu.roll(x, s, 0)
            top = jnp.where(row8 < s, pltpu.roll(before, s, 0), xr[0:SUBLANE])
            return jnp.concatenate([top, xr[SUBLANE:]], axis=0)

        tail = x[tm - SUBLANE:tm]
        carry_scr[j, :, cols] = tail
        st_ref[0, :, cols] = tail
        return back(1), back(2)
    nseq = tm // seq_len
    tix = lax.rem(lax.broadcasted_iota(jnp.int32, (tm, w), 0), seq_len)
    s1 = jnp.broadcast_to(prev_ref[:, SUBLANE - 1:SUBLANE, cols], (nseq, seq_len, w)).reshape(tm, w)
    s0 = jnp.broadcast_to(prev_ref[:, SUBLANE - 2:SUBLANE - 1, cols], (nseq, seq_len, w)).reshape(tm, w)
    x1 = jnp.where(tix == 0, s1, pltpu.roll(x, 1, 0))
    x2 = jnp.where(tix == 0, s0, jnp.where(tix == 1, s1, pltpu.roll(x, 2, 0)))
    st_ref[:, :, cols] = x.reshape(nseq, seq_len, w)[:, seq_len - SUBLANE:, :]
    return x1, x2


def _row_blocking(B, T, tm):
    if T >= tm:
        tm = _tile(T, tm, SUBLANE)
        return tm, 1, (lambda i: i // (T // tm))
    assert T % SUBLANE == 0
    nseq = _tile(B, max(tm // T, 1), 1)
    return nseq * T, nseq, (lambda i: i)


def _conv_branch_kernel(cb_ref, cc_ref, ch_ref, prev_ref, w_ref, o_ref, st_ref, carry_scr, *, seq_len):
    i, j = pl.program_id(0), pl.program_id(1)
    u = cc_ref[...].astype(F32) * ch_ref[...].astype(F32)
    u1, u2 = _earlier_rows(u, i, j, slice(None), prev_ref, st_ref, carry_scr, seq_len)
    y = u2 * w_ref[0:1, :] + u1 * w_ref[1:2, :] + u * w_ref[2:3, :]
    o_ref[...] = (cb_ref[...].astype(F32) * y).astype(o_ref.dtype)


def conv_branch(z, prev8, conv_w, layer, B, T, out_dtype, tm=1024, tc=512):
    M = B * T
    width = conv_w.shape[2]
    tc = _tile(width, tc, LANE)
    nc = width // tc
    tm, nseq, seq_of = _row_blocking(B, T, tm)
    n_st = (M // tm) * nseq
    z_specs = [pl.BlockSpec((tm, tc), (lambda i, j, o=n * nc: (i, o + j))) for n in range(3)]
    out, st = pl.pallas_call(
        functools.partial(_conv_branch_kernel, seq_len=T), grid=(M // tm, nc),
        in_specs=z_specs + [pl.BlockSpec((nseq, SUBLANE, tc), lambda i, j: (seq_of(i), 0, j)),
                            pl.BlockSpec((None, CONV_W, tc), lambda i, j: (layer, 0, j))],
        out_specs=[pl.BlockSpec((tm, tc), lambda i, j: (i, j)),
                   pl.BlockSpec((nseq, SUBLANE, tc), lambda i, j: (i, 0, j))],
        out_shape=[jax.ShapeDtypeStruct((M, width), out_dtype),
                   jax.ShapeDtypeStruct((n_st, SUBLANE, width), F32)],
        scratch_shapes=[pltpu.VMEM((nc, SUBLANE, tc), F32)],
        compiler_params=_cparams(("arbitrary", "arbitrary")), name="conv_branch",
    )(z, z, z, prev8, conv_w)
    return out, st.reshape(B, n_st // B, SUBLANE, width)[:, -1]


def _ffn_up_kernel(h_ref, wu_ref, wg_ref, prev_ref, cw_ref, cb_ref, act_ref, st_ref, carry_scr, *, seq_len):
    i, j = pl.program_id(0), pl.program_id(1)
    tn = act_ref.shape[1]
    h = h_ref[...]
    for c, w in _col_chunks(tn):
        cols = slice(c, c + w)
        u = _dot(h, wu_ref[:, cols].astype(BF16))
        g = _dot(h, wg_ref[:, cols].astype(BF16))
        g1, g2 = _earlier_rows(g, i, j, cols, prev_ref, st_ref, carry_scr, seq_len)
        gc = g2 * cw_ref[0:1, cols] + g1 * cw_ref[1:2, cols] + g * cw_ref[2:3, cols] + cb_ref[:, cols]
        act_ref[:, cols] = (gc * _sigmoid(gc) * u).astype(act_ref.dtype)


def ffn_first_half(h, w_up, layer, prev8, conv_w, conv_b, B, T, out_dtype, tm=2048, tn=512):
    M, K = h.shape
    dff = w_up.shape[2] // 2
    assert dff % LANE == 0 and dff >= tn
    nj = -(-dff // tn)
    tm, nseq, seq_of = _row_blocking(B, T, tm)
    n_st = (M // tm) * nseq

    def col(j, base=0):
        return pl.multiple_of(base + jnp.minimum(j * tn, dff - tn), LANE)

    def win(rows):
        return pl.BlockSpec((pl.Element(rows), pl.Element(tn)), lambda i, j: (0, col(j)))

    def st_win(seq_index):
        return pl.BlockSpec((pl.Element(nseq), pl.Element(SUBLANE), pl.Element(tn)),
                            lambda i, j: (seq_index(i) * nseq, 0, col(j)))

    act, st = pl.pallas_call(
        functools.partial(_ffn_up_kernel, seq_len=T), grid=(M // tm, nj),
        in_specs=[pl.BlockSpec((tm, K), lambda i, j: (i, 0)),
                  pl.BlockSpec((None, pl.Element(K), pl.Element(tn)), lambda i, j: (layer, 0, col(j))),
                  pl.BlockSpec((None, pl.Element(K), pl.Element(tn)), lambda i, j: (layer, 0, col(j, dff))),
                  st_win(seq_of), win(CONV_W), win(1)],
        out_specs=[pl.BlockSpec((pl.Element(tm), pl.Element(tn)),
                                lambda i, j: (pl.multiple_of(i * tm, SUBLANE), col(j))),
                   st_win(lambda i: i)],
        out_shape=[jax.ShapeDtypeStruct((M, dff), out_dtype),
                   jax.ShapeDtypeStruct((n_st, SUBLANE, dff), F32)],
        scratch_shapes=[pltpu.VMEM((nj, SUBLANE, tn), F32)],
        compiler_params=_cparams(("arbitrary", "arbitrary")), name="ffn_up",
    )(h, w_up, w_up, prev8, conv_w, conv_b)
    return act, st.reshape(B, n_st // B, SUBLANE, dff)[:, -1]


def _merge_kernel(a_ref, b_ref, c_ref, wb_ref, g0_ref, g1_ref, g2_ref, o_ref):
    xs = [r[...].astype(BF16) for r in (a_ref, b_ref, c_ref)]
    gs = (g0_ref, g1_ref, g2_ref)
    for c, w in _col_chunks(o_ref.shape[1]):
        cols = slice(c, c + w)

        def term(n):
            return _sigmoid(gs[n][:, cols].astype(F32)) * _dot(xs[n], wb_ref[n, :, cols].astype(BF16))

        o_ref[:, cols] = ((term(0) + term(1)) + term(2)).astype(o_ref.dtype)


def merge(br_a, br_b, br_c, wb, layer, zg, out_dtype, tm=1024, tn=512):
    M, mix = br_a.shape
    d = wb.shape[3]
    tm = _tile(M, tm, SUBLANE)
    tn = _tile(d, tn, LANE)
    br_spec = pl.BlockSpec((tm, mix), lambda i, j: (i, 0))
    g_specs = [pl.BlockSpec((tm, tn), (lambda i, j, o=n * d // tn: (i, o + j))) for n in range(3)]
    return pl.pallas_call(
        _merge_kernel, grid=(M // tm, d // tn),
        in_specs=[br_spec, br_spec, br_spec,
                  pl.BlockSpec((None, 3, mix, tn), lambda i, j: (layer, 0, 0, j))] + g_specs,
        out_specs=pl.BlockSpec((tm, tn), lambda i, j: (i, j)),
        out_shape=jax.ShapeDtypeStruct((M, d), out_dtype),
        compiler_params=_cparams(("parallel", "arbitrary")), name="merge",
    )(br_a, br_b, br_c, wb, zg, zg, zg)


def _xattn_kernel(q_ref, k_ref, v_ref, o_ref, *, heads, hd, scale):
    md = _mxu_dtype(q_ref.shape[0])
    for h in range(heads):
        sl = slice(h * hd, (h + 1) * hd)
        s = _dot_nt(q_ref[:, sl].astype(md), k_ref[:, sl].astype(md)) * scale
        e = jnp.exp(s - jnp.max(s, axis=-1, keepdims=True))
        den = jnp.sum(e, axis=-1, keepdims=True)
        o_ref[:, sl] = (_dot(e.astype(md), v_ref[:, sl].astype(md)) / den).astype(o_ref.dtype)


def xattn(q3, mem_k, mem_v, layer, heads, out_dtype, tq=512):
    B, T, w = q3.shape
    nm = mem_k.shape[-2]
    hd = w // heads
    tq = _tile(T, tq, SUBLANE)
    kern = functools.partial(_xattn_kernel, heads=heads, hd=hd, scale=hd ** -0.5)
    if mem_k.ndim == 4:
        m_spec = pl.BlockSpec((None, None, nm, w), lambda b, t: (layer, b, 0, 0))
    else:
        m_spec = pl.BlockSpec((None, nm, w), lambda b, t: (b, 0, 0))
    return pl.pallas_call(
        kern, grid=(B, T // tq),
        in_specs=[pl.BlockSpec((None, tq, w), lambda b, t: (b, t, 0)), m_spec, m_spec],
        out_specs=pl.BlockSpec((None, tq, w), lambda b, t: (b, t, 0)),
        out_shape=jax.ShapeDtypeStruct((B, T, w), out_dtype),
        compiler_params=_cparams(("parallel", "arbitrary")), name="xattn",
    )(q3, mem_k, mem_v)


def _xattn_decode_kernel(q_ref, k_ref, v_ref, o_ref, *, heads, hd, scale):
    bb, T, _ = q_ref.shape
    nm = k_ref.shape[1] // heads
    rows = max(T, 16)
    for s in range(bb):
        for h in range(heads):
            sl = slice(h * hd, (h + 1) * hd)
            q = _pad_rows(q_ref[s, :, sl].astype(F32), rows).astype(BF16)
            k = k_ref[s, pl.ds(h, nm, stride=heads), :].astype(BF16)
            v = v_ref[s, pl.ds(h, nm, stride=heads), :].astype(BF16)
            sc = _dot_nt(q, k) * scale
            e = jnp.exp(sc - jnp.max(sc, axis=-1, keepdims=True))
            den = jnp.sum(e, axis=-1, keepdims=True)
            o_ref[s, :, sl] = (_dot(e.astype(BF16), v) / den)[:T].astype(o_ref.dtype)


def xattn_decode(q3, mem_k, mem_v, layer, heads, out_dtype, seqs_per_step=8):
    B, T, w = q3.shape
    hd = w // heads
    bb = _tile(B, seqs_per_step, 1)
    kern = functools.partial(_xattn_decode_kernel, heads=heads, hd=hd, scale=hd ** -0.5)
    m_spec = pl.BlockSpec((None, bb) + mem_k.shape[2:], lambda b: (layer, b, 0, 0))
    return pl.pallas_call(
        kern, grid=(B // bb,),
        in_specs=[pl.BlockSpec((bb, T, w), lambda b: (b, 0, 0)), m_spec, m_spec],
        out_specs=pl.BlockSpec((bb, T, w), lambda b: (b, 0, 0)),
        out_shape=jax.ShapeDtypeStruct((B, T, w), out_dtype),
        compiler_params=_cparams(("parallel",)), name="xattn_decode",
    )(q3, mem_k, mem_v)


def _round_up(n, m):
    return -(-n // m) * m


def kernel(x_prompt, x_sample, state_gla, cache_swa_k, cache_swa_v, state_conv, state_ffn, cache_mem_k, cache_mem_v, mem_prompt, norm_mix, w_in, gla_gate_up, gla_gate_b, gla_norm, swa_sinks, rel_bias, conv_w, w_branch, w_out, norm_x, wx_q, wx_k, wx_v, wx_o, norm_ffn, ffn_up, ffn_conv_w, ffn_conv_b, ffn_down, norm_final):
    Bp, Tp, D = x_prompt.shape
    Bs, Ts, _ = x_sample.shape
    depth = w_in.shape[0]
    gla_heads, gla_dk, gla_dv = state_gla.shape[2:]
    swa_kv, swa_hd = cache_swa_k.shape[3:]
    swa_heads = swa_sinks.shape[1]
    x_heads, x_hd = cache_mem_k.shape[3:]
    n_mem = mem_prompt.shape[1]
    mix = conv_w.shape[2]
    dff = ffn_down.shape[1]
    rank = gla_gate_up.shape[1]
    gqk = gla_heads * gla_dk
    sqw, skw = swa_heads * swa_hd, swa_kv * swa_hd
    win = cache_swa_k.shape[2]
    xw = x_heads * x_hd
    assert win == WINDOW and mix == gla_heads * gla_dv == sqw

    w_in_t = jnp.swapaxes(w_in, 1, 2)
    a_w = 2 * gqk + 2 * mix
    s_w = sqw + 2 * skw
    o_glr, o_swa = a_w, a_w + rank
    o_conv = o_swa + s_w
    o_gates = o_conv + 3 * mix

    bias_p = _swa_bias_table(rel_bias, min(Tp, WINDOW), WINDOW, True)
    table_s = _swa_decode_table(rel_bias, swa_sinks, Ts, win, swa_kv)
    cache_k4 = cache_swa_k.reshape(depth, Bs, win, skw)
    cache_v4 = cache_swa_v.reshape(depth, Bs, win, skw)
    cmem_k4 = cache_mem_k.reshape(depth, Bs, n_mem * x_heads, x_hd)
    cmem_v4 = cache_mem_v.reshape(depth, Bs, n_mem * x_heads, x_hd)
    mem2 = mem_prompt.reshape(Bp * n_mem, D)

    def first8(st):
        return jnp.pad(st, ((0, 0), (SUBLANE - (CONV_W - 1), 0), (0, 0)))

    def trunk_layer(i, x, B, T, decode, s0, conv_prev8, ffn_prev8, mem_k, mem_v, act_dtype):
        M = B * T
        h = rmsnorm(x, norm_mix[i], BF16)
        zg, lg = matmul_nt_gate(h, w_in_t, i, a_w, o_glr, gla_gate_up, gla_gate_b, act_dtype, name="in_proj_gla")
        zs = matmul_nt(h, w_in_t, i, o_swa, s_w, act_dtype, tn=s_w // 2, name="in_proj_swa")
        zc = matmul_nt(h, w_in_t, i, o_conv, 3 * mix, act_dtype, name="in_proj_conv")
        zt = matmul_nt(h, w_in_t, i, o_gates, 3 * D, act_dtype, name="in_proj_gates")
        zs3 = zs.reshape(B, T, s_w)
        br_a, gla_st = gla(zg.reshape(B, T, a_w), lg.reshape(B, T, gqk), gla_norm, s0, i,
                           gla_heads, gla_dk, gla_dv, act_dtype, seqs_per_step=4 if decode else 1)
        if decode:
            br_b = swa_decode(zs3, cache_k4, cache_v4, table_s, i, swa_kv, swa_hd, act_dtype)
        else:
            br_b = swa(zs3, None, None, swa_sinks, bias_p, i, swa_kv, swa_hd, min(T, WINDOW), act_dtype)
        br_c, conv_st = conv_branch(zc, conv_prev8, conv_w, i, B, T, act_dtype)
        merged = merge(br_a.reshape(M, mix), br_b.reshape(M, mix), br_c, w_branch, i, zt, BF16)
        x = res_matmul(merged, w_out, x, i, name="out_proj")
        q = matmul(rmsnorm(x, norm_x[i], BF16), wx_q, i, act_dtype, name="xq_proj")
        if decode:
            xo = xattn_decode(q.reshape(B, T, xw), mem_k, mem_v, i, x_heads, act_dtype)
        else:
            xo = xattn(q.reshape(B, T, xw), mem_k, mem_v, i, x_heads, act_dtype)
        x = res_matmul(xo.reshape(M, xw), wx_o, x, i, name="xo_proj")
        act, ffn_st = ffn_first_half(rmsnorm(x, norm_ffn[i], BF16), ffn_up, i, ffn_prev8, ffn_conv_w[i],
                                     ffn_conv_b[i].reshape(1, dff), B, T, BF16)
        x = res_matmul(act, ffn_down, x, i, tm=1024, tn=256, name="ffn_down")
        k_new = zs3[:, T - min(T, WINDOW):, sqw:sqw + skw].astype(F32)
        v_new = zs3[:, T - min(T, WINDOW):, sqw + skw:].astype(F32)
        return x, (gla_st, k_new, v_new, conv_st[:, SUBLANE - (CONV_W - 1):],
                   ffn_st[:, SUBLANE - (CONV_W - 1):])

    xp = x_prompt.reshape(Bp * Tp, D)
    xs = x_sample.reshape(Bs * Ts, D)
    act_p = BF16
    act_s = BF16 if Ts % 16 == 0 else F32
    st_p, st_s = [], []
    for i in range(depth):
        mem_k = matmul(mem2, wx_k, i, F32, name="mem_k").reshape(Bp, n_mem, xw)
        mem_v = matmul(mem2, wx_v, i, F32, name="mem_v").reshape(Bp, n_mem, xw)
        xp, sp = trunk_layer(i, xp, Bp, Tp, False, None,
                             jnp.zeros((Bp, SUBLANE, mix), F32), jnp.zeros((Bp, SUBLANE, dff), F32),
                             mem_k, mem_v, act_p)
        xs, ss = trunk_layer(i, xs, Bs, Ts, True, state_gla, first8(state_conv[i]), first8(state_ffn[i]),
                             cmem_k4, cmem_v4, act_s)
        st_p.append(dict(gla=sp[0], k=sp[1].reshape(Bp, WINDOW, swa_kv, swa_hd),
                         v=sp[2].reshape(Bp, WINDOW, swa_kv, swa_hd), conv=sp[3], ffn=sp[4],
                         mem_k=mem_k.reshape(Bp, n_mem, x_heads, x_hd),
                         mem_v=mem_v.reshape(Bp, n_mem, x_heads, x_hd)))
        kc = jnp.concatenate([cache_k4[i], ss[1]], axis=1)[:, Ts:]
        vc = jnp.concatenate([cache_v4[i], ss[2]], axis=1)[:, Ts:]
        st_s.append(dict(gla=ss[0], k=kc.reshape(Bs, win, swa_kv, swa_hd), v=vc.reshape(Bs, win, swa_kv, swa_hd),
                         conv=ss[3], ffn=ss[4]))
    y_prompt = rmsnorm(xp, norm_final, F32).reshape(Bp, Tp, D)
    y_sample = rmsnorm(xs, norm_final, F32).reshape(Bs, Ts, D)

    def stack(lst, key):
        return jnp.stack([s[key] for s in lst])

    return (y_prompt, y_sample, stack(st_p, "gla"), stack(st_s, "gla"), stack(st_p, "k"), stack(st_p, "v"),
            stack(st_s, "k"), stack(st_s, "v"), stack(st_p, "conv"), stack(st_s, "conv"),
            stack(st_p, "ffn"), stack(st_s, "ffn"), stack(st_p, "mem_k"), stack(st_p, "mem_v"))
```

```python
import functools
import math

import jax
import jax.numpy as jnp
from jax import lax
from jax.experimental import pallas as pl
from jax.experimental.pallas import tpu as pltpu

F32 = jnp.float32
BF16 = jnp.bfloat16

EPS = 1e-6
GLA_TAU = 16.0
GLA_CHUNK = 64
GLA_SUB = 16
WINDOW = 128
N_BUCKETS = 32
MAX_DIST = 128
CONV_W = 3
NEG = -1e30

LANE = 128
SUBLANE = 8
MXU_COLS = 256
MXU_ROWS = 512
VMEM_LIMIT = 56 * 1024 * 1024


def _cparams(sem):
    return pltpu.CompilerParams(dimension_semantics=sem, vmem_limit_bytes=VMEM_LIMIT)


def _tile(n, pref, unit):
    if n <= pref:
        return n
    t = (pref // unit) * unit
    while t > unit and n % t:
        t -= unit
    assert n % t == 0, (n, pref, unit)
    return t


def _mxu_dtype(rows):
    return BF16 if rows % 16 == 0 else F32


def _sigmoid(x):
    return 1.0 / (1.0 + jnp.exp(-x))


def _dot(a, b):
    return jnp.dot(a, b, preferred_element_type=F32)


def _dot_nt(a, b):
    return lax.dot_general(a, b, (((1,), (1,)), ((), ())), preferred_element_type=F32)


def _col_chunks(n):
    if n % LANE:
        return [(0, n)]
    return [(c, min(MXU_COLS, n - c)) for c in range(0, n, MXU_COLS)]


def _row_chunks(n):
    m = MXU_ROWS if n % MXU_ROWS == 0 else n
    return [(r, m) for r in range(0, n, m)]


def _pad_rows(x, rows):
    if x.shape[0] == rows:
        return x
    return jnp.concatenate([x, jnp.zeros((rows - x.shape[0], x.shape[1]), x.dtype)], axis=0)


def _norm_kernel(x_ref, g_ref, o_ref):
    x = x_ref[...]
    ms = jnp.mean(x * x, axis=-1, keepdims=True)
    o_ref[...] = (x * lax.rsqrt(ms + EPS) * g_ref[...]).astype(o_ref.dtype)


def rmsnorm(x, gain, out_dtype, tm=512):
    M, K = x.shape
    tm = _tile(M, tm, SUBLANE)
    return pl.pallas_call(
        _norm_kernel, grid=(M // tm,),
        in_specs=[pl.BlockSpec((tm, K), lambda i: (i, 0)), pl.BlockSpec((1, K), lambda i: (0, 0))],
        out_specs=pl.BlockSpec((tm, K), lambda i: (i, 0)),
        out_shape=jax.ShapeDtypeStruct((M, K), out_dtype),
        compiler_params=_cparams(("parallel",)), name="rmsnorm")(x, gain.reshape(1, K))


def _mm_kernel(a_ref, w_ref, o_ref):
    for c, w in _col_chunks(o_ref.shape[1]):
        wc = w_ref[:, c:c + w].astype(BF16)
        for r, n in _row_chunks(o_ref.shape[0]):
            o_ref[r:r + n, c:c + w] = _dot(a_ref[r:r + n, :].astype(BF16), wc).astype(o_ref.dtype)


def _mm_nt_kernel(a_ref, wt_ref, o_ref):
    for c, w in _col_chunks(o_ref.shape[1]):
        wc = wt_ref[c:c + w, :].astype(BF16)
        for r, n in _row_chunks(o_ref.shape[0]):
            o_ref[r:r + n, c:c + w] = _dot_nt(a_ref[r:r + n, :], wc).astype(o_ref.dtype)


def _mm_nt_gate_kernel(a_ref, wt_ref, wglr_ref, up_ref, gb_ref, o_ref, lg_ref):
    _mm_nt_kernel(a_ref, wt_ref, o_ref)

    @pl.when(pl.program_id(1) == 0)
    def _():
        wg = _pad_rows(wglr_ref[...], LANE).astype(BF16)
        up = _pad_rows(up_ref[...], LANE).astype(BF16)
        for r, n in _row_chunks(o_ref.shape[0]):
            glr = _dot_nt(a_ref[r:r + n, :], wg)
            z = _dot(glr.astype(BF16), up) + gb_ref[...]
            lg_ref[r:r + n, :] = -(jnp.maximum(-z, 0.0) + jnp.log1p(jnp.exp(-jnp.abs(z)))) / GLA_TAU


def _res_mm_kernel(*refs):
    r_ref, o_ref = refs[-2:]
    pairs = list(zip(refs[0:-2:2], refs[1:-2:2]))
    for c, w in _col_chunks(o_ref.shape[1]):
        wcs = [w_ref[:, c:c + w].astype(BF16) for _, w_ref in pairs]
        for r, n in _row_chunks(o_ref.shape[0]):
            acc = r_ref[r:r + n, c:c + w]
            for (a_ref, _), wc in zip(pairs, wcs):
                acc = acc + _dot(a_ref[r:r + n, :].astype(BF16), wc)
            o_ref[r:r + n, c:c + w] = acc


def matmul(a, w, layer, out_dtype, tm=2048, tn=512, name="mm"):
    M, K = a.shape
    N = w.shape[-1]
    tm = _tile(M, tm, SUBLANE)
    tn = _tile(N, tn, LANE)
    return pl.pallas_call(
        _mm_kernel, grid=(M // tm, N // tn),
        in_specs=[pl.BlockSpec((tm, K), lambda i, j: (i, 0)),
                  pl.BlockSpec((None, K, tn), lambda i, j: (layer, 0, j))],
        out_specs=pl.BlockSpec((tm, tn), lambda i, j: (i, j)),
        out_shape=jax.ShapeDtypeStruct((M, N), out_dtype),
        compiler_params=_cparams(("parallel", "arbitrary")), name=name)(a, w)


def _wt_spec(layer, row0, tn, K):
    return pl.BlockSpec((None, pl.Element(tn), pl.Element(K)),
                        lambda i, j: (layer, pl.multiple_of(row0 + j * tn, SUBLANE), 0))


def matmul_nt(a, wt, layer, row0, n, out_dtype, tm=2048, tn=512, name="mm_nt"):
    M, K = a.shape
    assert row0 % SUBLANE == 0
    tm = _tile(M, tm, SUBLANE)
    tn = _tile(n, tn, LANE)
    return pl.pallas_call(
        _mm_nt_kernel, grid=(M // tm, n // tn),
        in_specs=[pl.BlockSpec((tm, K), lambda i, j: (i, 0)), _wt_spec(layer, row0, tn, K)],
        out_specs=pl.BlockSpec((tm, tn), lambda i, j: (i, j)),
        out_shape=jax.ShapeDtypeStruct((M, n), out_dtype),
        compiler_params=_cparams(("parallel", "arbitrary")), name=name)(a, wt)


def matmul_nt_gate(a, wt, layer, n, glr_row0, gate_up, gate_b, out_dtype, tm=2048, tn=512, name="mm_nt_gate"):
    M, K = a.shape
    depth, rank, gw = gate_up.shape
    assert glr_row0 % SUBLANE == 0 and rank % SUBLANE == 0
    tm = _tile(M, tm, SUBLANE)
    tn = _tile(n, tn, LANE)
    return pl.pallas_call(
        _mm_nt_gate_kernel, grid=(M // tm, n // tn),
        in_specs=[pl.BlockSpec((tm, K), lambda i, j: (i, 0)), _wt_spec(layer, 0, tn, K),
                  pl.BlockSpec((None, pl.Element(rank), pl.Element(K)), lambda i, j: (layer, glr_row0, 0)),
                  pl.BlockSpec((None, rank, gw), lambda i, j: (layer, 0, 0)),
                  pl.BlockSpec((None, 1, gw), lambda i, j: (layer, 0, 0))],
        out_specs=[pl.BlockSpec((tm, tn), lambda i, j: (i, j)), pl.BlockSpec((tm, gw), lambda i, j: (i, 0))],
        out_shape=[jax.ShapeDtypeStruct((M, n), out_dtype), jax.ShapeDtypeStruct((M, gw), F32)],
        compiler_params=_cparams(("parallel", "arbitrary")), name=name,
    )(a, wt, wt, gate_up, gate_b.reshape(depth, 1, gw))


def res_matmul(a_parts, w, res, layer, tm=2048, tn=512, name="res_mm"):
    M = a_parts[0].shape[0]
    N = w.shape[-1]
    tm = _tile(M, tm, SUBLANE)
    tn = _tile(N, tn, LANE)
    in_specs, args, row0 = [], [], 0
    for a in a_parts:
        kp = a.shape[1]
        assert row0 % SUBLANE == 0
        in_specs += [pl.BlockSpec((tm, kp), lambda i, j: (i, 0)),
                     pl.BlockSpec((None, pl.Element(kp), pl.Element(tn)),
                                  lambda i, j, r0=row0: (layer, r0, pl.multiple_of(j * tn, LANE)))]
        args += [a, w]
        row0 += kp
    assert row0 == w.shape[1]
    return pl.pallas_call(
        _res_mm_kernel, grid=(M // tm, N // tn),
        in_specs=in_specs + [pl.BlockSpec((tm, tn), lambda i, j: (i, j))],
        out_specs=pl.BlockSpec((tm, tn), lambda i, j: (i, j)),
        out_shape=jax.ShapeDtypeStruct((M, N), F32),
        compiler_params=_cparams(("parallel", "arbitrary")), name=name)(*args, res)


def _gla_chunk(q, k, v, lg, st, md, lp, consts):
    row, rsub, lsub, ones = consts
    c = min(GLA_SUB, lp)
    b = lg
    sh = 1
    while sh < lp:
        b = b + jnp.where(row >= sh, pltpu.roll(b, sh, 0), 0.0)
        sh *= 2
    o = _dot_nt((q * jnp.exp(b)).astype(md), st.astype(md))
    a_rows = []
    for i in range(lp // c):
        lo = i * c
        qb, kb, bb = q[lo:lo + c], k[lo:lo + c], b[lo:lo + c]
        ws = []
        for s in range(c):
            d = bb - bb[s:s + 1, :]
            ws.append(qb * kb[s:s + 1, :] * jnp.exp(jnp.where(rsub >= s, d, NEG)))
        rsum = _dot(jnp.concatenate(ws, axis=0).astype(md), ones)
        a_i = jnp.zeros((c, LANE), F32)
        for s in range(c):
            a_i = jnp.where(lsub == lo + s, rsum[s * c:(s + 1) * c], a_i)
        if i > 0:
            ref = b[lo - 1:lo, :]
            qt = qb * jnp.exp(bb - ref)
            kt = k * jnp.exp(jnp.where(row < lo, ref - b, NEG))
            a_i = a_i + _dot_nt(qt.astype(md), _pad_rows(kt, LANE).astype(md))
        a_rows.append(a_i)
    a = jnp.concatenate(a_rows, axis=0)
    vk = _pad_rows(v, LANE)
    o = o + _dot(a.astype(md), vk.astype(md))
    bend = b[lp - 1:lp, :]
    kd = _pad_rows(k * jnp.exp(bend - b), LANE)
    st = st * jnp.exp(bend) + _dot(vk.T.astype(md), kd.astype(md))
    return o, st


def _gla_kernel(*refs, lin, lp, n_chunks, heads, scale, has_s0):
    if has_s0:
        q_ref, k_ref, v_ref, r_ref, lg_ref, gn_ref, s0_ref, o_ref, st_ref, s_scr = refs
    else:
        q_ref, k_ref, v_ref, r_ref, lg_ref, gn_ref, o_ref, st_ref, s_scr = refs
    t = pl.program_id(1)
    bb = q_ref.shape[0]
    dk = q_ref.shape[-1] // heads
    dv = v_ref.shape[-1] // heads
    c = min(GLA_SUB, lp)
    md = _mxu_dtype(lp)
    pairs = [(s, h) for s in range(bb) for h in range(heads)]

    @pl.when(t == 0)
    def _():
        for s, h in pairs:
            s_scr[s, h] = s0_ref[s, h].T if has_s0 else jnp.zeros((dv, dk), F32)

    row = lax.broadcasted_iota(jnp.int32, (lp, dk), 0)
    consts = (row, lax.broadcasted_iota(jnp.int32, (c, dk), 0),
              lax.broadcasted_iota(jnp.int32, (c, LANE), 1), jnp.ones((dk, LANE), md))

    def chunk(ci, carry):
        rows = pl.ds(pl.multiple_of(ci * lin, lin), lin)
        for s, h in pairs:
            kcols, vcols = slice(h * dk, (h + 1) * dk), slice(h * dv, (h + 1) * dv)
            q = _pad_rows(q_ref[s, rows, kcols].astype(F32), lp) * scale
            k = _pad_rows(k_ref[s, rows, kcols].astype(F32), lp)
            v = _pad_rows(v_ref[s, rows, vcols].astype(F32), lp)
            lg = _pad_rows(lg_ref[s, rows, kcols], lp)
            o, st = _gla_chunk(q, k, v, lg, s_scr[s, h], md, lp, consts)
            s_scr[s, h] = st
            o = o[:lin]
            ms = jnp.mean(o * o, axis=-1, keepdims=True)
            y = o * lax.rsqrt(ms + EPS) * gn_ref[...]
            r = r_ref[s, rows, vcols].astype(F32)
            o_ref[s, rows, vcols] = (y * (r * _sigmoid(r))).astype(o_ref.dtype)
        return carry

    lax.fori_loop(0, n_chunks, chunk, 0)

    @pl.when(t == pl.num_programs(1) - 1)
    def _():
        for s, h in pairs:
            st_ref[s, h] = s_scr[s, h].T


def gla(zg, lg, gnorm, s0, layer, heads, dk, dv, out_dtype, seqs_per_step=1):
    B, T, _ = zg.shape
    depth = gnorm.shape[0]
    qw, vw = heads * dk, heads * dv
    assert vw % qw == 0
    lin = min(GLA_CHUNK, T)
    assert T % lin == 0
    lp = max(lin, 16)
    tt = _tile(T, 512, lin)
    bb = _tile(B, seqs_per_step, 1)
    in_specs = [pl.BlockSpec((bb, tt, qw), lambda b, t: (b, t, 0)),
                pl.BlockSpec((bb, tt, qw), lambda b, t: (b, t, 1)),
                pl.BlockSpec((bb, tt, vw), lambda b, t: (b, t, 2 * qw // vw)),
                pl.BlockSpec((bb, tt, vw), lambda b, t: (b, t, 2 * qw // vw + 1)),
                pl.BlockSpec((bb, tt, qw), lambda b, t: (b, t, 0)),
                pl.BlockSpec((None, 1, dv), lambda b, t: (layer, 0, 0))]
    args = [zg, zg, zg, zg, lg, gnorm.reshape(depth, 1, dv)]
    if s0 is not None:
        in_specs.append(pl.BlockSpec((None, bb, heads, dk, dv), lambda b, t: (layer, b, 0, 0, 0)))
        args.append(s0)
    kern = functools.partial(_gla_kernel, lin=lin, lp=lp, n_chunks=tt // lin, heads=heads, scale=dk ** -0.5,
                             has_s0=s0 is not None)
    return pl.pallas_call(
        kern, grid=(B // bb, T // tt), in_specs=in_specs,
        out_specs=[pl.BlockSpec((bb, tt, vw), lambda b, t: (b, t, 0)),
                   pl.BlockSpec((bb, heads, dk, dv), lambda b, t: (b, 0, 0, 0))],
        out_shape=[jax.ShapeDtypeStruct((B, T, vw), out_dtype),
                   jax.ShapeDtypeStruct((B, heads, dk, dv), F32)],
        scratch_shapes=[pltpu.VMEM((bb, heads, dv, dk), F32)],
        compiler_params=_cparams(("parallel", "arbitrary")), name="gla",
    )(*args)


def _swa_kernel(sink_ref, q_ref, kp_ref, kc_ref, vp_ref, vc_ref, bias_ref, o_ref, *,
                layer, heads, group, hd, scale):
    tq = q_ref.shape[0]
    md = _mxu_dtype(tq)
    k = jnp.concatenate([kp_ref[...].astype(md), kc_ref[...].astype(md)], axis=0)
    v = jnp.concatenate([vp_ref[...].astype(md), vc_ref[...].astype(md)], axis=0)
    outs = []
    for h in range(heads):
        kv = h // group
        qh = (q_ref[:, h * hd:(h + 1) * hd].astype(F32) * scale).astype(md)
        s = _dot_nt(qh, k[:, kv * hd:(kv + 1) * hd]) + bias_ref[h]
        sink = sink_ref[layer, h]
        m = jnp.maximum(jnp.max(s, axis=-1, keepdims=True), sink)
        e = jnp.exp(s - m)
        den = jnp.sum(e, axis=-1, keepdims=True) + jnp.exp(sink - m)
        outs.append(_dot(e.astype(md), v[:, kv * hd:(kv + 1) * hd]) / den)
    per = LANE // hd
    for j in range(heads // per):
        o_ref[:, j * LANE:(j + 1) * LANE] = jnp.concatenate(
            outs[j * per:(j + 1) * per], axis=-1).astype(o_ref.dtype)


def _t5_bucket(dist):
    n = jnp.maximum(dist, 0)
    max_exact = N_BUCKETS // 2
    nf = jnp.maximum(n, 1).astype(F32)
    large = max_exact + (jnp.log(nf / max_exact) / math.log(MAX_DIST / max_exact)
                         * (N_BUCKETS - max_exact)).astype(jnp.int32)
    large = jnp.minimum(large, N_BUCKETS - 1)
    return jnp.where(n < max_exact, n, large)


def _swa_bias_table(rel_bias, tq, p, with_first):
    i = jnp.arange(tq)[:, None]
    j = jnp.arange(p + tq)[None, :]
    dist = p + i - j
    valid = (dist >= 0) & (dist <= WINDOW)
    onehot = (_t5_bucket(dist)[:, :, None] == jnp.arange(N_BUCKETS)).astype(F32)
    bias = jnp.einsum("ijb,bh->hij", onehot, rel_bias.astype(F32), precision=lax.Precision.HIGHEST)
    regular = jnp.where(valid[None], bias, NEG)
    if not with_first:
        return regular[None]
    return jnp.stack([jnp.where((valid & (j >= p))[None], bias, NEG), regular])


def swa(z3, prev_k, prev_v, sinks, bias, layer, kv_heads, hd, tq, out_dtype):
    B, T, _ = z3.shape
    heads = sinks.shape[1]
    qw, kw = heads * hd, kv_heads * hd
    assert qw % kw == 0
    ko, vo = qw // kw, qw // kw + 1
    nb = T // tq
    prompt = prev_k is None
    p = bias.shape[3] - tq
    if prompt:
        assert p == tq and bias.shape[0] == 2
        prev_specs = [pl.BlockSpec((None, p, kw), lambda b, n: (b, jnp.maximum(n - 1, 0), ko)),
                      pl.BlockSpec((None, p, kw), lambda b, n: (b, jnp.maximum(n - 1, 0), vo))]
        prev_k = prev_v = z3
        bias_spec = pl.BlockSpec((None,) + bias.shape[1:], lambda b, n: (jnp.minimum(n, 1), 0, 0, 0))
    else:
        assert nb == 1 and bias.shape[0] == 1
        prev_specs = [pl.BlockSpec((None, None, p, kw), lambda b, n: (layer, b, 0, 0))] * 2
        bias_spec = pl.BlockSpec((None,) + bias.shape[1:], lambda b, n: (0, 0, 0, 0))
    kern = functools.partial(_swa_kernel, layer=layer, heads=heads, group=heads // kv_heads, hd=hd,
                             scale=hd ** -0.5)
    return pl.pallas_call(
        kern, grid=(B, nb),
        in_specs=[pl.BlockSpec(memory_space=pltpu.SMEM),
                  pl.BlockSpec((None, tq, qw), lambda b, n: (b, n, 0)),
                  prev_specs[0],
                  pl.BlockSpec((None, tq, kw), lambda b, n: (b, n, ko)),
                  prev_specs[1],
                  pl.BlockSpec((None, tq, kw), lambda b, n: (b, n, vo)),
                  bias_spec],
        out_specs=pl.BlockSpec((None, tq, qw), lambda b, n: (b, n, 0)),
        out_shape=jax.ShapeDtypeStruct((B, T, qw), out_dtype),
        compiler_params=_cparams(("parallel", "arbitrary")), name="swa",
    )(sinks, z3, prev_k, z3, prev_v, z3, bias)


def _swa_decode_kernel(q_ref, kc_ref, kn_ref, vc_ref, vn_ref, tab_ref, o_ref, *, kv_heads, group, hd, scale):
    bb, T, _ = q_ref.shape
    kw = kc_ref.shape[-1]
    npad = tab_ref.shape[-1] - kc_ref.shape[1] - T
    md = _mxu_dtype(group * T)
    per = LANE // hd
    for s in range(bb):
        zpad = jnp.zeros((npad, kw), F32)
        k = jnp.concatenate([kc_ref[s].astype(F32), kn_ref[s].astype(F32), zpad], axis=0).astype(md)
        v = jnp.concatenate([vc_ref[s].astype(F32), vn_ref[s].astype(F32), zpad], axis=0).astype(md)
        outs = []
        for g in range(kv_heads):
            qs = jnp.concatenate([q_ref[s, :, (g * group + j) * hd:(g * group + j + 1) * hd].astype(F32)
                                  for j in range(group)], axis=0) * scale
            sc = _dot_nt(qs.astype(md), k[:, g * hd:(g + 1) * hd]) + tab_ref[g]
            e = jnp.exp(sc - jnp.max(sc, axis=-1, keepdims=True))
            den = jnp.sum(e, axis=-1, keepdims=True)
            o = _dot(e.astype(md), v[:, g * hd:(g + 1) * hd]) / den
            outs += [o[j * T:(j + 1) * T] for j in range(group)]
        for j in range(len(outs) // per):
            o_ref[s, :, j * LANE:(j + 1) * LANE] = jnp.concatenate(
                outs[j * per:(j + 1) * per], axis=-1).astype(o_ref.dtype)


def _swa_decode_table(rel_bias, sinks, T, p, kv_heads):
    depth, heads = sinks.shape
    group = heads // kv_heads
    base = _swa_bias_table(rel_bias, T, p, False)[0].reshape(kv_heads, group * T, p + T)
    npad = _round_up(p + T + 1, 16) - (p + T)
    sink_col = jnp.repeat(sinks.astype(F32).reshape(depth, kv_heads, group), T, axis=2)[..., None]
    parts = [jnp.broadcast_to(base[None], (depth,) + base.shape), sink_col,
             jnp.full((depth, kv_heads, group * T, npad - 1), NEG, F32)]
    return jnp.concatenate(parts, axis=-1)


def swa_decode(z3, cache_k, cache_v, table, layer, kv_heads, hd, out_dtype, seqs_per_step=8):
    B, T, zw = z3.shape
    kw = kv_heads * hd
    qw = zw - 2 * kw
    p = cache_k.shape[2]
    bb = _tile(B, seqs_per_step, 1)
    kern = functools.partial(_swa_decode_kernel, kv_heads=kv_heads, group=qw // kw, hd=hd, scale=hd ** -0.5)
    c_spec = pl.BlockSpec((None, bb, p, kw), lambda b: (layer, b, 0, 0))
    return pl.pallas_call(
        kern, grid=(B // bb,),
        in_specs=[pl.BlockSpec((bb, T, qw), lambda b: (b, 0, 0)),
                  c_spec, pl.BlockSpec((bb, T, kw), lambda b: (b, 0, qw // kw)),
                  c_spec, pl.BlockSpec((bb, T, kw), lambda b: (b, 0, qw // kw + 1)),
                  pl.BlockSpec((None,) + table.shape[1:], lambda b: (layer, 0, 0, 0))],
        out_specs=pl.BlockSpec((bb, T, qw), lambda b: (b, 0, 0)),
        out_shape=jax.ShapeDtypeStruct((B, T, qw), out_dtype),
        compiler_params=_cparams(("parallel",)), name="swa_decode",
    )(z3, cache_k, z3, cache_v, z3, table)


class _ConvTaps:
    def __init__(self, i, j, cols, tm, prev_ref, st_ref, carry_scr, seq_len):
        self.j, self.cols, self.seq_len = j, cols, seq_len
        self.prev_ref, self.st_ref, self.carry_scr = prev_ref, st_ref, carry_scr
        self.single = tm <= seq_len
        if self.single:
            @pl.when((i % (seq_len // tm)) == 0)
            def _():
                carry_scr[j, :, cols] = prev_ref[0, :, cols]

            self.before = carry_scr[j, :, cols]

    def chunk(self, x, r):
        n, w = x.shape
        if self.single:
            before = self.before
            row8 = lax.broadcasted_iota(jnp.int32, (SUBLANE, w), 0)

            def back(s):
                xr = pltpu.roll(x, s, 0)
                top = jnp.where(row8 < s, pltpu.roll(before, s, 0), xr[0:SUBLANE])
                return jnp.concatenate([top, xr[SUBLANE:]], axis=0)

            self.before = x[n - SUBLANE:n]
            return back(1), back(2)
        t = self.seq_len
        assert r % t == 0 and n % t == 0
        seqs = slice(r // t, (r + n) // t)
        tix = lax.rem(lax.broadcasted_iota(jnp.int32, (n, w), 0), t)
        s1 = jnp.broadcast_to(self.prev_ref[seqs, SUBLANE - 1:SUBLANE, self.cols], (n // t, t, w)).reshape(n, w)
        s0 = jnp.broadcast_to(self.prev_ref[seqs, SUBLANE - 2:SUBLANE - 1, self.cols], (n // t, t, w)).reshape(n, w)
        self.st_ref[seqs, :, self.cols] = x.reshape(n // t, t, w)[:, t - SUBLANE:, :]
        x1 = jnp.where(tix == 0, s1, pltpu.roll(x, 1, 0))
        x2 = jnp.where(tix == 0, s0, jnp.where(tix == 1, s1, pltpu.roll(x, 2, 0)))
        return x1, x2

    def finish(self):
        if self.single:
            self.carry_scr[self.j, :, self.cols] = self.before
            self.st_ref[0, :, self.cols] = self.before


def _row_blocking(B, T, tm):
    if T >= tm:
        tm = _tile(T, tm, SUBLANE)
        return tm, 1, (lambda i: i // (T // tm))
    assert T % SUBLANE == 0
    nseq = _tile(B, max(tm // T, 1), 1)
    return nseq * T, nseq, (lambda i: i)


def _conv_branch_kernel(cb_ref, cc_ref, ch_ref, prev_ref, w_ref, o_ref, st_ref, carry_scr, *, seq_len):
    i, j = pl.program_id(0), pl.program_id(1)
    taps = _ConvTaps(i, j, slice(None), o_ref.shape[0], prev_ref, st_ref, carry_scr, seq_len)
    for r, n in _row_chunks(o_ref.shape[0]):
        rows = slice(r, r + n)
        u = cc_ref[rows, :].astype(F32) * ch_ref[rows, :].astype(F32)
        u1, u2 = taps.chunk(u, r)
        y = u2 * w_ref[0:1, :] + u1 * w_ref[1:2, :] + u * w_ref[2:3, :]
        o_ref[rows, :] = (cb_ref[rows, :].astype(F32) * y).astype(o_ref.dtype)
    taps.finish()


def conv_branch(z, prev8, conv_w, layer, B, T, out_dtype, tm=1024, tc=512):
    M = B * T
    width = conv_w.shape[2]
    tc = _tile(width, tc, LANE)
    nc = width // tc
    tm, nseq, seq_of = _row_blocking(B, T, tm)
    n_st = (M // tm) * nseq
    z_specs = [pl.BlockSpec((tm, tc), (lambda i, j, o=n * nc: (i, o + j))) for n in range(3)]
    out, st = pl.pallas_call(
        functools.partial(_conv_branch_kernel, seq_len=T), grid=(M // tm, nc),
        in_specs=z_specs + [pl.BlockSpec((nseq, SUBLANE, tc), lambda i, j: (seq_of(i), 0, j)),
                            pl.BlockSpec((None, CONV_W, tc), lambda i, j: (layer, 0, j))],
        out_specs=[pl.BlockSpec((tm, tc), lambda i, j: (i, j)),
                   pl.BlockSpec((nseq, SUBLANE, tc), lambda i, j: (i, 0, j))],
        out_shape=[jax.ShapeDtypeStruct((M, width), out_dtype),
                   jax.ShapeDtypeStruct((n_st, SUBLANE, width), F32)],
        scratch_shapes=[pltpu.VMEM((nc, SUBLANE, tc), F32)],
        compiler_params=_cparams(("arbitrary", "arbitrary")), name="conv_branch",
    )(z, z, z, prev8, conv_w)
    return out, st.reshape(B, n_st // B, SUBLANE, width)[:, -1]


def _ffn_up_kernel(h_ref, wu_ref, wg_ref, prev_ref, cw_ref, cb_ref, act_ref, st_ref, carry_scr, *, seq_len):
    i, j = pl.program_id(0), pl.program_id(1)
    tm, tn = act_ref.shape
    for c, w in _col_chunks(tn):
        cols = slice(c, c + w)
        wu = wu_ref[:, cols].astype(BF16)
        wg = wg_ref[:, cols].astype(BF16)
        taps = _ConvTaps(i, j, cols, tm, prev_ref, st_ref, carry_scr, seq_len)
        for r, n in _row_chunks(tm):
            h = h_ref[r:r + n, :]
            g = _dot(h, wg)
            g1, g2 = taps.chunk(g, r)
            gc = g2 * cw_ref[0:1, cols] + g1 * cw_ref[1:2, cols] + g * cw_ref[2:3, cols] + cb_ref[:, cols]
            act_ref[r:r + n, cols] = (gc * _sigmoid(gc) * _dot(h, wu)).astype(act_ref.dtype)
        taps.finish()


def ffn_first_half(h, w_up, layer, col0, width, prev8, conv_w, conv_b, B, T, out_dtype, tm=2048, tn=512):
    M, K = h.shape
    dff = w_up.shape[2] // 2
    tn = min(tn, width)
    assert col0 % LANE == 0 and dff % LANE == 0 and width % tn == 0
    nj = width // tn
    tm, nseq, seq_of = _row_blocking(B, T, tm)
    n_st = (M // tm) * nseq

    def col(j, base=col0):
        return pl.multiple_of(base + j * tn, LANE)

    def win(rows):
        return pl.BlockSpec((pl.Element(rows), pl.Element(tn)), lambda i, j: (0, col(j)))

    act, st = pl.pallas_call(
        functools.partial(_ffn_up_kernel, seq_len=T), grid=(M // tm, nj),
        in_specs=[pl.BlockSpec((tm, K), lambda i, j: (i, 0)),
                  pl.BlockSpec((None, pl.Element(K), pl.Element(tn)), lambda i, j: (layer, 0, col(j))),
                  pl.BlockSpec((None, pl.Element(K), pl.Element(tn)), lambda i, j: (layer, 0, col(j, dff + col0))),
                  pl.BlockSpec((pl.Element(nseq), pl.Element(SUBLANE), pl.Element(tn)),
                               lambda i, j: (seq_of(i) * nseq, 0, col(j))),
                  win(CONV_W), win(1)],
        out_specs=[pl.BlockSpec((tm, tn), lambda i, j: (i, j)),
                   pl.BlockSpec((nseq, SUBLANE, tn), lambda i, j: (i, 0, j))],
        out_shape=[jax.ShapeDtypeStruct((M, width), out_dtype),
                   jax.ShapeDtypeStruct((n_st, SUBLANE, width), F32)],
        scratch_shapes=[pltpu.VMEM((nj, SUBLANE, tn), F32)],
        compiler_params=_cparams(("arbitrary", "arbitrary")), name="ffn_up",
    )(h, w_up, w_up, prev8, conv_w, conv_b)
    return act, st.reshape(B, n_st // B, SUBLANE, width)[:, -1]


def _merge_kernel(a_ref, b_ref, c_ref, wb_ref, g0_ref, g1_ref, g2_ref, o_ref):
    xs = (a_ref, b_ref, c_ref)
    gs = (g0_ref, g1_ref, g2_ref)
    for c, w in _col_chunks(o_ref.shape[1]):
        cols = slice(c, c + w)
        ws = [wb_ref[n, :, cols].astype(BF16) for n in range(3)]
        for r, m in _row_chunks(o_ref.shape[0]):
            rows = slice(r, r + m)

            def term(n):
                return _sigmoid(gs[n][rows, cols].astype(F32)) * _dot(xs[n][rows, :].astype(BF16), ws[n])

            o_ref[rows, cols] = ((term(0) + term(1)) + term(2)).astype(o_ref.dtype)


def merge(br_a, br_b, br_c, wb, layer, zg, out_dtype, tm=1024, tn=512):
    M, mix = br_a.shape
    d = wb.shape[3]
    tm = _tile(M, tm, SUBLANE)
    tn = _tile(d, tn, LANE)
    br_spec = pl.BlockSpec((tm, mix), lambda i, j: (i, 0))
    g_specs = [pl.BlockSpec((tm, tn), (lambda i, j, o=n * d // tn: (i, o + j))) for n in range(3)]
    return pl.pallas_call(
        _merge_kernel, grid=(M // tm, d // tn),
        in_specs=[br_spec, br_spec, br_spec,
                  pl.BlockSpec((None, 3, mix, tn), lambda i, j: (layer, 0, 0, j))] + g_specs,
        out_specs=pl.BlockSpec((tm, tn), lambda i, j: (i, j)),
        out_shape=jax.ShapeDtypeStruct((M, d), out_dtype),
        compiler_params=_cparams(("parallel", "arbitrary")), name="merge",
    )(br_a, br_b, br_c, wb, zg, zg, zg)


def _xattn_kernel(q_ref, k_ref, v_ref, o_ref, *, heads, hd, scale):
    md = _mxu_dtype(q_ref.shape[0])
    for h in range(heads):
        sl = slice(h * hd, (h + 1) * hd)
        s = _dot_nt(q_ref[:, sl].astype(md), k_ref[:, sl].astype(md)) * scale
        e = jnp.exp(s - jnp.max(s, axis=-1, keepdims=True))
        den = jnp.sum(e, axis=-1, keepdims=True)
        o_ref[:, sl] = (_dot(e.astype(md), v_ref[:, sl].astype(md)) / den).astype(o_ref.dtype)


def xattn(q3, mem_k, mem_v, layer, heads, out_dtype, tq=512):
    B, T, w = q3.shape
    nm = mem_k.shape[-2]
    hd = w // heads
    tq = _tile(T, tq, SUBLANE)
    kern = functools.partial(_xattn_kernel, heads=heads, hd=hd, scale=hd ** -0.5)
    if mem_k.ndim == 4:
        m_spec = pl.BlockSpec((None, None, nm, w), lambda b, t: (layer, b, 0, 0))
    else:
        m_spec = pl.BlockSpec((None, nm, w), lambda b, t: (b, 0, 0))
    return pl.pallas_call(
        kern, grid=(B, T // tq),
        in_specs=[pl.BlockSpec((None, tq, w), lambda b, t: (b, t, 0)), m_spec, m_spec],
        out_specs=pl.BlockSpec((None, tq, w), lambda b, t: (b, t, 0)),
        out_shape=jax.ShapeDtypeStruct((B, T, w), out_dtype),
        compiler_params=_cparams(("parallel", "arbitrary")), name="xattn",
    )(q3, mem_k, mem_v)


def _xattn_decode_kernel(q_ref, k_ref, v_ref, o_ref, *, heads, hd, scale):
    bb, T, _ = q_ref.shape
    nm = k_ref.shape[1] // heads
    rows = max(T, 16)
    for s in range(bb):
        for h in range(heads):
            sl = slice(h * hd, (h + 1) * hd)
            q = _pad_rows(q_ref[s, :, sl].astype(F32), rows).astype(BF16)
            k = k_ref[s, pl.ds(h, nm, stride=heads), :].astype(BF16)
            v = v_ref[s, pl.ds(h, nm, stride=heads), :].astype(BF16)
            sc = _dot_nt(q, k) * scale
            e = jnp.exp(sc - jnp.max(sc, axis=-1, keepdims=True))
            den = jnp.sum(e, axis=-1, keepdims=True)
            o_ref[s, :, sl] = (_dot(e.astype(BF16), v) / den)[:T].astype(o_ref.dtype)


def xattn_decode(q3, mem_k, mem_v, layer, heads, out_dtype, seqs_per_step=8):
    B, T, w = q3.shape
    hd = w // heads
    bb = _tile(B, seqs_per_step, 1)
    kern = functools.partial(_xattn_decode_kernel, heads=heads, hd=hd, scale=hd ** -0.5)
    m_spec = pl.BlockSpec((None, bb) + mem_k.shape[2:], lambda b: (layer, b, 0, 0))
    return pl.pallas_call(
        kern, grid=(B // bb,),
        in_specs=[pl.BlockSpec((bb, T, w), lambda b: (b, 0, 0)), m_spec, m_spec],
        out_specs=pl.BlockSpec((bb, T, w), lambda b: (b, 0, 0)),
        out_shape=jax.ShapeDtypeStruct((B, T, w), out_dtype),
        compiler_params=_cparams(("parallel",)), name="xattn_decode",
    )(q3, mem_k, mem_v)


def _round_up(n, m):
    return -(-n // m) * m


def kernel(x_prompt, x_sample, state_gla, cache_swa_k, cache_swa_v, state_conv, state_ffn, cache_mem_k, cache_mem_v, mem_prompt, norm_mix, w_in, gla_gate_up, gla_gate_b, gla_norm, swa_sinks, rel_bias, conv_w, w_branch, w_out, norm_x, wx_q, wx_k, wx_v, wx_o, norm_ffn, ffn_up, ffn_conv_w, ffn_conv_b, ffn_down, norm_final):
    Bp, Tp, D = x_prompt.shape
    Bs, Ts, _ = x_sample.shape
    depth = w_in.shape[0]
    gla_heads, gla_dk, gla_dv = state_gla.shape[2:]
    swa_kv, swa_hd = cache_swa_k.shape[3:]
    swa_heads = swa_sinks.shape[1]
    x_heads, x_hd = cache_mem_k.shape[3:]
    n_mem = mem_prompt.shape[1]
    mix = conv_w.shape[2]
    dff = ffn_down.shape[1]
    rank = gla_gate_up.shape[1]
    gqk = gla_heads * gla_dk
    sqw, skw = swa_heads * swa_hd, swa_kv * swa_hd
    win = cache_swa_k.shape[2]
    xw = x_heads * x_hd
    assert win == WINDOW and mix == gla_heads * gla_dv == sqw

    w_in_t = jnp.swapaxes(w_in, 1, 2)
    a_w = 2 * gqk + 2 * mix
    s_w = sqw + 2 * skw
    o_glr, o_swa = a_w, a_w + rank
    o_conv = o_swa + s_w
    o_gates = o_conv + 3 * mix
    ffn_main = dff // 512 * 512
    ffn_parts = [(0, ffn_main)] + ([(ffn_main, dff - ffn_main)] if dff > ffn_main else [])

    bias_p = _swa_bias_table(rel_bias, min(Tp, WINDOW), WINDOW, True)
    table_s = _swa_decode_table(rel_bias, swa_sinks, Ts, win, swa_kv)
    cache_k4 = cache_swa_k.reshape(depth, Bs, win, skw)
    cache_v4 = cache_swa_v.reshape(depth, Bs, win, skw)
    cmem_k4 = cache_mem_k.reshape(depth, Bs, n_mem * x_heads, x_hd)
    cmem_v4 = cache_mem_v.reshape(depth, Bs, n_mem * x_heads, x_hd)
    mem2 = mem_prompt.reshape(Bp * n_mem, D)

    def first8(st):
        return jnp.pad(st, ((0, 0), (SUBLANE - (CONV_W - 1), 0), (0, 0)))

    def trunk_layer(i, x, B, T, decode, s0, conv_prev8, ffn_prev8, mem_k, mem_v, act_dtype):
        M = B * T
        h = rmsnorm(x, norm_mix[i], BF16)
        zg, lg = matmul_nt_gate(h, w_in_t, i, a_w, o_glr, gla_gate_up, gla_gate_b, act_dtype, name="in_proj_gla")
        zs = matmul_nt(h, w_in_t, i, o_swa, s_w, act_dtype, tn=s_w // 2, name="in_proj_swa")
        zc = matmul_nt(h, w_in_t, i, o_conv, 3 * mix, act_dtype, name="in_proj_conv")
        zt = matmul_nt(h, w_in_t, i, o_gates, 3 * D, act_dtype, name="in_proj_gates")
        zs3 = zs.reshape(B, T, s_w)
        br_a, gla_st = gla(zg.reshape(B, T, a_w), lg.reshape(B, T, gqk), gla_norm, s0, i,
                           gla_heads, gla_dk, gla_dv, act_dtype, seqs_per_step=4 if decode else 1)
        if decode:
            br_b = swa_decode(zs3, cache_k4, cache_v4, table_s, i, swa_kv, swa_hd, act_dtype)
        else:
            br_b = swa(zs3, None, None, swa_sinks, bias_p, i, swa_kv, swa_hd, min(T, WINDOW), act_dtype)
        br_c, conv_st = conv_branch(zc, conv_prev8, conv_w, i, B, T, act_dtype)
        merged = merge(br_a.reshape(M, mix), br_b.reshape(M, mix), br_c, w_branch, i, zt, BF16)
        x = res_matmul([merged], w_out, x, i, name="out_proj")
        q = matmul(rmsnorm(x, norm_x[i], BF16), wx_q, i, act_dtype, name="xq_proj")
        if decode:
            xo = xattn_decode(q.reshape(B, T, xw), mem_k, mem_v, i, x_heads, act_dtype)
        else:
            xo = xattn(q.reshape(B, T, xw), mem_k, mem_v, i, x_heads, act_dtype)
        x = res_matmul([xo.reshape(M, xw)], wx_o, x, i, name="xo_proj")
        hf = rmsnorm(x, norm_ffn[i], BF16)
        acts, ffn_sts = zip(*[ffn_first_half(hf, ffn_up, i, c0, wd, ffn_prev8, ffn_conv_w[i],
                                             ffn_conv_b[i].reshape(1, dff), B, T, BF16) for c0, wd in ffn_parts])
        x = res_matmul(list(acts), ffn_down, x, i, tm=1024, tn=256, name="ffn_down")
        ffn_st = jnp.concatenate(ffn_sts, axis=-1)
        k_new = zs3[:, T - min(T, WINDOW):, sqw:sqw + skw].astype(F32)
        v_new = zs3[:, T - min(T, WINDOW):, sqw + skw:].astype(F32)
        return x, (gla_st, k_new, v_new, conv_st[:, SUBLANE - (CONV_W - 1):],
                   ffn_st[:, SUBLANE - (CONV_W - 1):])

    xp = x_prompt.reshape(Bp * Tp, D)
    xs = x_sample.reshape(Bs * Ts, D)
    act_p = BF16
    act_s = BF16 if Ts % 16 == 0 else F32
    st_p, st_s = [], []
    for i in range(depth):
        mem_k = matmul(mem2, wx_k, i, F32, name="mem_k").reshape(Bp, n_mem, xw)
        mem_v = matmul(mem2, wx_v, i, F32, name="mem_v").reshape(Bp, n_mem, xw)
        xp, sp = trunk_layer(i, xp, Bp, Tp, False, None,
                             jnp.zeros((Bp, SUBLANE, mix), F32), jnp.zeros((Bp, SUBLANE, dff), F32),
                             mem_k, mem_v, act_p)
        xs, ss = trunk_layer(i, xs, Bs, Ts, True, state_gla, first8(state_conv[i]), first8(state_ffn[i]),
                             cmem_k4, cmem_v4, act_s)
        st_p.append(dict(gla=sp[0], k=sp[1].reshape(Bp, WINDOW, swa_kv, swa_hd),
                         v=sp[2].reshape(Bp, WINDOW, swa_kv, swa_hd), conv=sp[3], ffn=sp[4],
                         mem_k=mem_k.reshape(Bp, n_mem, x_heads, x_hd),
                         mem_v=mem_v.reshape(Bp, n_mem, x_heads, x_hd)))
        kc = jnp.concatenate([cache_k4[i], ss[1]], axis=1)[:, Ts:]
        vc = jnp.concatenate([cache_v4[i], ss[2]], axis=1)[:, Ts:]
        st_s.append(dict(gla=ss[0], k=kc.reshape(Bs, win, swa_kv, swa_hd), v=vc.reshape(Bs, win, swa_kv, swa_hd),
                         conv=ss[3], ffn=ss[4]))
    y_prompt = rmsnorm(xp, norm_final, F32).reshape(Bp, Tp, D)
    y_sample = rmsnorm(xs, norm_final, F32).reshape(Bs, Ts, D)

    def stack(lst, key):
        return jnp.stack([s[key] for s in lst])

    return (y_prompt, y_sample, stack(st_p, "gla"), stack(st_s, "gla"), stack(st_p, "k"), stack(st_p, "v"),
            stack(st_s, "k"), stack(st_s, "v"), stack(st_p, "conv"), stack(st_s, "conv"),
            stack(st_p, "ffn"), stack(st_s, "ffn"), stack(st_p, "mem_k"), stack(st_p, "mem_v"))
```

```python
import functools
import math

import jax
import jax.numpy as jnp
from jax import lax
from jax.experimental import pallas as pl
from jax.experimental.pallas import tpu as pltpu

F32 = jnp.float32
BF16 = jnp.bfloat16

EPS = 1e-6
GLA_TAU = 16.0
GLA_CHUNK = 64
GLA_SUB = 16
WINDOW = 128
N_BUCKETS = 32
MAX_DIST = 128
CONV_W = 3
NEG = -1e30

LANE = 128
SUBLANE = 8
MXU_COLS = 256
MXU_ROWS = 512
VMEM_LIMIT = 56 * 1024 * 1024


def _cparams(sem):
    return pltpu.CompilerParams(dimension_semantics=sem, vmem_limit_bytes=VMEM_LIMIT)


def _tile(n, pref, unit):
    if n <= pref:
        return n
    t = (pref // unit) * unit
    while t > unit and n % t:
        t -= unit
    assert n % t == 0, (n, pref, unit)
    return t


def _mxu_dtype(rows):
    return BF16 if rows % 16 == 0 else F32


def _sigmoid(x):
    return 1.0 / (1.0 + jnp.exp(-x))


def _dot(a, b):
    return jnp.dot(a, b, preferred_element_type=F32)


def _dot_nt(a, b):
    return lax.dot_general(a, b, (((1,), (1,)), ((), ())), preferred_element_type=F32)


def _col_chunks(n):
    if n % LANE:
        return [(0, n)]
    return [(c, min(MXU_COLS, n - c)) for c in range(0, n, MXU_COLS)]


def _row_chunks(n):
    m = MXU_ROWS if n % MXU_ROWS == 0 else n
    return [(r, m) for r in range(0, n, m)]


def _pad_rows(x, rows):
    if x.shape[0] == rows:
        return x
    return jnp.concatenate([x, jnp.zeros((rows - x.shape[0], x.shape[1]), x.dtype)], axis=0)


def _norm_kernel(x_ref, g_ref, o_ref):
    x = x_ref[...]
    ms = jnp.mean(x * x, axis=-1, keepdims=True)
    o_ref[...] = (x * lax.rsqrt(ms + EPS) * g_ref[...]).astype(o_ref.dtype)


def rmsnorm(x, gain, out_dtype, tm=512):
    M, K = x.shape
    tm = _tile(M, tm, SUBLANE)
    return pl.pallas_call(
        _norm_kernel, grid=(M // tm,),
        in_specs=[pl.BlockSpec((tm, K), lambda i: (i, 0)), pl.BlockSpec((1, K), lambda i: (0, 0))],
        out_specs=pl.BlockSpec((tm, K), lambda i: (i, 0)),
        out_shape=jax.ShapeDtypeStruct((M, K), out_dtype),
        compiler_params=_cparams(("parallel",)), name="rmsnorm")(x, gain.reshape(1, K))


def _mm_kernel(a_ref, w_ref, o_ref):
    for c, w in _col_chunks(o_ref.shape[1]):
        wc = w_ref[:, c:c + w].astype(BF16)
        for r, n in _row_chunks(o_ref.shape[0]):
            o_ref[r:r + n, c:c + w] = _dot(a_ref[r:r + n, :].astype(BF16), wc).astype(o_ref.dtype)


def _mm_nt_kernel(a_ref, wt_ref, o_ref):
    for c, w in _col_chunks(o_ref.shape[1]):
        wc = wt_ref[c:c + w, :].astype(BF16)
        for r, n in _row_chunks(o_ref.shape[0]):
            o_ref[r:r + n, c:c + w] = _dot_nt(a_ref[r:r + n, :], wc).astype(o_ref.dtype)


def _mm_nt_gate_kernel(a_ref, wt_ref, wglr_ref, up_ref, gb_ref, o_ref, lg_ref):
    _mm_nt_kernel(a_ref, wt_ref, o_ref)

    @pl.when(pl.program_id(1) == 0)
    def _():
        wg = _pad_rows(wglr_ref[...], LANE).astype(BF16)
        up = _pad_rows(up_ref[...], LANE).astype(BF16)
        for r, n in _row_chunks(o_ref.shape[0]):
            glr = _dot_nt(a_ref[r:r + n, :], wg)
            z = _dot(glr.astype(BF16), up) + gb_ref[...]
            lg_ref[r:r + n, :] = -(jnp.maximum(-z, 0.0) + jnp.log1p(jnp.exp(-jnp.abs(z)))) / GLA_TAU


def _res_mm_kernel(*refs):
    r_ref, o_ref = refs[-2:]
    pairs = list(zip(refs[0:-2:2], refs[1:-2:2]))
    for c, w in _col_chunks(o_ref.shape[1]):
        wcs = [w_ref[:, c:c + w].astype(BF16) for _, w_ref in pairs]
        for r, n in _row_chunks(o_ref.shape[0]):
            acc = r_ref[r:r + n, c:c + w]
            for (a_ref, _), wc in zip(pairs, wcs):
                acc = acc + _dot(a_ref[r:r + n, :].astype(BF16), wc)
            o_ref[r:r + n, c:c + w] = acc


def matmul(a, w, layer, out_dtype, tm=2048, tn=512, name="mm"):
    M, K = a.shape
    N = w.shape[-1]
    tm = _tile(M, tm, SUBLANE)
    tn = _tile(N, tn, LANE)
    return pl.pallas_call(
        _mm_kernel, grid=(M // tm, N // tn),
        in_specs=[pl.BlockSpec((tm, K), lambda i, j: (i, 0)),
                  pl.BlockSpec((None, K, tn), lambda i, j: (layer, 0, j))],
        out_specs=pl.BlockSpec((tm, tn), lambda i, j: (i, j)),
        out_shape=jax.ShapeDtypeStruct((M, N), out_dtype),
        compiler_params=_cparams(("parallel", "arbitrary")), name=name)(a, w)


def _wt_spec(layer, row0, tn, K):
    return pl.BlockSpec((None, pl.Element(tn), pl.Element(K)),
                        lambda i, j: (layer, pl.multiple_of(row0 + j * tn, SUBLANE), 0))


def matmul_nt(a, wt, layer, row0, n, out_dtype, tm=2048, tn=512, name="mm_nt"):
    M, K = a.shape
    assert row0 % SUBLANE == 0
    tm = _tile(M, tm, SUBLANE)
    tn = _tile(n, tn, LANE)
    return pl.pallas_call(
        _mm_nt_kernel, grid=(M // tm, n // tn),
        in_specs=[pl.BlockSpec((tm, K), lambda i, j: (i, 0)), _wt_spec(layer, row0, tn, K)],
        out_specs=pl.BlockSpec((tm, tn), lambda i, j: (i, j)),
        out_shape=jax.ShapeDtypeStruct((M, n), out_dtype),
        compiler_params=_cparams(("parallel", "arbitrary")), name=name)(a, wt)


def matmul_nt_gate(a, wt, layer, n, glr_row0, gate_up, gate_b, out_dtype, tm=2048, tn=512, name="mm_nt_gate"):
    M, K = a.shape
    depth, rank, gw = gate_up.shape
    assert glr_row0 % SUBLANE == 0 and rank % SUBLANE == 0
    tm = _tile(M, tm, SUBLANE)
    tn = _tile(n, tn, LANE)
    return pl.pallas_call(
        _mm_nt_gate_kernel, grid=(M // tm, n // tn),
        in_specs=[pl.BlockSpec((tm, K), lambda i, j: (i, 0)), _wt_spec(layer, 0, tn, K),
                  pl.BlockSpec((None, pl.Element(rank), pl.Element(K)), lambda i, j: (layer, glr_row0, 0)),
                  pl.BlockSpec((None, rank, gw), lambda i, j: (layer, 0, 0)),
                  pl.BlockSpec((None, 1, gw), lambda i, j: (layer, 0, 0))],
        out_specs=[pl.BlockSpec((tm, tn), lambda i, j: (i, j)), pl.BlockSpec((tm, gw), lambda i, j: (i, 0))],
        out_shape=[jax.ShapeDtypeStruct((M, n), out_dtype), jax.ShapeDtypeStruct((M, gw), F32)],
        compiler_params=_cparams(("parallel", "arbitrary")), name=name,
    )(a, wt, wt, gate_up, gate_b.reshape(depth, 1, gw))


def _res_mm_norm_kernel(a_ref, w_ref, r_ref, g_ref, o_ref, h_ref, wb_scr):
    @pl.when(pl.program_id(0) == 0)
    def _():
        for c, w in _col_chunks(wb_scr.shape[1]):
            wb_scr[:, c:c + w] = w_ref[:, c:c + w].astype(BF16)

    n_cols = o_ref.shape[1]
    for r, n in _row_chunks(o_ref.shape[0]):
        rows = slice(r, r + n)
        a = a_ref[rows, :].astype(BF16)
        ss = jnp.zeros((n, 1), F32)
        for c, w in _col_chunks(n_cols):
            x = r_ref[rows, c:c + w] + _dot(a, wb_scr[:, c:c + w])
            o_ref[rows, c:c + w] = x
            ss = ss + jnp.sum(x * x, axis=-1, keepdims=True)
        scale = lax.rsqrt(ss / n_cols + EPS)
        for c, w in _col_chunks(n_cols):
            h_ref[rows, c:c + w] = (o_ref[rows, c:c + w] * scale * g_ref[:, c:c + w]).astype(h_ref.dtype)


def res_matmul_norm(a, w, res, gain, layer, tm=512, name="res_mm_norm"):
    M, K = a.shape
    N = w.shape[-1]
    tm = _tile(M, tm, SUBLANE)
    once = pl.Buffered(1)
    return pl.pallas_call(
        _res_mm_norm_kernel, grid=(M // tm,),
        in_specs=[pl.BlockSpec((tm, K), lambda i: (i, 0)),
                  pl.BlockSpec((None, K, N), lambda i: (layer, 0, 0), pipeline_mode=once),
                  pl.BlockSpec((tm, N), lambda i: (i, 0)),
                  pl.BlockSpec((1, N), lambda i: (0, 0))],
        out_specs=[pl.BlockSpec((tm, N), lambda i: (i, 0)), pl.BlockSpec((tm, N), lambda i: (i, 0))],
        out_shape=[jax.ShapeDtypeStruct((M, N), F32), jax.ShapeDtypeStruct((M, N), BF16)],
        scratch_shapes=[pltpu.VMEM((K, N), BF16)],
        compiler_params=_cparams(("arbitrary",)), name=name)(a, w, res, gain.reshape(1, N))


def res_matmul(a_parts, w, res, layer, tm=2048, tn=512, name="res_mm"):
    M = a_parts[0].shape[0]
    N = w.shape[-1]
    tm = _tile(M, tm, SUBLANE)
    tn = _tile(N, tn, LANE)
    in_specs, args, row0 = [], [], 0
    for a in a_parts:
        kp = a.shape[1]
        assert row0 % SUBLANE == 0
        in_specs += [pl.BlockSpec((tm, kp), lambda i, j: (i, 0), pipeline_mode=pl.Buffered(1)),
                     pl.BlockSpec((None, pl.Element(kp), pl.Element(tn)),
                                  lambda i, j, r0=row0: (layer, r0, pl.multiple_of(j * tn, LANE)))]
        args += [a, w]
        row0 += kp
    assert row0 == w.shape[1]
    return pl.pallas_call(
        _res_mm_kernel, grid=(M // tm, N // tn),
        in_specs=in_specs + [pl.BlockSpec((tm, tn), lambda i, j: (i, j))],
        out_specs=pl.BlockSpec((tm, tn), lambda i, j: (i, j)),
        out_shape=jax.ShapeDtypeStruct((M, N), F32),
        compiler_params=_cparams(("parallel", "arbitrary")), name=name)(*args, res)


def _gla_chunk(q, k, v, lg, st, md, lp, consts):
    row, rsub_l, lsub, ones = consts
    c = min(GLA_SUB, lp)
    b = lg
    sh = 1
    while sh < lp:
        b = b + jnp.where(row >= sh, pltpu.roll(b, sh, 0), 0.0)
        sh *= 2
    o = _dot_nt((q * jnp.exp(b)).astype(md), st.astype(md))
    a_rows = []
    for i in range(lp // c):
        lo = i * c
        qb, kb, bb = q[lo:lo + c], k[lo:lo + c], b[lo:lo + c]
        ws = [qb * kb[s:s + 1, :] * jnp.exp(bb - bb[s:s + 1, :]) for s in range(c)]
        rsum = _dot(jnp.concatenate(ws, axis=0).astype(md), ones)
        a_i = jnp.zeros((c, LANE), F32)
        for s in range(c):
            a_i = jnp.where(lsub == lo + s, rsum[s * c:(s + 1) * c], a_i)
        a_i = jnp.where(lsub <= lo + rsub_l, a_i, 0.0)
        if i > 0:
            ref = b[lo - 1:lo, :]
            qt = qb * jnp.exp(bb - ref)
            kt = k * jnp.exp(jnp.where(row < lo, ref - b, NEG))
            a_i = a_i + _dot_nt(qt.astype(md), _pad_rows(kt, LANE).astype(md))
        a_rows.append(a_i)
    a = jnp.concatenate(a_rows, axis=0)
    vk = _pad_rows(v, LANE)
    o = o + _dot(a.astype(md), vk.astype(md))
    bend = b[lp - 1:lp, :]
    kd = _pad_rows(k * jnp.exp(bend - b), LANE)
    st = st * jnp.exp(bend) + _dot(vk.T.astype(md), kd.astype(md))
    return o, st


def _gla_kernel(*refs, lin, lp, n_chunks, heads, scale, has_s0):
    if has_s0:
        q_ref, k_ref, v_ref, r_ref, lg_ref, gn_ref, s0_ref, o_ref, st_ref, s_scr = refs
    else:
        q_ref, k_ref, v_ref, r_ref, lg_ref, gn_ref, o_ref, st_ref, s_scr = refs
    t = pl.program_id(1)
    bb = q_ref.shape[0]
    dk = q_ref.shape[-1] // heads
    dv = v_ref.shape[-1] // heads
    c = min(GLA_SUB, lp)
    md = _mxu_dtype(lp)
    pairs = [(s, h) for s in range(bb) for h in range(heads)]

    @pl.when(t == 0)
    def _():
        for s, h in pairs:
            s_scr[s, h] = s0_ref[s, h].T if has_s0 else jnp.zeros((dv, dk), F32)

    row = lax.broadcasted_iota(jnp.int32, (lp, dk), 0)
    consts = (row, lax.broadcasted_iota(jnp.int32, (c, LANE), 0),
              lax.broadcasted_iota(jnp.int32, (c, LANE), 1), jnp.ones((dk, LANE), md))

    def chunk(ci, carry):
        rows = pl.ds(pl.multiple_of(ci * lin, lin), lin)
        for s, h in pairs:
            kcols, vcols = slice(h * dk, (h + 1) * dk), slice(h * dv, (h + 1) * dv)
            q = _pad_rows(q_ref[s, rows, kcols].astype(F32), lp) * scale
            k = _pad_rows(k_ref[s, rows, kcols].astype(F32), lp)
            v = _pad_rows(v_ref[s, rows, vcols].astype(F32), lp)
            lg = _pad_rows(lg_ref[s, rows, kcols], lp)
            o, st = _gla_chunk(q, k, v, lg, s_scr[s, h], md, lp, consts)
            s_scr[s, h] = st
            o = o[:lin]
            ms = jnp.mean(o * o, axis=-1, keepdims=True)
            y = o * lax.rsqrt(ms + EPS) * gn_ref[...]
            r = r_ref[s, rows, vcols].astype(F32)
            o_ref[s, rows, vcols] = (y * (r * _sigmoid(r))).astype(o_ref.dtype)
        return carry

    lax.fori_loop(0, n_chunks, chunk, 0)

    @pl.when(t == pl.num_programs(1) - 1)
    def _():
        for s, h in pairs:
            st_ref[s, h] = s_scr[s, h].T


def gla(zg, lg, gnorm, s0, layer, heads, dk, dv, out_dtype, seqs_per_step=1):
    B, T, _ = zg.shape
    depth = gnorm.shape[0]
    qw, vw = heads * dk, heads * dv
    assert vw % qw == 0
    lin = min(GLA_CHUNK, T)
    assert T % lin == 0
    lp = max(lin, 16)
    tt = _tile(T, 512, lin)
    bb = _tile(B, seqs_per_step, 1)
    in_specs = [pl.BlockSpec((bb, tt, qw), lambda b, t: (b, t, 0)),
                pl.BlockSpec((bb, tt, qw), lambda b, t: (b, t, 1)),
                pl.BlockSpec((bb, tt, vw), lambda b, t: (b, t, 2 * qw // vw)),
                pl.BlockSpec((bb, tt, vw), lambda b, t: (b, t, 2 * qw // vw + 1)),
                pl.BlockSpec((bb, tt, qw), lambda b, t: (b, t, 0)),
                pl.BlockSpec((None, 1, dv), lambda b, t: (layer, 0, 0))]
    args = [zg, zg, zg, zg, lg, gnorm.reshape(depth, 1, dv)]
    if s0 is not None:
        in_specs.append(pl.BlockSpec((None, bb, heads, dk, dv), lambda b, t: (layer, b, 0, 0, 0)))
        args.append(s0)
    kern = functools.partial(_gla_kernel, lin=lin, lp=lp, n_chunks=tt // lin, heads=heads, scale=dk ** -0.5,
                             has_s0=s0 is not None)
    return pl.pallas_call(
        kern, grid=(B // bb, T // tt), in_specs=in_specs,
        out_specs=[pl.BlockSpec((bb, tt, vw), lambda b, t: (b, t, 0)),
                   pl.BlockSpec((bb, heads, dk, dv), lambda b, t: (b, 0, 0, 0))],
        out_shape=[jax.ShapeDtypeStruct((B, T, vw), out_dtype),
                   jax.ShapeDtypeStruct((B, heads, dk, dv), F32)],
        scratch_shapes=[pltpu.VMEM((bb, heads, dv, dk), F32)],
        compiler_params=_cparams(("parallel", "arbitrary")), name="gla",
    )(*args)


def _swa_kernel(sink_ref, q_ref, kp_ref, kc_ref, vp_ref, vc_ref, bias_ref, o_ref, *,
                layer, heads, group, hd, scale):
    tq = q_ref.shape[0]
    md = _mxu_dtype(tq)
    k = jnp.concatenate([kp_ref[...].astype(md), kc_ref[...].astype(md)], axis=0)
    v = jnp.concatenate([vp_ref[...].astype(md), vc_ref[...].astype(md)], axis=0)
    outs = []
    for h in range(heads):
        kv = h // group
        qh = (q_ref[:, h * hd:(h + 1) * hd].astype(F32) * scale).astype(md)
        s = _dot_nt(qh, k[:, kv * hd:(kv + 1) * hd]) + bias_ref[h]
        sink = sink_ref[layer, h]
        m = jnp.maximum(jnp.max(s, axis=-1, keepdims=True), sink)
        e = jnp.exp(s - m)
        den = jnp.sum(e, axis=-1, keepdims=True) + jnp.exp(sink - m)
        outs.append(_dot(e.astype(md), v[:, kv * hd:(kv + 1) * hd]) / den)
    per = LANE // hd
    for j in range(heads // per):
        o_ref[:, j * LANE:(j + 1) * LANE] = jnp.concatenate(
            outs[j * per:(j + 1) * per], axis=-1).astype(o_ref.dtype)


def _t5_bucket(dist):
    n = jnp.maximum(dist, 0)
    max_exact = N_BUCKETS // 2
    nf = jnp.maximum(n, 1).astype(F32)
    large = max_exact + (jnp.log(nf / max_exact) / math.log(MAX_DIST / max_exact)
                         * (N_BUCKETS - max_exact)).astype(jnp.int32)
    large = jnp.minimum(large, N_BUCKETS - 1)
    return jnp.where(n < max_exact, n, large)


def _swa_bias_table(rel_bias, tq, p, with_first):
    i = jnp.arange(tq)[:, None]
    j = jnp.arange(p + tq)[None, :]
    dist = p + i - j
    valid = (dist >= 0) & (dist <= WINDOW)
    onehot = (_t5_bucket(dist)[:, :, None] == jnp.arange(N_BUCKETS)).astype(F32)
    bias = jnp.einsum("ijb,bh->hij", onehot, rel_bias.astype(F32), precision=lax.Precision.HIGHEST)
    regular = jnp.where(valid[None], bias, NEG)
    if not with_first:
        return regular[None]
    return jnp.stack([jnp.where((valid & (j >= p))[None], bias, NEG), regular])


def swa(z3, prev_k, prev_v, sinks, bias, layer, kv_heads, hd, tq, out_dtype):
    B, T, _ = z3.shape
    heads = sinks.shape[1]
    qw, kw = heads * hd, kv_heads * hd
    assert qw % kw == 0
    ko, vo = qw // kw, qw // kw + 1
    nb = T // tq
    prompt = prev_k is None
    p = bias.shape[3] - tq
    if prompt:
        assert p == tq and bias.shape[0] == 2
        prev_specs = [pl.BlockSpec((None, p, kw), lambda b, n: (b, jnp.maximum(n - 1, 0), ko)),
                      pl.BlockSpec((None, p, kw), lambda b, n: (b, jnp.maximum(n - 1, 0), vo))]
        prev_k = prev_v = z3
        bias_spec = pl.BlockSpec((None,) + bias.shape[1:], lambda b, n: (jnp.minimum(n, 1), 0, 0, 0))
    else:
        assert nb == 1 and bias.shape[0] == 1
        prev_specs = [pl.BlockSpec((None, None, p, kw), lambda b, n: (layer, b, 0, 0))] * 2
        bias_spec = pl.BlockSpec((None,) + bias.shape[1:], lambda b, n: (0, 0, 0, 0))
    kern = functools.partial(_swa_kernel, layer=layer, heads=heads, group=heads // kv_heads, hd=hd,
                             scale=hd ** -0.5)
    return pl.pallas_call(
        kern, grid=(B, nb),
        in_specs=[pl.BlockSpec(memory_space=pltpu.SMEM),
                  pl.BlockSpec((None, tq, qw), lambda b, n: (b, n, 0)),
                  prev_specs[0],
                  pl.BlockSpec((None, tq, kw), lambda b, n: (b, n, ko)),
                  prev_specs[1],
                  pl.BlockSpec((None, tq, kw), lambda b, n: (b, n, vo)),
                  bias_spec],
        out_specs=pl.BlockSpec((None, tq, qw), lambda b, n: (b, n, 0)),
        out_shape=jax.ShapeDtypeStruct((B, T, qw), out_dtype),
        compiler_params=_cparams(("parallel", "arbitrary")), name="swa",
    )(sinks, z3, prev_k, z3, prev_v, z3, bias)


def _swa_decode_kernel(q_ref, kc_ref, kn_ref, vc_ref, vn_ref, tab_ref, o_ref, *, kv_heads, group, hd, scale):
    bb, T, _ = q_ref.shape
    kw = kc_ref.shape[-1]
    npad = tab_ref.shape[-1] - kc_ref.shape[1] - T
    md = _mxu_dtype(group * T)
    per = LANE // hd
    for s in range(bb):
        zpad = jnp.zeros((npad, kw), F32)
        k = jnp.concatenate([kc_ref[s].astype(F32), kn_ref[s].astype(F32), zpad], axis=0).astype(md)
        v = jnp.concatenate([vc_ref[s].astype(F32), vn_ref[s].astype(F32), zpad], axis=0).astype(md)
        outs = []
        for g in range(kv_heads):
            qs = jnp.concatenate([q_ref[s, :, (g * group + j) * hd:(g * group + j + 1) * hd].astype(F32)
                                  for j in range(group)], axis=0) * scale
            sc = _dot_nt(qs.astype(md), k[:, g * hd:(g + 1) * hd]) + tab_ref[g]
            e = jnp.exp(sc - jnp.max(sc, axis=-1, keepdims=True))
            den = jnp.sum(e, axis=-1, keepdims=True)
            o = _dot(e.astype(md), v[:, g * hd:(g + 1) * hd]) / den
            outs += [o[j * T:(j + 1) * T] for j in range(group)]
        for j in range(len(outs) // per):
            o_ref[s, :, j * LANE:(j + 1) * LANE] = jnp.concatenate(
                outs[j * per:(j + 1) * per], axis=-1).astype(o_ref.dtype)


def _swa_decode_table(rel_bias, sinks, T, p, kv_heads):
    depth, heads = sinks.shape
    group = heads // kv_heads
    base = _swa_bias_table(rel_bias, T, p, False)[0].reshape(kv_heads, group * T, p + T)
    npad = _round_up(p + T + 1, 16) - (p + T)
    sink_col = jnp.repeat(sinks.astype(F32).reshape(depth, kv_heads, group), T, axis=2)[..., None]
    parts = [jnp.broadcast_to(base[None], (depth,) + base.shape), sink_col,
             jnp.full((depth, kv_heads, group * T, npad - 1), NEG, F32)]
    return jnp.concatenate(parts, axis=-1)


def swa_decode(z3, cache_k, cache_v, table, layer, kv_heads, hd, out_dtype, seqs_per_step=8):
    B, T, zw = z3.shape
    kw = kv_heads * hd
    qw = zw - 2 * kw
    p = cache_k.shape[2]
    bb = _tile(B, seqs_per_step, 1)
    kern = functools.partial(_swa_decode_kernel, kv_heads=kv_heads, group=qw // kw, hd=hd, scale=hd ** -0.5)
    c_spec = pl.BlockSpec((None, bb, p, kw), lambda b: (layer, b, 0, 0))
    return pl.pallas_call(
        kern, grid=(B // bb,),
        in_specs=[pl.BlockSpec((bb, T, qw), lambda b: (b, 0, 0)),
                  c_spec, pl.BlockSpec((bb, T, kw), lambda b: (b, 0, qw // kw)),
                  c_spec, pl.BlockSpec((bb, T, kw), lambda b: (b, 0, qw // kw + 1)),
                  pl.BlockSpec((None,) + table.shape[1:], lambda b: (layer, 0, 0, 0))],
        out_specs=pl.BlockSpec((bb, T, qw), lambda b: (b, 0, 0)),
        out_shape=jax.ShapeDtypeStruct((B, T, qw), out_dtype),
        compiler_params=_cparams(("parallel",)), name="swa_decode",
    )(z3, cache_k, z3, cache_v, z3, table)


class _ConvTaps:
    def __init__(self, i, j, cols, tm, prev_ref, st_ref, carry_scr, seq_len):
        self.j, self.cols, self.seq_len = j, cols, seq_len
        self.prev_ref, self.st_ref, self.carry_scr = prev_ref, st_ref, carry_scr
        self.single = tm <= seq_len
        if self.single:
            @pl.when((i % (seq_len // tm)) == 0)
            def _():
                carry_scr[j, :, cols] = prev_ref[0, :, cols]

            self.before = carry_scr[j, :, cols]

    def chunk(self, x, r):
        n, w = x.shape
        if self.single:
            before = self.before
            row8 = lax.broadcasted_iota(jnp.int32, (SUBLANE, w), 0)

            def back(s):
                xr = pltpu.roll(x, s, 0)
                top = jnp.where(row8 < s, pltpu.roll(before, s, 0), xr[0:SUBLANE])
                return jnp.concatenate([top, xr[SUBLANE:]], axis=0)

            self.before = x[n - SUBLANE:n]
            return back(1), back(2)
        t = self.seq_len
        assert r % t == 0 and n % t == 0
        seqs = slice(r // t, (r + n) // t)
        tix = lax.rem(lax.broadcasted_iota(jnp.int32, (n, w), 0), t)
        s1 = jnp.broadcast_to(self.prev_ref[seqs, SUBLANE - 1:SUBLANE, self.cols], (n // t, t, w)).reshape(n, w)
        s0 = jnp.broadcast_to(self.prev_ref[seqs, SUBLANE - 2:SUBLANE - 1, self.cols], (n // t, t, w)).reshape(n, w)
        self.st_ref[seqs, :, self.cols] = x.reshape(n // t, t, w)[:, t - SUBLANE:, :]
        x1 = jnp.where(tix == 0, s1, pltpu.roll(x, 1, 0))
        x2 = jnp.where(tix == 0, s0, jnp.where(tix == 1, s1, pltpu.roll(x, 2, 0)))
        return x1, x2

    def finish(self):
        if self.single:
            self.carry_scr[self.j, :, self.cols] = self.before
            self.st_ref[0, :, self.cols] = self.before


def _row_blocking(B, T, tm):
    if T >= tm:
        tm = _tile(T, tm, SUBLANE)
        return tm, 1, (lambda i: i // (T // tm))
    assert T % SUBLANE == 0
    nseq = _tile(B, max(tm // T, 1), 1)
    return nseq * T, nseq, (lambda i: i)


def _conv_branch_kernel(cb_ref, cc_ref, ch_ref, prev_ref, w_ref, o_ref, st_ref, carry_scr, *, seq_len):
    i, j = pl.program_id(0), pl.program_id(1)
    taps = _ConvTaps(i, j, slice(None), o_ref.shape[0], prev_ref, st_ref, carry_scr, seq_len)
    for r, n in _row_chunks(o_ref.shape[0]):
        rows = slice(r, r + n)
        u = cc_ref[rows, :].astype(F32) * ch_ref[rows, :].astype(F32)
        u1, u2 = taps.chunk(u, r)
        y = u2 * w_ref[0:1, :] + u1 * w_ref[1:2, :] + u * w_ref[2:3, :]
        o_ref[rows, :] = (cb_ref[rows, :].astype(F32) * y).astype(o_ref.dtype)
    taps.finish()


def conv_branch(z, prev8, conv_w, layer, B, T, out_dtype, tm=1024, tc=512):
    M = B * T
    width = conv_w.shape[2]
    tc = _tile(width, tc, LANE)
    nc = width // tc
    tm, nseq, seq_of = _row_blocking(B, T, tm)
    n_st = (M // tm) * nseq
    z_specs = [pl.BlockSpec((tm, tc), (lambda i, j, o=n * nc: (i, o + j))) for n in range(3)]
    out, st = pl.pallas_call(
        functools.partial(_conv_branch_kernel, seq_len=T), grid=(M // tm, nc),
        in_specs=z_specs + [pl.BlockSpec((nseq, SUBLANE, tc), lambda i, j: (seq_of(i), 0, j)),
                            pl.BlockSpec((None, CONV_W, tc), lambda i, j: (layer, 0, j))],
        out_specs=[pl.BlockSpec((tm, tc), lambda i, j: (i, j)),
                   pl.BlockSpec((nseq, SUBLANE, tc), lambda i, j: (i, 0, j))],
        out_shape=[jax.ShapeDtypeStruct((M, width), out_dtype),
                   jax.ShapeDtypeStruct((n_st, SUBLANE, width), F32)],
        scratch_shapes=[pltpu.VMEM((nc, SUBLANE, tc), F32)],
        compiler_params=_cparams(("arbitrary", "arbitrary")), name="conv_branch",
    )(z, z, z, prev8, conv_w)
    return out, st.reshape(B, n_st // B, SUBLANE, width)[:, -1]


def _ffn_up_kernel(h_ref, wu_ref, wg_ref, prev_ref, cw_ref, cb_ref, act_ref, st_ref, carry_scr, *, seq_len):
    i, j = pl.program_id(0), pl.program_id(1)
    tm, tn = act_ref.shape
    for c, w in _col_chunks(tn):
        cols = slice(c, c + w)
        wu = wu_ref[:, cols].astype(BF16)
        wg = wg_ref[:, cols].astype(BF16)
        taps = _ConvTaps(i, j, cols, tm, prev_ref, st_ref, carry_scr, seq_len)
        for r, n in _row_chunks(tm):
            h = h_ref[r:r + n, :]
            g = _dot(h, wg)
            g1, g2 = taps.chunk(g, r)
            gc = g2 * cw_ref[0:1, cols] + g1 * cw_ref[1:2, cols] + g * cw_ref[2:3, cols] + cb_ref[:, cols]
            act_ref[r:r + n, cols] = (gc * _sigmoid(gc) * _dot(h, wu)).astype(act_ref.dtype)
        taps.finish()


def ffn_first_half(h, w_up, layer, col0, width, prev8, conv_w, conv_b, B, T, out_dtype, tm=2048, tn=512):
    M, K = h.shape
    dff = w_up.shape[2] // 2
    tn = min(tn, width)
    assert col0 % LANE == 0 and dff % LANE == 0 and width % tn == 0
    nj = width // tn
    tm, nseq, seq_of = _row_blocking(B, T, tm)
    n_st = (M // tm) * nseq

    def col(j, base=col0):
        return pl.multiple_of(base + j * tn, LANE)

    def win(rows):
        return pl.BlockSpec((pl.Element(rows), pl.Element(tn)), lambda i, j: (0, col(j)))

    act, st = pl.pallas_call(
        functools.partial(_ffn_up_kernel, seq_len=T), grid=(M // tm, nj),
        in_specs=[pl.BlockSpec((tm, K), lambda i, j: (i, 0)),
                  pl.BlockSpec((None, pl.Element(K), pl.Element(tn)), lambda i, j: (layer, 0, col(j))),
                  pl.BlockSpec((None, pl.Element(K), pl.Element(tn)), lambda i, j: (layer, 0, col(j, dff + col0))),
                  pl.BlockSpec((pl.Element(nseq), pl.Element(SUBLANE), pl.Element(tn)),
                               lambda i, j: (seq_of(i) * nseq, 0, col(j))),
                  win(CONV_W), win(1)],
        out_specs=[pl.BlockSpec((tm, tn), lambda i, j: (i, j)),
                   pl.BlockSpec((nseq, SUBLANE, tn), lambda i, j: (i, 0, j))],
        out_shape=[jax.ShapeDtypeStruct((M, width), out_dtype),
                   jax.ShapeDtypeStruct((n_st, SUBLANE, width), F32)],
        scratch_shapes=[pltpu.VMEM((nj, SUBLANE, tn), F32)],
        compiler_params=_cparams(("arbitrary", "arbitrary")), name="ffn_up",
    )(h, w_up, w_up, prev8, conv_w, conv_b)
    return act, st.reshape(B, n_st // B, SUBLANE, width)[:, -1]


def _merge_kernel(a_ref, b_ref, c_ref, wb_ref, g0_ref, g1_ref, g2_ref, o_ref):
    xs = (a_ref, b_ref, c_ref)
    gs = (g0_ref, g1_ref, g2_ref)
    for c, w in _col_chunks(o_ref.shape[1]):
        cols = slice(c, c + w)
        ws = [wb_ref[n, :, cols].astype(BF16) for n in range(3)]
        for r, m in _row_chunks(o_ref.shape[0]):
            rows = slice(r, r + m)

            def term(n):
                return _sigmoid(gs[n][rows, cols].astype(F32)) * _dot(xs[n][rows, :].astype(BF16), ws[n])

            o_ref[rows, cols] = ((term(0) + term(1)) + term(2)).astype(o_ref.dtype)


def merge(br_a, br_b, br_c, wb, layer, zg, out_dtype, tm=2048, tn=512):
    M, mix = br_a.shape
    d = wb.shape[3]
    tm = _tile(M, tm, SUBLANE)
    tn = _tile(d, tn, LANE)
    br_spec = pl.BlockSpec((tm, mix), lambda i, j: (i, 0), pipeline_mode=pl.Buffered(1))
    g_specs = [pl.BlockSpec((tm, tn), (lambda i, j, o=n * d // tn: (i, o + j))) for n in range(3)]
    return pl.pallas_call(
        _merge_kernel, grid=(M // tm, d // tn),
        in_specs=[br_spec, br_spec, br_spec,
                  pl.BlockSpec((None, 3, mix, tn), lambda i, j: (layer, 0, 0, j))] + g_specs,
        out_specs=pl.BlockSpec((tm, tn), lambda i, j: (i, j)),
        out_shape=jax.ShapeDtypeStruct((M, d), out_dtype),
        compiler_params=_cparams(("parallel", "arbitrary")), name="merge",
    )(br_a, br_b, br_c, wb, zg, zg, zg)


def _xattn_kernel(q_ref, k_ref, v_ref, o_ref, *, heads, hd, scale):
    md = _mxu_dtype(q_ref.shape[0])
    for h in range(heads):
        sl = slice(h * hd, (h + 1) * hd)
        s = _dot_nt(q_ref[:, sl].astype(md), k_ref[:, sl].astype(md)) * scale
        e = jnp.exp(s - jnp.max(s, axis=-1, keepdims=True))
        den = jnp.sum(e, axis=-1, keepdims=True)
        o_ref[:, sl] = (_dot(e.astype(md), v_ref[:, sl].astype(md)) / den).astype(o_ref.dtype)


def xattn(q3, mem_k, mem_v, layer, heads, out_dtype, tq=512):
    B, T, w = q3.shape
    nm = mem_k.shape[-2]
    hd = w // heads
    tq = _tile(T, tq, SUBLANE)
    kern = functools.partial(_xattn_kernel, heads=heads, hd=hd, scale=hd ** -0.5)
    if mem_k.ndim == 4:
        m_spec = pl.BlockSpec((None, None, nm, w), lambda b, t: (layer, b, 0, 0))
    else:
        m_spec = pl.BlockSpec((None, nm, w), lambda b, t: (b, 0, 0))
    return pl.pallas_call(
        kern, grid=(B, T // tq),
        in_specs=[pl.BlockSpec((None, tq, w), lambda b, t: (b, t, 0)), m_spec, m_spec],
        out_specs=pl.BlockSpec((None, tq, w), lambda b, t: (b, t, 0)),
        out_shape=jax.ShapeDtypeStruct((B, T, w), out_dtype),
        compiler_params=_cparams(("parallel", "arbitrary")), name="xattn",
    )(q3, mem_k, mem_v)


def _xattn_decode_kernel(q_ref, k_ref, v_ref, o_ref, *, heads, hd, scale):
    bb, T, _ = q_ref.shape
    nm = k_ref.shape[1] // heads
    rows = max(T, 16)
    for s in range(bb):
        for h in range(heads):
            sl = slice(h * hd, (h + 1) * hd)
            q = _pad_rows(q_ref[s, :, sl].astype(F32), rows).astype(BF16)
            k = k_ref[s, pl.ds(h, nm, stride=heads), :].astype(BF16)
            v = v_ref[s, pl.ds(h, nm, stride=heads), :].astype(BF16)
            sc = _dot_nt(q, k) * scale
            e = jnp.exp(sc - jnp.max(sc, axis=-1, keepdims=True))
            den = jnp.sum(e, axis=-1, keepdims=True)
            o_ref[s, :, sl] = (_dot(e.astype(BF16), v) / den)[:T].astype(o_ref.dtype)


def xattn_decode(q3, mem_k, mem_v, layer, heads, out_dtype, seqs_per_step=8):
    B, T, w = q3.shape
    hd = w // heads
    bb = _tile(B, seqs_per_step, 1)
    kern = functools.partial(_xattn_decode_kernel, heads=heads, hd=hd, scale=hd ** -0.5)
    m_spec = pl.BlockSpec((None, bb) + mem_k.shape[2:], lambda b: (layer, b, 0, 0))
    return pl.pallas_call(
        kern, grid=(B // bb,),
        in_specs=[pl.BlockSpec((bb, T, w), lambda b: (b, 0, 0)), m_spec, m_spec],
        out_specs=pl.BlockSpec((bb, T, w), lambda b: (b, 0, 0)),
        out_shape=jax.ShapeDtypeStruct((B, T, w), out_dtype),
        compiler_params=_cparams(("parallel",)), name="xattn_decode",
    )(q3, mem_k, mem_v)


def _round_up(n, m):
    return -(-n // m) * m


def kernel(x_prompt, x_sample, state_gla, cache_swa_k, cache_swa_v, state_conv, state_ffn, cache_mem_k, cache_mem_v, mem_prompt, norm_mix, w_in, gla_gate_up, gla_gate_b, gla_norm, swa_sinks, rel_bias, conv_w, w_branch, w_out, norm_x, wx_q, wx_k, wx_v, wx_o, norm_ffn, ffn_up, ffn_conv_w, ffn_conv_b, ffn_down, norm_final):
    Bp, Tp, D = x_prompt.shape
    Bs, Ts, _ = x_sample.shape
    depth = w_in.shape[0]
    gla_heads, gla_dk, gla_dv = state_gla.shape[2:]
    swa_kv, swa_hd = cache_swa_k.shape[3:]
    swa_heads = swa_sinks.shape[1]
    x_heads, x_hd = cache_mem_k.shape[3:]
    n_mem = mem_prompt.shape[1]
    mix = conv_w.shape[2]
    dff = ffn_down.shape[1]
    rank = gla_gate_up.shape[1]
    gqk = gla_heads * gla_dk
    sqw, skw = swa_heads * swa_hd, swa_kv * swa_hd
    win = cache_swa_k.shape[2]
    xw = x_heads * x_hd
    assert win == WINDOW and mix == gla_heads * gla_dv == sqw

    w_in_t = jnp.swapaxes(w_in, 1, 2)
    a_w = 2 * gqk + 2 * mix
    s_w = sqw + 2 * skw
    o_glr, o_swa = a_w, a_w + rank
    o_conv = o_swa + s_w
    o_gates = o_conv + 3 * mix
    ffn_main = dff // 512 * 512
    ffn_parts = [(0, ffn_main)] + ([(ffn_main, dff - ffn_main)] if dff > ffn_main else [])

    bias_p = _swa_bias_table(rel_bias, min(Tp, WINDOW), WINDOW, True)
    table_s = _swa_decode_table(rel_bias, swa_sinks, Ts, win, swa_kv)
    cache_k4 = cache_swa_k.reshape(depth, Bs, win, skw)
    cache_v4 = cache_swa_v.reshape(depth, Bs, win, skw)
    cmem_k4 = cache_mem_k.reshape(depth, Bs, n_mem * x_heads, x_hd)
    cmem_v4 = cache_mem_v.reshape(depth, Bs, n_mem * x_heads, x_hd)
    mem2 = mem_prompt.reshape(Bp * n_mem, D)

    def first8(st):
        return jnp.pad(st, ((0, 0), (SUBLANE - (CONV_W - 1), 0), (0, 0)))

    def trunk_layer(i, x, B, T, decode, s0, conv_prev8, ffn_prev8, mem_k, mem_v, act_dtype):
        M = B * T
        h = rmsnorm(x, norm_mix[i], BF16)
        zg, lg = matmul_nt_gate(h, w_in_t, i, a_w, o_glr, gla_gate_up, gla_gate_b, act_dtype, name="in_proj_gla")
        zs = matmul_nt(h, w_in_t, i, o_swa, s_w, act_dtype, tn=s_w // 2, name="in_proj_swa")
        zc = matmul_nt(h, w_in_t, i, o_conv, 3 * mix, act_dtype, name="in_proj_conv")
        zt = matmul_nt(h, w_in_t, i, o_gates, 3 * D, act_dtype, name="in_proj_gates")
        zs3 = zs.reshape(B, T, s_w)
        br_a, gla_st = gla(zg.reshape(B, T, a_w), lg.reshape(B, T, gqk), gla_norm, s0, i,
                           gla_heads, gla_dk, gla_dv, act_dtype, seqs_per_step=4 if decode else 1)
        if decode:
            br_b = swa_decode(zs3, cache_k4, cache_v4, table_s, i, swa_kv, swa_hd, act_dtype)
        else:
            br_b = swa(zs3, None, None, swa_sinks, bias_p, i, swa_kv, swa_hd, min(T, WINDOW), act_dtype)
        br_c, conv_st = conv_branch(zc, conv_prev8, conv_w, i, B, T, act_dtype)
        merged = merge(br_a.reshape(M, mix), br_b.reshape(M, mix), br_c, w_branch, i, zt, BF16)
        x, hx = res_matmul_norm(merged, w_out, x, norm_x[i], i, name="out_proj")
        q = matmul(hx, wx_q, i, act_dtype, name="xq_proj")
        if decode:
            xo = xattn_decode(q.reshape(B, T, xw), mem_k, mem_v, i, x_heads, act_dtype)
        else:
            xo = xattn(q.reshape(B, T, xw), mem_k, mem_v, i, x_heads, act_dtype)
        x, hf = res_matmul_norm(xo.reshape(M, xw), wx_o, x, norm_ffn[i], i, name="xo_proj")
        acts, ffn_sts = zip(*[ffn_first_half(hf, ffn_up, i, c0, wd, ffn_prev8, ffn_conv_w[i],
                                             ffn_conv_b[i].reshape(1, dff), B, T, BF16) for c0, wd in ffn_parts])
        x = res_matmul(list(acts), ffn_down, x, i, tm=2048, tn=256, name="ffn_down")
        ffn_st = jnp.concatenate(ffn_sts, axis=-1)
        k_new = zs3[:, T - min(T, WINDOW):, sqw:sqw + skw].astype(F32)
        v_new = zs3[:, T - min(T, WINDOW):, sqw + skw:].astype(F32)
        return x, (gla_st, k_new, v_new, conv_st[:, SUBLANE - (CONV_W - 1):],
                   ffn_st[:, SUBLANE - (CONV_W - 1):])

    xp = x_prompt.reshape(Bp * Tp, D)
    xs = x_sample.reshape(Bs * Ts, D)
    act_p = BF16
    act_s = BF16 if Ts % 16 == 0 else F32
    st_p, st_s = [], []
    for i in range(depth):
        mem_k = matmul(mem2, wx_k, i, F32, name="mem_k").reshape(Bp, n_mem, xw)
        mem_v = matmul(mem2, wx_v, i, F32, name="mem_v").reshape(Bp, n_mem, xw)
        xp, sp = trunk_layer(i, xp, Bp, Tp, False, None,
                             jnp.zeros((Bp, SUBLANE, mix), F32), jnp.zeros((Bp, SUBLANE, dff), F32),
                             mem_k, mem_v, act_p)
        xs, ss = trunk_layer(i, xs, Bs, Ts, True, state_gla, first8(state_conv[i]), first8(state_ffn[i]),
                             cmem_k4, cmem_v4, act_s)
        st_p.append(dict(gla=sp[0], k=sp[1].reshape(Bp, WINDOW, swa_kv, swa_hd),
                         v=sp[2].reshape(Bp, WINDOW, swa_kv, swa_hd), conv=sp[3], ffn=sp[4],
                         mem_k=mem_k.reshape(Bp, n_mem, x_heads, x_hd),
                         mem_v=mem_v.reshape(Bp, n_mem, x_heads, x_hd)))
        kc = jnp.concatenate([cache_k4[i], ss[1]], axis=1)[:, Ts:]
        vc = jnp.concatenate([cache_v4[i], ss[2]], axis=1)[:, Ts:]
        st_s.append(dict(gla=ss[0], k=kc.reshape(Bs, win, swa_kv, swa_hd), v=vc.reshape(Bs, win, swa_kv, swa_hd),
                         conv=ss[3], ffn=ss[4]))
    y_prompt = rmsnorm(xp, norm_final, F32).reshape(Bp, Tp, D)
    y_sample = rmsnorm(xs, norm_final, F32).reshape(Bs, Ts, D)

    def stack(lst, key):
        return jnp.stack([s[key] for s in lst])

    return (y_prompt, y_sample, stack(st_p, "gla"), stack(st_s, "gla"), stack(st_p, "k"), stack(st_p, "v"),
            stack(st_s, "k"), stack(st_s, "v"), stack(st_p, "conv"), stack(st_s, "conv"),
            stack(st_p, "ffn"), stack(st_s, "ffn"), stack(st_p, "mem_k"), stack(st_p, "mem_v"))
```

```python
import functools
import math

import jax
import jax.numpy as jnp
from jax import lax
from jax.experimental import pallas as pl
from jax.experimental.pallas import tpu as pltpu

F32 = jnp.float32
BF16 = jnp.bfloat16

EPS = 1e-6
GLA_TAU = 16.0
GLA_CHUNK = 64
GLA_SUB = 8
WINDOW = 128
N_BUCKETS = 32
MAX_DIST = 128
CONV_W = 3
NEG = -1e30

LANE = 128
SUBLANE = 8
MXU_COLS = 256
MXU_ROWS = 512
VMEM_LIMIT = 56 * 1024 * 1024


def _cparams(sem):
    return pltpu.CompilerParams(dimension_semantics=sem, vmem_limit_bytes=VMEM_LIMIT)


def _tile(n, pref, unit):
    if n <= pref:
        return n
    t = (pref // unit) * unit
    while t > unit and n % t:
        t -= unit
    assert n % t == 0, (n, pref, unit)
    return t


def _mxu_dtype(rows):
    return BF16 if rows % 16 == 0 else F32


def _sigmoid(x):
    return 1.0 / (1.0 + jnp.exp(-x))


def _dot(a, b):
    return jnp.dot(a, b, preferred_element_type=F32)


def _dot_nt(a, b):
    return lax.dot_general(a, b, (((1,), (1,)), ((), ())), preferred_element_type=F32)


def _col_chunks(n):
    if n % LANE:
        return [(0, n)]
    return [(c, min(MXU_COLS, n - c)) for c in range(0, n, MXU_COLS)]


def _row_chunks(n):
    m = MXU_ROWS if n % MXU_ROWS == 0 else n
    return [(r, m) for r in range(0, n, m)]


def _pad_rows(x, rows):
    if x.shape[0] == rows:
        return x
    return jnp.concatenate([x, jnp.zeros((rows - x.shape[0], x.shape[1]), x.dtype)], axis=0)


def _norm_kernel(x_ref, g_ref, o_ref):
    x = x_ref[...]
    ms = jnp.mean(x * x, axis=-1, keepdims=True)
    o_ref[...] = (x * lax.rsqrt(ms + EPS) * g_ref[...]).astype(o_ref.dtype)


def rmsnorm(x, gain, out_dtype, tm=512):
    M, K = x.shape
    tm = _tile(M, tm, SUBLANE)
    return pl.pallas_call(
        _norm_kernel, grid=(M // tm,),
        in_specs=[pl.BlockSpec((tm, K), lambda i: (i, 0)), pl.BlockSpec((1, K), lambda i: (0, 0))],
        out_specs=pl.BlockSpec((tm, K), lambda i: (i, 0)),
        out_shape=jax.ShapeDtypeStruct((M, K), out_dtype),
        compiler_params=_cparams(("parallel",)), name="rmsnorm")(x, gain.reshape(1, K))


def _mm_kernel(a_ref, w_ref, o_ref):
    for c, w in _col_chunks(o_ref.shape[1]):
        wc = w_ref[:, c:c + w].astype(BF16)
        for r, n in _row_chunks(o_ref.shape[0]):
            o_ref[r:r + n, c:c + w] = _dot(a_ref[r:r + n, :].astype(BF16), wc).astype(o_ref.dtype)


def _mm_nt_kernel(a_ref, wt_ref, o_ref):
    for c, w in _col_chunks(o_ref.shape[1]):
        wc = wt_ref[c:c + w, :].astype(BF16)
        for r, n in _row_chunks(o_ref.shape[0]):
            o_ref[r:r + n, c:c + w] = _dot_nt(a_ref[r:r + n, :], wc).astype(o_ref.dtype)


def _mm_nt_gate_kernel(a_ref, wt_ref, wglr_ref, up_ref, gb_ref, o_ref, lg_ref):
    _mm_nt_kernel(a_ref, wt_ref, o_ref)

    @pl.when(pl.program_id(1) == 0)
    def _():
        wg = _pad_rows(wglr_ref[...], LANE).astype(BF16)
        up = _pad_rows(up_ref[...], LANE).astype(BF16)
        for r, n in _row_chunks(o_ref.shape[0]):
            glr = _dot_nt(a_ref[r:r + n, :], wg)
            z = _dot(glr.astype(BF16), up) + gb_ref[...]
            lg_ref[r:r + n, :] = -(jnp.maximum(-z, 0.0) + jnp.log1p(jnp.exp(-jnp.abs(z)))) / GLA_TAU


def _res_mm_kernel(*refs):
    r_ref, o_ref = refs[-2:]
    pairs = list(zip(refs[0:-2:2], refs[1:-2:2]))
    for c, w in _col_chunks(o_ref.shape[1]):
        wcs = [w_ref[:, c:c + w].astype(BF16) for _, w_ref in pairs]
        for r, n in _row_chunks(o_ref.shape[0]):
            acc = r_ref[r:r + n, c:c + w]
            for (a_ref, _), wc in zip(pairs, wcs):
                acc = acc + _dot(a_ref[r:r + n, :].astype(BF16), wc)
            o_ref[r:r + n, c:c + w] = acc


def matmul(a, w, layer, out_dtype, tm=2048, tn=512, name="mm"):
    M, K = a.shape
    N = w.shape[-1]
    tm = _tile(M, tm, SUBLANE)
    tn = _tile(N, tn, LANE)
    return pl.pallas_call(
        _mm_kernel, grid=(M // tm, N // tn),
        in_specs=[pl.BlockSpec((tm, K), lambda i, j: (i, 0)),
                  pl.BlockSpec((None, K, tn), lambda i, j: (layer, 0, j))],
        out_specs=pl.BlockSpec((tm, tn), lambda i, j: (i, j)),
        out_shape=jax.ShapeDtypeStruct((M, N), out_dtype),
        compiler_params=_cparams(("parallel", "arbitrary")), name=name)(a, w)


def _wt_spec(layer, row0, tn, K):
    return pl.BlockSpec((None, pl.Element(tn), pl.Element(K)),
                        lambda i, j: (layer, pl.multiple_of(row0 + j * tn, SUBLANE), 0))


def matmul_nt(a, wt, layer, row0, n, out_dtype, tm=2048, tn=512, name="mm_nt"):
    M, K = a.shape
    assert row0 % SUBLANE == 0
    tm = _tile(M, tm, SUBLANE)
    tn = _tile(n, tn, LANE)
    return pl.pallas_call(
        _mm_nt_kernel, grid=(M // tm, n // tn),
        in_specs=[pl.BlockSpec((tm, K), lambda i, j: (i, 0)), _wt_spec(layer, row0, tn, K)],
        out_specs=pl.BlockSpec((tm, tn), lambda i, j: (i, j)),
        out_shape=jax.ShapeDtypeStruct((M, n), out_dtype),
        compiler_params=_cparams(("parallel", "arbitrary")), name=name)(a, wt)


def matmul_nt_gate(a, wt, layer, n, glr_row0, gate_up, gate_b, out_dtype, tm=2048, tn=512, name="mm_nt_gate"):
    M, K = a.shape
    depth, rank, gw = gate_up.shape
    assert glr_row0 % SUBLANE == 0 and rank % SUBLANE == 0
    tm = _tile(M, tm, SUBLANE)
    tn = _tile(n, tn, LANE)
    return pl.pallas_call(
        _mm_nt_gate_kernel, grid=(M // tm, n // tn),
        in_specs=[pl.BlockSpec((tm, K), lambda i, j: (i, 0)), _wt_spec(layer, 0, tn, K),
                  pl.BlockSpec((None, pl.Element(rank), pl.Element(K)), lambda i, j: (layer, glr_row0, 0)),
                  pl.BlockSpec((None, rank, gw), lambda i, j: (layer, 0, 0)),
                  pl.BlockSpec((None, 1, gw), lambda i, j: (layer, 0, 0))],
        out_specs=[pl.BlockSpec((tm, tn), lambda i, j: (i, j)), pl.BlockSpec((tm, gw), lambda i, j: (i, 0))],
        out_shape=[jax.ShapeDtypeStruct((M, n), out_dtype), jax.ShapeDtypeStruct((M, gw), F32)],
        compiler_params=_cparams(("parallel", "arbitrary")), name=name,
    )(a, wt, wt, gate_up, gate_b.reshape(depth, 1, gw))


def _res_mm_norm_kernel(a_ref, w_ref, r_ref, g_ref, o_ref, h_ref, wb_scr):
    @pl.when(pl.program_id(0) == 0)
    def _():
        for c, w in _col_chunks(wb_scr.shape[1]):
            wb_scr[:, c:c + w] = w_ref[:, c:c + w].astype(BF16)

    n_cols = o_ref.shape[1]
    for r, n in _row_chunks(o_ref.shape[0]):
        rows = slice(r, r + n)
        a = a_ref[rows, :].astype(BF16)
        ss = jnp.zeros((n, 1), F32)
        for c, w in _col_chunks(n_cols):
            x = r_ref[rows, c:c + w] + _dot(a, wb_scr[:, c:c + w])
            o_ref[rows, c:c + w] = x
            ss = ss + jnp.sum(x * x, axis=-1, keepdims=True)
        scale = lax.rsqrt(ss / n_cols + EPS)
        for c, w in _col_chunks(n_cols):
            h_ref[rows, c:c + w] = (o_ref[rows, c:c + w] * scale * g_ref[:, c:c + w]).astype(h_ref.dtype)


def res_matmul_norm(a, w, res, gain, layer, tm=512, name="res_mm_norm"):
    M, K = a.shape
    N = w.shape[-1]
    tm = _tile(M, tm, SUBLANE)
    once = pl.Buffered(1)
    return pl.pallas_call(
        _res_mm_norm_kernel, grid=(M // tm,),
        in_specs=[pl.BlockSpec((tm, K), lambda i: (i, 0)),
                  pl.BlockSpec((None, K, N), lambda i: (layer, 0, 0), pipeline_mode=once),
                  pl.BlockSpec((tm, N), lambda i: (i, 0)),
                  pl.BlockSpec((1, N), lambda i: (0, 0))],
        out_specs=[pl.BlockSpec((tm, N), lambda i: (i, 0)), pl.BlockSpec((tm, N), lambda i: (i, 0))],
        out_shape=[jax.ShapeDtypeStruct((M, N), F32), jax.ShapeDtypeStruct((M, N), BF16)],
        scratch_shapes=[pltpu.VMEM((K, N), BF16)],
        compiler_params=_cparams(("arbitrary",)), name=name)(a, w, res, gain.reshape(1, N))


def res_matmul(a_parts, w, res, layer, tm=2048, tn=512, name="res_mm"):
    M = a_parts[0].shape[0]
    N = w.shape[-1]
    tm = _tile(M, tm, SUBLANE)
    tn = _tile(N, tn, LANE)
    in_specs, args, row0 = [], [], 0
    for a in a_parts:
        kp = a.shape[1]
        assert row0 % SUBLANE == 0
        in_specs += [pl.BlockSpec((tm, kp), lambda i, j: (i, 0), pipeline_mode=pl.Buffered(1)),
                     pl.BlockSpec((None, pl.Element(kp), pl.Element(tn)),
                                  lambda i, j, r0=row0: (layer, r0, pl.multiple_of(j * tn, LANE)))]
        args += [a, w]
        row0 += kp
    assert row0 == w.shape[1]
    return pl.pallas_call(
        _res_mm_kernel, grid=(M // tm, N // tn),
        in_specs=in_specs + [pl.BlockSpec((tm, tn), lambda i, j: (i, j))],
        out_specs=pl.BlockSpec((tm, tn), lambda i, j: (i, j)),
        out_shape=jax.ShapeDtypeStruct((M, N), F32),
        compiler_params=_cparams(("parallel", "arbitrary")), name=name)(*args, res)


def _gla_chunk(qs, ks, vs, lgs, sts, md, lp, consts):
    row, rsub_l, lsub, ones = consts
    c = min(GLA_SUB, lp)
    dk = qs[0].shape[1]
    nblk = lp // c
    probs = range(len(qs))
    bs = list(lgs)
    sh = 1
    while sh < lp:
        bs = [b + jnp.where(row >= sh, pltpu.roll(b, sh, 0), 0.0) for b in bs]
        sh *= 2
    os_ = [_dot_nt((qs[i] * jnp.exp(bs[i])).astype(md), sts[i].astype(md)) for i in probs]
    rsums = []
    for q, k, b in zip(qs, ks, bs):
        ws = []
        for i in range(nblk):
            qb, kb, bb = (x[i * c:(i + 1) * c] for x in (q, k, b))
            ws += [qb * kb[s:s + 1, :] * jnp.exp(bb - bb[s:s + 1, :]) for s in range(c)]
        rsums.append(_dot(jnp.concatenate(ws, axis=0).astype(md), ones))
    a_rows = []
    for rsum in rsums:
        rows_p = []
        for i in range(nblk):
            a_i = jnp.zeros((c, LANE), F32)
            for s in range(c):
                a_i = jnp.where(lsub == i * c + s, rsum[(i * c + s) * c:(i * c + s + 1) * c], a_i)
            rows_p.append(jnp.where(lsub <= i * c + rsub_l, a_i, 0.0))
        a_rows.append(rows_p)
    n = c
    while n < lp:
        for p in range(0, lp, 2 * n):
            offs = []
            for q, k, b in zip(qs, ks, bs):
                ref = b[p + n - 1:p + n, :]
                qt = q[p + n:p + 2 * n] * jnp.exp(b[p + n:p + 2 * n] - ref)
                kt = k[p:p + n] * jnp.exp(ref - b[p:p + n])
                parts = [jnp.zeros((p, dk), F32)] if p else []
                ktp = jnp.concatenate(parts + [kt, jnp.zeros((LANE - p - n, dk), F32)], axis=0)
                offs.append(_dot_nt(_pad_rows(qt, max(n, 16)).astype(md), ktp.astype(md)))
            for rows_p, off in zip(a_rows, offs):
                for t in range(n // c):
                    rows_p[(p + n) // c + t] = rows_p[(p + n) // c + t] + off[t * c:(t + 1) * c]
        n *= 2
    vks = [_pad_rows(v, LANE) for v in vs]
    os_ = [os_[i] + _dot(jnp.concatenate(a_rows[i], axis=0).astype(md), vks[i].astype(md)) for i in probs]
    new_sts = []
    for i in probs:
        bend = bs[i][lp - 1:lp, :]
        kd = _pad_rows(ks[i] * jnp.exp(bend - bs[i]), LANE)
        new_sts.append(sts[i] * jnp.exp(bend) + _dot(vks[i].T.astype(md), kd.astype(md)))
    return os_, new_sts


def _gla_kernel(*refs, lin, lp, n_chunks, heads, scale, has_s0):
    if has_s0:
        q_ref, k_ref, v_ref, r_ref, lg_ref, gn_ref, s0_ref, o_ref, st_ref, s_scr = refs
    else:
        q_ref, k_ref, v_ref, r_ref, lg_ref, gn_ref, o_ref, st_ref, s_scr = refs
    t = pl.program_id(1)
    bb = q_ref.shape[0]
    dk = q_ref.shape[-1] // heads
    dv = v_ref.shape[-1] // heads
    c = min(GLA_SUB, lp)
    md = _mxu_dtype(lp)
    pairs = [(s, h) for s in range(bb) for h in range(heads)]

    @pl.when(t == 0)
    def _():
        for s, h in pairs:
            s_scr[s, h] = s0_ref[s, h].T if has_s0 else jnp.zeros((dv, dk), F32)

    row = lax.broadcasted_iota(jnp.int32, (lp, dk), 0)
    consts = (row, lax.broadcasted_iota(jnp.int32, (c, LANE), 0),
              lax.broadcasted_iota(jnp.int32, (c, LANE), 1), jnp.ones((dk, LANE), md))

    def chunk(ci, carry):
        rows = pl.ds(pl.multiple_of(ci * lin, lin), lin)
        kc = [slice(h * dk, (h + 1) * dk) for _, h in pairs]
        vc = [slice(h * dv, (h + 1) * dv) for _, h in pairs]
        qs = [_pad_rows(q_ref[s, rows, kc[i]].astype(F32), lp) * scale for i, (s, _) in enumerate(pairs)]
        ks = [_pad_rows(k_ref[s, rows, kc[i]].astype(F32), lp) for i, (s, _) in enumerate(pairs)]
        vs = [_pad_rows(v_ref[s, rows, vc[i]].astype(F32), lp) for i, (s, _) in enumerate(pairs)]
        lgs = [_pad_rows(lg_ref[s, rows, kc[i]], lp) for i, (s, _) in enumerate(pairs)]
        os_, sts = _gla_chunk(qs, ks, vs, lgs, [s_scr[s, h] for s, h in pairs], md, lp, consts)
        for i, (s, h) in enumerate(pairs):
            s_scr[s, h] = sts[i]
            o = os_[i][:lin]
            ms = jnp.mean(o * o, axis=-1, keepdims=True)
            y = o * lax.rsqrt(ms + EPS) * gn_ref[...]
            r = r_ref[s, rows, vc[i]].astype(F32)
            o_ref[s, rows, vc[i]] = (y * (r * _sigmoid(r))).astype(o_ref.dtype)
        return carry

    lax.fori_loop(0, n_chunks, chunk, 0)

    @pl.when(t == pl.num_programs(1) - 1)
    def _():
        for s, h in pairs:
            st_ref[s, h] = s_scr[s, h].T


def gla(zg, lg, gnorm, s0, layer, heads, dk, dv, out_dtype, seqs_per_step=1):
    B, T, _ = zg.shape
    depth = gnorm.shape[0]
    qw, vw = heads * dk, heads * dv
    assert vw % qw == 0
    lin = min(GLA_CHUNK, T)
    assert T % lin == 0
    lp = max(lin, 16)
    tt = _tile(T, 512, lin)
    bb = _tile(B, seqs_per_step, 1)
    in_specs = [pl.BlockSpec((bb, tt, qw), lambda b, t: (b, t, 0)),
                pl.BlockSpec((bb, tt, qw), lambda b, t: (b, t, 1)),
                pl.BlockSpec((bb, tt, vw), lambda b, t: (b, t, 2 * qw // vw)),
                pl.BlockSpec((bb, tt, vw), lambda b, t: (b, t, 2 * qw // vw + 1)),
                pl.BlockSpec((bb, tt, qw), lambda b, t: (b, t, 0)),
                pl.BlockSpec((None, 1, dv), lambda b, t: (layer, 0, 0))]
    args = [zg, zg, zg, zg, lg, gnorm.reshape(depth, 1, dv)]
    if s0 is not None:
        in_specs.append(pl.BlockSpec((None, bb, heads, dk, dv), lambda b, t: (layer, b, 0, 0, 0)))
        args.append(s0)
    kern = functools.partial(_gla_kernel, lin=lin, lp=lp, n_chunks=tt // lin, heads=heads, scale=dk ** -0.5,
                             has_s0=s0 is not None)
    return pl.pallas_call(
        kern, grid=(B // bb, T // tt), in_specs=in_specs,
        out_specs=[pl.BlockSpec((bb, tt, vw), lambda b, t: (b, t, 0)),
                   pl.BlockSpec((bb, heads, dk, dv), lambda b, t: (b, 0, 0, 0))],
        out_shape=[jax.ShapeDtypeStruct((B, T, vw), out_dtype),
                   jax.ShapeDtypeStruct((B, heads, dk, dv), F32)],
        scratch_shapes=[pltpu.VMEM((bb, heads, dv, dk), F32)],
        compiler_params=_cparams(("parallel", "arbitrary")), name="gla",
    )(*args)


def _swa_kernel(sink_ref, q_ref, kp_ref, kc_ref, vp_ref, vc_ref, bias_ref, o_ref, *,
                layer, heads, group, hd, scale):
    tq = q_ref.shape[0]
    md = _mxu_dtype(tq)
    k = jnp.concatenate([kp_ref[...].astype(md), kc_ref[...].astype(md)], axis=0)
    v = jnp.concatenate([vp_ref[...].astype(md), vc_ref[...].astype(md)], axis=0)
    hs = range(heads)
    kvc = [slice((h // group) * hd, (h // group + 1) * hd) for h in hs]
    ss = [_dot_nt((q_ref[:, h * hd:(h + 1) * hd].astype(F32) * scale).astype(md), k[:, kvc[h]]) + bias_ref[h]
          for h in hs]
    sinks = [sink_ref[layer, h] for h in hs]
    ms = [jnp.maximum(jnp.max(ss[h], axis=-1, keepdims=True), sinks[h]) for h in hs]
    es = [jnp.exp(ss[h] - ms[h]) for h in hs]
    dens = [jnp.sum(es[h], axis=-1, keepdims=True) + jnp.exp(sinks[h] - ms[h]) for h in hs]
    outs = [_dot(es[h].astype(md), v[:, kvc[h]]) / dens[h] for h in hs]
    per = LANE // hd
    for j in range(heads // per):
        o_ref[:, j * LANE:(j + 1) * LANE] = jnp.concatenate(
            outs[j * per:(j + 1) * per], axis=-1).astype(o_ref.dtype)


def _t5_bucket(dist):
    n = jnp.maximum(dist, 0)
    max_exact = N_BUCKETS // 2
    nf = jnp.maximum(n, 1).astype(F32)
    large = max_exact + (jnp.log(nf / max_exact) / math.log(MAX_DIST / max_exact)
                         * (N_BUCKETS - max_exact)).astype(jnp.int32)
    large = jnp.minimum(large, N_BUCKETS - 1)
    return jnp.where(n < max_exact, n, large)


def _swa_bias_table(rel_bias, tq, p, with_first):
    i = jnp.arange(tq)[:, None]
    j = jnp.arange(p + tq)[None, :]
    dist = p + i - j
    valid = (dist >= 0) & (dist <= WINDOW)
    onehot = (_t5_bucket(dist)[:, :, None] == jnp.arange(N_BUCKETS)).astype(F32)
    bias = jnp.einsum("ijb,bh->hij", onehot, rel_bias.astype(F32), precision=lax.Precision.HIGHEST)
    regular = jnp.where(valid[None], bias, NEG)
    if not with_first:
        return regular[None]
    return jnp.stack([jnp.where((valid & (j >= p))[None], bias, NEG), regular])


def swa(z3, prev_k, prev_v, sinks, bias, layer, kv_heads, hd, tq, out_dtype):
    B, T, _ = z3.shape
    heads = sinks.shape[1]
    qw, kw = heads * hd, kv_heads * hd
    assert qw % kw == 0
    ko, vo = qw // kw, qw // kw + 1
    nb = T // tq
    prompt = prev_k is None
    p = bias.shape[3] - tq
    if prompt:
        assert p == tq and bias.shape[0] == 2
        prev_specs = [pl.BlockSpec((None, p, kw), lambda b, n: (b, jnp.maximum(n - 1, 0), ko)),
                      pl.BlockSpec((None, p, kw), lambda b, n: (b, jnp.maximum(n - 1, 0), vo))]
        prev_k = prev_v = z3
        bias_spec = pl.BlockSpec((None,) + bias.shape[1:], lambda b, n: (jnp.minimum(n, 1), 0, 0, 0))
    else:
        assert nb == 1 and bias.shape[0] == 1
        prev_specs = [pl.BlockSpec((None, None, p, kw), lambda b, n: (layer, b, 0, 0))] * 2
        bias_spec = pl.BlockSpec((None,) + bias.shape[1:], lambda b, n: (0, 0, 0, 0))
    kern = functools.partial(_swa_kernel, layer=layer, heads=heads, group=heads // kv_heads, hd=hd,
                             scale=hd ** -0.5)
    return pl.pallas_call(
        kern, grid=(B, nb),
        in_specs=[pl.BlockSpec(memory_space=pltpu.SMEM),
                  pl.BlockSpec((None, tq, qw), lambda b, n: (b, n, 0)),
                  prev_specs[0],
                  pl.BlockSpec((None, tq, kw), lambda b, n: (b, n, ko)),
                  prev_specs[1],
                  pl.BlockSpec((None, tq, kw), lambda b, n: (b, n, vo)),
                  bias_spec],
        out_specs=pl.BlockSpec((None, tq, qw), lambda b, n: (b, n, 0)),
        out_shape=jax.ShapeDtypeStruct((B, T, qw), out_dtype),
        compiler_params=_cparams(("parallel", "arbitrary")), name="swa",
    )(sinks, z3, prev_k, z3, prev_v, z3, bias)


def _swa_decode_kernel(q_ref, kc_ref, kn_ref, vc_ref, vn_ref, tab_ref, o_ref, *, kv_heads, group, hd, scale):
    bb, T, _ = q_ref.shape
    kw = kc_ref.shape[-1]
    npad = tab_ref.shape[-1] - kc_ref.shape[1] - T
    md = _mxu_dtype(group * T)
    per = LANE // hd
    zpad = jnp.zeros((npad, kw), F32)
    pairs = [(s, g) for s in range(bb) for g in range(kv_heads)]
    ks = [jnp.concatenate([kc_ref[s].astype(F32), kn_ref[s].astype(F32), zpad], axis=0).astype(md) for s in range(bb)]
    vs = [jnp.concatenate([vc_ref[s].astype(F32), vn_ref[s].astype(F32), zpad], axis=0).astype(md) for s in range(bb)]
    scs = []
    for s, g in pairs:
        qs = jnp.concatenate([q_ref[s, :, (g * group + j) * hd:(g * group + j + 1) * hd].astype(F32)
                              for j in range(group)], axis=0) * scale
        scs.append(_dot_nt(qs.astype(md), ks[s][:, g * hd:(g + 1) * hd]) + tab_ref[g])
    es = [jnp.exp(sc - jnp.max(sc, axis=-1, keepdims=True)) for sc in scs]
    dens = [jnp.sum(e, axis=-1, keepdims=True) for e in es]
    os_ = [_dot(e.astype(md), vs[s][:, g * hd:(g + 1) * hd]) / den for (s, g), e, den in zip(pairs, es, dens)]
    for s in range(bb):
        outs = [os_[s * kv_heads + g][j * T:(j + 1) * T] for g in range(kv_heads) for j in range(group)]
        for j in range(len(outs) // per):
            o_ref[s, :, j * LANE:(j + 1) * LANE] = jnp.concatenate(
                outs[j * per:(j + 1) * per], axis=-1).astype(o_ref.dtype)


def _swa_decode_table(rel_bias, sinks, T, p, kv_heads):
    depth, heads = sinks.shape
    group = heads // kv_heads
    base = _swa_bias_table(rel_bias, T, p, False)[0].reshape(kv_heads, group * T, p + T)
    npad = _round_up(p + T + 1, 16) - (p + T)
    sink_col = jnp.repeat(sinks.astype(F32).reshape(depth, kv_heads, group), T, axis=2)[..., None]
    parts = [jnp.broadcast_to(base[None], (depth,) + base.shape), sink_col,
             jnp.full((depth, kv_heads, group * T, npad - 1), NEG, F32)]
    return jnp.concatenate(parts, axis=-1)


def swa_decode(z3, cache_k, cache_v, table, layer, kv_heads, hd, out_dtype, seqs_per_step=8):
    B, T, zw = z3.shape
    kw = kv_heads * hd
    qw = zw - 2 * kw
    p = cache_k.shape[2]
    bb = _tile(B, seqs_per_step, 1)
    kern = functools.partial(_swa_decode_kernel, kv_heads=kv_heads, group=qw // kw, hd=hd, scale=hd ** -0.5)
    c_spec = pl.BlockSpec((None, bb, p, kw), lambda b: (layer, b, 0, 0))
    return pl.pallas_call(
        kern, grid=(B // bb,),
        in_specs=[pl.BlockSpec((bb, T, qw), lambda b: (b, 0, 0)),
                  c_spec, pl.BlockSpec((bb, T, kw), lambda b: (b, 0, qw // kw)),
                  c_spec, pl.BlockSpec((bb, T, kw), lambda b: (b, 0, qw // kw + 1)),
                  pl.BlockSpec((None,) + table.shape[1:], lambda b: (layer, 0, 0, 0))],
        out_specs=pl.BlockSpec((bb, T, qw), lambda b: (b, 0, 0)),
        out_shape=jax.ShapeDtypeStruct((B, T, qw), out_dtype),
        compiler_params=_cparams(("parallel",)), name="swa_decode",
    )(z3, cache_k, z3, cache_v, z3, table)


class _ConvTaps:
    def __init__(self, i, j, cols, tm, prev_ref, st_ref, carry_scr, seq_len):
        self.j, self.cols, self.seq_len = j, cols, seq_len
        self.prev_ref, self.st_ref, self.carry_scr = prev_ref, st_ref, carry_scr
        self.single = tm <= seq_len
        if self.single:
            @pl.when((i % (seq_len // tm)) == 0)
            def _():
                carry_scr[j, :, cols] = prev_ref[0, :, cols]

            self.before = carry_scr[j, :, cols]

    def chunk(self, x, r):
        n, w = x.shape
        if self.single:
            before = self.before
            row8 = lax.broadcasted_iota(jnp.int32, (SUBLANE, w), 0)

            def back(s):
                xr = pltpu.roll(x, s, 0)
                top = jnp.where(row8 < s, pltpu.roll(before, s, 0), xr[0:SUBLANE])
                return jnp.concatenate([top, xr[SUBLANE:]], axis=0)

            self.before = x[n - SUBLANE:n]
            return back(1), back(2)
        t = self.seq_len
        assert r % t == 0 and n % t == 0
        seqs = slice(r // t, (r + n) // t)
        tix = lax.rem(lax.broadcasted_iota(jnp.int32, (n, w), 0), t)
        s1 = jnp.broadcast_to(self.prev_ref[seqs, SUBLANE - 1:SUBLANE, self.cols], (n // t, t, w)).reshape(n, w)
        s0 = jnp.broadcast_to(self.prev_ref[seqs, SUBLANE - 2:SUBLANE - 1, self.cols], (n // t, t, w)).reshape(n, w)
        self.st_ref[seqs, :, self.cols] = x.reshape(n // t, t, w)[:, t - SUBLANE:, :]
        x1 = jnp.where(tix == 0, s1, pltpu.roll(x, 1, 0))
        x2 = jnp.where(tix == 0, s0, jnp.where(tix == 1, s1, pltpu.roll(x, 2, 0)))
        return x1, x2

    def finish(self):
        if self.single:
            self.carry_scr[self.j, :, self.cols] = self.before
            self.st_ref[0, :, self.cols] = self.before


def _row_blocking(B, T, tm):
    if T >= tm:
        tm = _tile(T, tm, SUBLANE)
        return tm, 1, (lambda i: i // (T // tm))
    assert T % SUBLANE == 0
    nseq = _tile(B, max(tm // T, 1), 1)
    return nseq * T, nseq, (lambda i: i)


def _conv_branch_kernel(cb_ref, cc_ref, ch_ref, prev_ref, w_ref, o_ref, st_ref, carry_scr, *, seq_len):
    i, j = pl.program_id(0), pl.program_id(1)
    taps = _ConvTaps(i, j, slice(None), o_ref.shape[0], prev_ref, st_ref, carry_scr, seq_len)
    for r, n in _row_chunks(o_ref.shape[0]):
        rows = slice(r, r + n)
        u = cc_ref[rows, :].astype(F32) * ch_ref[rows, :].astype(F32)
        u1, u2 = taps.chunk(u, r)
        y = u2 * w_ref[0:1, :] + u1 * w_ref[1:2, :] + u * w_ref[2:3, :]
        o_ref[rows, :] = (cb_ref[rows, :].astype(F32) * y).astype(o_ref.dtype)
    taps.finish()


def conv_branch(z, prev8, conv_w, layer, B, T, out_dtype, tm=1024, tc=512):
    M = B * T
    width = conv_w.shape[2]
    tc = _tile(width, tc, LANE)
    nc = width // tc
    tm, nseq, seq_of = _row_blocking(B, T, tm)
    n_st = (M // tm) * nseq
    z_specs = [pl.BlockSpec((tm, tc), (lambda i, j, o=n * nc: (i, o + j))) for n in range(3)]
    out, st = pl.pallas_call(
        functools.partial(_conv_branch_kernel, seq_len=T), grid=(M // tm, nc),
        in_specs=z_specs + [pl.BlockSpec((nseq, SUBLANE, tc), lambda i, j: (seq_of(i), 0, j)),
                            pl.BlockSpec((None, CONV_W, tc), lambda i, j: (layer, 0, j))],
        out_specs=[pl.BlockSpec((tm, tc), lambda i, j: (i, j)),
                   pl.BlockSpec((nseq, SUBLANE, tc), lambda i, j: (i, 0, j))],
        out_shape=[jax.ShapeDtypeStruct((M, width), out_dtype),
                   jax.ShapeDtypeStruct((n_st, SUBLANE, width), F32)],
        scratch_shapes=[pltpu.VMEM((nc, SUBLANE, tc), F32)],
        compiler_params=_cparams(("arbitrary", "arbitrary")), name="conv_branch",
    )(z, z, z, prev8, conv_w)
    return out, st.reshape(B, n_st // B, SUBLANE, width)[:, -1]


def _ffn_up_kernel(h_ref, wu_ref, wg_ref, prev_ref, cw_ref, cb_ref, act_ref, st_ref, carry_scr, *, seq_len):
    i, j = pl.program_id(0), pl.program_id(1)
    tm, tn = act_ref.shape
    for c, w in _col_chunks(tn):
        cols = slice(c, c + w)
        wu = wu_ref[:, cols].astype(BF16)
        wg = wg_ref[:, cols].astype(BF16)
        taps = _ConvTaps(i, j, cols, tm, prev_ref, st_ref, carry_scr, seq_len)
        for r, n in _row_chunks(tm):
            h = h_ref[r:r + n, :]
            g = _dot(h, wg)
            g1, g2 = taps.chunk(g, r)
            gc = g2 * cw_ref[0:1, cols] + g1 * cw_ref[1:2, cols] + g * cw_ref[2:3, cols] + cb_ref[:, cols]
            act_ref[r:r + n, cols] = (gc * _sigmoid(gc) * _dot(h, wu)).astype(act_ref.dtype)
        taps.finish()


def ffn_first_half(h, w_up, layer, col0, width, prev8, conv_w, conv_b, B, T, out_dtype, tm=2048, tn=512):
    M, K = h.shape
    dff = w_up.shape[2] // 2
    tn = min(tn, width)
    assert col0 % LANE == 0 and dff % LANE == 0 and width % tn == 0
    nj = width // tn
    tm, nseq, seq_of = _row_blocking(B, T, tm)
    n_st = (M // tm) * nseq

    def col(j, base=col0):
        return pl.multiple_of(base + j * tn, LANE)

    def win(rows):
        return pl.BlockSpec((pl.Element(rows), pl.Element(tn)), lambda i, j: (0, col(j)))

    act, st = pl.pallas_call(
        functools.partial(_ffn_up_kernel, seq_len=T), grid=(M // tm, nj),
        in_specs=[pl.BlockSpec((tm, K), lambda i, j: (i, 0)),
                  pl.BlockSpec((None, pl.Element(K), pl.Element(tn)), lambda i, j: (layer, 0, col(j))),
                  pl.BlockSpec((None, pl.Element(K), pl.Element(tn)), lambda i, j: (layer, 0, col(j, dff + col0))),
                  pl.BlockSpec((pl.Element(nseq), pl.Element(SUBLANE), pl.Element(tn)),
                               lambda i, j: (seq_of(i) * nseq, 0, col(j))),
                  win(CONV_W), win(1)],
        out_specs=[pl.BlockSpec((tm, tn), lambda i, j: (i, j)),
                   pl.BlockSpec((nseq, SUBLANE, tn), lambda i, j: (i, 0, j))],
        out_shape=[jax.ShapeDtypeStruct((M, width), out_dtype),
                   jax.ShapeDtypeStruct((n_st, SUBLANE, width), F32)],
        scratch_shapes=[pltpu.VMEM((nj, SUBLANE, tn), F32)],
        compiler_params=_cparams(("arbitrary", "arbitrary")), name="ffn_up",
    )(h, w_up, w_up, prev8, conv_w, conv_b)
    return act, st.reshape(B, n_st // B, SUBLANE, width)[:, -1]


def _merge_kernel(a_ref, b_ref, c_ref, wb_ref, g0_ref, g1_ref, g2_ref, o_ref):
    xs = (a_ref, b_ref, c_ref)
    gs = (g0_ref, g1_ref, g2_ref)
    for c, w in _col_chunks(o_ref.shape[1]):
        cols = slice(c, c + w)
        ws = [wb_ref[n, :, cols].astype(BF16) for n in range(3)]
        for r, m in _row_chunks(o_ref.shape[0]):
            rows = slice(r, r + m)

            def term(n):
                return _sigmoid(gs[n][rows, cols].astype(F32)) * _dot(xs[n][rows, :].astype(BF16), ws[n])

            o_ref[rows, cols] = ((term(0) + term(1)) + term(2)).astype(o_ref.dtype)


def merge(br_a, br_b, br_c, wb, layer, zg, out_dtype, tm=1024, tn=512):
    M, mix = br_a.shape
    d = wb.shape[3]
    tm = _tile(M, tm, SUBLANE)
    tn = _tile(d, tn, LANE)
    br_spec = pl.BlockSpec((tm, mix), lambda i, j: (i, 0))
    g_specs = [pl.BlockSpec((tm, tn), (lambda i, j, o=n * d // tn: (i, o + j))) for n in range(3)]
    return pl.pallas_call(
        _merge_kernel, grid=(M // tm, d // tn),
        in_specs=[br_spec, br_spec, br_spec,
                  pl.BlockSpec((None, 3, mix, tn), lambda i, j: (layer, 0, 0, j))] + g_specs,
        out_specs=pl.BlockSpec((tm, tn), lambda i, j: (i, j)),
        out_shape=jax.ShapeDtypeStruct((M, d), out_dtype),
        compiler_params=_cparams(("parallel", "arbitrary")), name="merge",
    )(br_a, br_b, br_c, wb, zg, zg, zg)


def _xattn_kernel(q_ref, k_ref, v_ref, o_ref, *, heads, hd, scale):
    md = _mxu_dtype(q_ref.shape[0])
    for h in range(heads):
        sl = slice(h * hd, (h + 1) * hd)
        s = _dot_nt(q_ref[:, sl].astype(md), k_ref[:, sl].astype(md)) * scale
        e = jnp.exp(s - jnp.max(s, axis=-1, keepdims=True))
        den = jnp.sum(e, axis=-1, keepdims=True)
        o_ref[:, sl] = (_dot(e.astype(md), v_ref[:, sl].astype(md)) / den).astype(o_ref.dtype)


def xattn(q3, mem_k, mem_v, layer, heads, out_dtype, tq=512):
    B, T, w = q3.shape
    nm = mem_k.shape[-2]
    hd = w // heads
    tq = _tile(T, tq, SUBLANE)
    kern = functools.partial(_xattn_kernel, heads=heads, hd=hd, scale=hd ** -0.5)
    if mem_k.ndim == 4:
        m_spec = pl.BlockSpec((None, None, nm, w), lambda b, t: (layer, b, 0, 0))
    else:
        m_spec = pl.BlockSpec((None, nm, w), lambda b, t: (b, 0, 0))
    return pl.pallas_call(
        kern, grid=(B, T // tq),
        in_specs=[pl.BlockSpec((None, tq, w), lambda b, t: (b, t, 0)), m_spec, m_spec],
        out_specs=pl.BlockSpec((None, tq, w), lambda b, t: (b, t, 0)),
        out_shape=jax.ShapeDtypeStruct((B, T, w), out_dtype),
        compiler_params=_cparams(("parallel", "arbitrary")), name="xattn",
    )(q3, mem_k, mem_v)


def _xattn_decode_kernel(q_ref, k_ref, v_ref, o_ref, *, heads, hd, scale):
    bb, T, _ = q_ref.shape
    nm = k_ref.shape[1] // heads
    rows = max(T, 16)
    pairs = [(s, h) for s in range(bb) for h in range(heads)]
    scs = [_dot_nt(_pad_rows(q_ref[s, :, h * hd:(h + 1) * hd].astype(F32), rows).astype(BF16),
                   k_ref[s, pl.ds(h, nm, stride=heads), :].astype(BF16)) * scale for s, h in pairs]
    es = [jnp.exp(sc - jnp.max(sc, axis=-1, keepdims=True)) for sc in scs]
    dens = [jnp.sum(e, axis=-1, keepdims=True) for e in es]
    for (s, h), e, den in zip(pairs, es, dens):
        v = v_ref[s, pl.ds(h, nm, stride=heads), :].astype(BF16)
        o_ref[s, :, h * hd:(h + 1) * hd] = (_dot(e.astype(BF16), v) / den)[:T].astype(o_ref.dtype)


def xattn_decode(q3, mem_k, mem_v, layer, heads, out_dtype, seqs_per_step=8):
    B, T, w = q3.shape
    hd = w // heads
    bb = _tile(B, seqs_per_step, 1)
    kern = functools.partial(_xattn_decode_kernel, heads=heads, hd=hd, scale=hd ** -0.5)
    m_spec = pl.BlockSpec((None, bb) + mem_k.shape[2:], lambda b: (layer, b, 0, 0))
    return pl.pallas_call(
        kern, grid=(B // bb,),
        in_specs=[pl.BlockSpec((bb, T, w), lambda b: (b, 0, 0)), m_spec, m_spec],
        out_specs=pl.BlockSpec((bb, T, w), lambda b: (b, 0, 0)),
        out_shape=jax.ShapeDtypeStruct((B, T, w), out_dtype),
        compiler_params=_cparams(("parallel",)), name="xattn_decode",
    )(q3, mem_k, mem_v)


def _round_up(n, m):
    return -(-n // m) * m


def kernel(x_prompt, x_sample, state_gla, cache_swa_k, cache_swa_v, state_conv, state_ffn, cache_mem_k, cache_mem_v, mem_prompt, norm_mix, w_in, gla_gate_up, gla_gate_b, gla_norm, swa_sinks, rel_bias, conv_w, w_branch, w_out, norm_x, wx_q, wx_k, wx_v, wx_o, norm_ffn, ffn_up, ffn_conv_w, ffn_conv_b, ffn_down, norm_final):
    Bp, Tp, D = x_prompt.shape
    Bs, Ts, _ = x_sample.shape
    depth = w_in.shape[0]
    gla_heads, gla_dk, gla_dv = state_gla.shape[2:]
    swa_kv, swa_hd = cache_swa_k.shape[3:]
    swa_heads = swa_sinks.shape[1]
    x_heads, x_hd = cache_mem_k.shape[3:]
    n_mem = mem_prompt.shape[1]
    mix = conv_w.shape[2]
    dff = ffn_down.shape[1]
    rank = gla_gate_up.shape[1]
    gqk = gla_heads * gla_dk
    sqw, skw = swa_heads * swa_hd, swa_kv * swa_hd
    win = cache_swa_k.shape[2]
    xw = x_heads * x_hd
    assert win == WINDOW and mix == gla_heads * gla_dv == sqw

    w_in_t = jnp.swapaxes(w_in, 1, 2)
    a_w = 2 * gqk + 2 * mix
    s_w = sqw + 2 * skw
    o_glr, o_swa = a_w, a_w + rank
    o_conv = o_swa + s_w
    o_gates = o_conv + 3 * mix
    ffn_main = dff // 512 * 512
    ffn_parts = [(0, ffn_main)] + ([(ffn_main, dff - ffn_main)] if dff > ffn_main else [])

    bias_p = _swa_bias_table(rel_bias, min(Tp, WINDOW), WINDOW, True)
    table_s = _swa_decode_table(rel_bias, swa_sinks, Ts, win, swa_kv)
    cache_k4 = cache_swa_k.reshape(depth, Bs, win, skw)
    cache_v4 = cache_swa_v.reshape(depth, Bs, win, skw)
    cmem_k4 = cache_mem_k.reshape(depth, Bs, n_mem * x_heads, x_hd)
    cmem_v4 = cache_mem_v.reshape(depth, Bs, n_mem * x_heads, x_hd)
    mem2 = mem_prompt.reshape(Bp * n_mem, D)

    def first8(st):
        return jnp.pad(st, ((0, 0), (SUBLANE - (CONV_W - 1), 0), (0, 0)))

    def trunk_layer(i, x, B, T, decode, s0, conv_prev8, ffn_prev8, mem_k, mem_v, act_dtype):
        M = B * T
        h = rmsnorm(x, norm_mix[i], BF16)
        zg, lg = matmul_nt_gate(h, w_in_t, i, a_w, o_glr, gla_gate_up, gla_gate_b, act_dtype, name="in_proj_gla")
        zs = matmul_nt(h, w_in_t, i, o_swa, s_w, act_dtype, tn=s_w, name="in_proj_swa")
        zc = matmul_nt(h, w_in_t, i, o_conv, 3 * mix, act_dtype, name="in_proj_conv")
        zt = matmul_nt(h, w_in_t, i, o_gates, 3 * D, act_dtype, name="in_proj_gates")
        zs3 = zs.reshape(B, T, s_w)
        br_a, gla_st = gla(zg.reshape(B, T, a_w), lg.reshape(B, T, gqk), gla_norm, s0, i,
                           gla_heads, gla_dk, gla_dv, act_dtype, seqs_per_step=4 if decode else 1)
        if decode:
            br_b = swa_decode(zs3, cache_k4, cache_v4, table_s, i, swa_kv, swa_hd, act_dtype)
        else:
            br_b = swa(zs3, None, None, swa_sinks, bias_p, i, swa_kv, swa_hd, min(T, WINDOW), act_dtype)
        br_c, conv_st = conv_branch(zc, conv_prev8, conv_w, i, B, T, act_dtype)
        merged = merge(br_a.reshape(M, mix), br_b.reshape(M, mix), br_c, w_branch, i, zt, BF16)
        x, hx = res_matmul_norm(merged, w_out, x, norm_x[i], i, name="out_proj")
        q = matmul(hx, wx_q, i, act_dtype, name="xq_proj")
        if decode:
            xo = xattn_decode(q.reshape(B, T, xw), mem_k, mem_v, i, x_heads, act_dtype)
        else:
            xo = xattn(q.reshape(B, T, xw), mem_k, mem_v, i, x_heads, act_dtype)
        x, hf = res_matmul_norm(xo.reshape(M, xw), wx_o, x, norm_ffn[i], i, name="xo_proj")
        acts, ffn_sts = zip(*[ffn_first_half(hf, ffn_up, i, c0, wd, ffn_prev8, ffn_conv_w[i],
                                             ffn_conv_b[i].reshape(1, dff), B, T, BF16) for c0, wd in ffn_parts])
        x = res_matmul(list(acts), ffn_down, x, i, tm=2048, tn=256, name="ffn_down")
        ffn_st = jnp.concatenate(ffn_sts, axis=-1)
        k_new = zs3[:, T - min(T, WINDOW):, sqw:sqw + skw].astype(F32)
        v_new = zs3[:, T - min(T, WINDOW):, sqw + skw:].astype(F32)
        return x, (gla_st, k_new, v_new, conv_st[:, SUBLANE - (CONV_W - 1):],
                   ffn_st[:, SUBLANE - (CONV_W - 1):])

    xp = x_prompt.reshape(Bp * Tp, D)
    xs = x_sample.reshape(Bs * Ts, D)
    act_p = BF16
    act_s = BF16 if Ts % 16 == 0 else F32
    st_p, st_s = [], []
    for i in range(depth):
        mem_k = matmul(mem2, wx_k, i, F32, name="mem_k").reshape(Bp, n_mem, xw)
        mem_v = matmul(mem2, wx_v, i, F32, name="mem_v").reshape(Bp, n_mem, xw)
        xp, sp = trunk_layer(i, xp, Bp, Tp, False, None,
                             jnp.zeros((Bp, SUBLANE, mix), F32), jnp.zeros((Bp, SUBLANE, dff), F32),
                             mem_k, mem_v, act_p)
        xs, ss = trunk_layer(i, xs, Bs, Ts, True, state_gla, first8(state_conv[i]), first8(state_ffn[i]),
                             cmem_k4, cmem_v4, act_s)
        st_p.append(dict(gla=sp[0], k=sp[1].reshape(Bp, WINDOW, swa_kv, swa_hd),
                         v=sp[2].reshape(Bp, WINDOW, swa_kv, swa_hd), conv=sp[3], ffn=sp[4],
                         mem_k=mem_k.reshape(Bp, n_mem, x_heads, x_hd),
                         mem_v=mem_v.reshape(Bp, n_mem, x_heads, x_hd)))
        kc = jnp.concatenate([cache_k4[i], ss[1]], axis=1)[:, Ts:]
        vc = jnp.concatenate([cache_v4[i], ss[2]], axis=1)[:, Ts:]
        st_s.append(dict(gla=ss[0], k=kc.reshape(Bs, win, swa_kv, swa_hd), v=vc.reshape(Bs, win, swa_kv, swa_hd),
                         conv=ss[3], ffn=ss[4]))
    y_prompt = rmsnorm(xp, norm_final, F32).reshape(Bp, Tp, D)
    y_sample = rmsnorm(xs, norm_final, F32).reshape(Bs, Ts, D)

    def stack(lst, key):
        return jnp.stack([s[key] for s in lst])

    return (y_prompt, y_sample, stack(st_p, "gla"), stack(st_s, "gla"), stack(st_p, "k"), stack(st_p, "v"),
            stack(st_s, "k"), stack(st_s, "v"), stack(st_p, "conv"), stack(st_s, "conv"),
            stack(st_p, "ffn"), stack(st_s, "ffn"), stack(st_p, "mem_k"), stack(st_p, "mem_v"))
```

```python
import functools
import math

import jax
import jax.numpy as jnp
from jax import lax
from jax.experimental import pallas as pl
from jax.experimental.pallas import tpu as pltpu

F32 = jnp.float32
BF16 = jnp.bfloat16

EPS = 1e-6
GLA_TAU = 16.0
GLA_CHUNK = 64
GLA_SUB = 8
WINDOW = 128
N_BUCKETS = 32
MAX_DIST = 128
CONV_W = 3
NEG = -1e30

LANE = 128
SUBLANE = 8
MXU_COLS = 256
MXU_ROWS = 512
VMEM_LIMIT = 56 * 1024 * 1024


def _cparams(sem):
    return pltpu.CompilerParams(dimension_semantics=sem, vmem_limit_bytes=VMEM_LIMIT)


def _tile(n, pref, unit):
    if n <= pref:
        return n
    t = (pref // unit) * unit
    while t > unit and n % t:
        t -= unit
    assert n % t == 0, (n, pref, unit)
    return t


def _mxu_dtype(rows):
    return BF16 if rows % 16 == 0 else F32


def _sigmoid(x):
    return 1.0 / (1.0 + jnp.exp(-x))


def _dot(a, b):
    return jnp.dot(a, b, preferred_element_type=F32)


def _dot_nt(a, b):
    return lax.dot_general(a, b, (((1,), (1,)), ((), ())), preferred_element_type=F32)


def _col_chunks(n):
    if n % LANE:
        return [(0, n)]
    return [(c, min(MXU_COLS, n - c)) for c in range(0, n, MXU_COLS)]


def _row_chunks(n):
    m = MXU_ROWS if n % MXU_ROWS == 0 else n
    return [(r, m) for r in range(0, n, m)]


def _pad_rows(x, rows):
    if x.shape[0] == rows:
        return x
    return jnp.concatenate([x, jnp.zeros((rows - x.shape[0], x.shape[1]), x.dtype)], axis=0)


def _norm_kernel(x_ref, g_ref, o_ref):
    x = x_ref[...]
    ms = jnp.mean(x * x, axis=-1, keepdims=True)
    o_ref[...] = (x * lax.rsqrt(ms + EPS) * g_ref[...]).astype(o_ref.dtype)


def rmsnorm(x, gain, out_dtype, tm=512):
    M, K = x.shape
    tm = _tile(M, tm, SUBLANE)
    return pl.pallas_call(
        _norm_kernel, grid=(M // tm,),
        in_specs=[pl.BlockSpec((tm, K), lambda i: (i, 0)), pl.BlockSpec((1, K), lambda i: (0, 0))],
        out_specs=pl.BlockSpec((tm, K), lambda i: (i, 0)),
        out_shape=jax.ShapeDtypeStruct((M, K), out_dtype),
        compiler_params=_cparams(("parallel",)), name="rmsnorm")(x, gain.reshape(1, K))


def _mm_kernel(a_ref, w_ref, o_ref):
    for c, w in _col_chunks(o_ref.shape[1]):
        wc = w_ref[:, c:c + w].astype(BF16)
        for r, n in _row_chunks(o_ref.shape[0]):
            o_ref[r:r + n, c:c + w] = _dot(a_ref[r:r + n, :].astype(BF16), wc).astype(o_ref.dtype)


def _mm_nt_kernel(a_ref, wt_ref, o_ref):
    for c, w in _col_chunks(o_ref.shape[1]):
        wc = wt_ref[c:c + w, :].astype(BF16)
        for r, n in _row_chunks(o_ref.shape[0]):
            o_ref[r:r + n, c:c + w] = _dot_nt(a_ref[r:r + n, :], wc).astype(o_ref.dtype)


def _log_decay(a_ref, wglr_ref, up_ref, gb_ref, lg_ref):
    wg = _pad_rows(wglr_ref[...], LANE).astype(BF16)
    up = _pad_rows(up_ref[...], LANE).astype(BF16)
    for r, n in _row_chunks(lg_ref.shape[0]):
        glr = _dot_nt(a_ref[r:r + n, :], wg)
        z = _dot(glr.astype(BF16), up) + gb_ref[...]
        lg_ref[r:r + n, :] = -(jnp.maximum(-z, 0.0) + jnp.log1p(jnp.exp(-jnp.abs(z)))) / GLA_TAU


def _mm_nt_gate_kernel(a_ref, wt_ref, wglr_ref, up_ref, gb_ref, o_ref, lg_ref):
    _mm_nt_kernel(a_ref, wt_ref, o_ref)

    @pl.when(pl.program_id(1) == 0)
    def _():
        _log_decay(a_ref, wglr_ref, up_ref, gb_ref, lg_ref)


def _mm_nt2_kernel(a_ref, a2_ref, wt_ref, o_ref, o2_ref):
    _mm_nt_kernel(a_ref, wt_ref, o_ref)

    @pl.when(pl.program_id(0) == 0)
    def _():
        _mm_nt_kernel(a2_ref, wt_ref, o2_ref)


def _mm_nt_gate2_kernel(a_ref, a2_ref, wt_ref, wglr_ref, up_ref, gb_ref, o_ref, lg_ref, o2_ref, lg2_ref):
    _mm_nt_gate_kernel(a_ref, wt_ref, wglr_ref, up_ref, gb_ref, o_ref, lg_ref)

    @pl.when(pl.program_id(0) == 0)
    def _():
        _mm_nt_kernel(a2_ref, wt_ref, o2_ref)

    @pl.when((pl.program_id(0) == 0) & (pl.program_id(1) == 0))
    def _():
        _log_decay(a2_ref, wglr_ref, up_ref, gb_ref, lg2_ref)


def _res_mm_body(a_refs, w_refs, r_ref, o_ref):
    for c, w in _col_chunks(o_ref.shape[1]):
        wcs = [w_ref[:, c:c + w].astype(BF16) for w_ref in w_refs]
        for r, n in _row_chunks(o_ref.shape[0]):
            acc = r_ref[r:r + n, c:c + w]
            for a_ref, wc in zip(a_refs, wcs):
                acc = acc + _dot(a_ref[r:r + n, :].astype(BF16), wc)
            o_ref[r:r + n, c:c + w] = acc


def _res_mm_kernel(*refs, n_parts, two_groups):
    a_refs, w_refs = refs[0:2 * n_parts:2], refs[1:2 * n_parts:2]
    if not two_groups:
        _res_mm_body(a_refs, w_refs, refs[-2], refs[-1])
        return
    _res_mm_body(a_refs, w_refs, refs[2 * n_parts], refs[-2])

    @pl.when(pl.program_id(0) == 0)
    def _():
        _res_mm_body(refs[2 * n_parts + 1:3 * n_parts + 1], w_refs, refs[3 * n_parts + 1], refs[-1])


def matmul(a, w, layer, out_dtype, tm=2048, tn=512, name="mm"):
    M, K = a.shape
    N = w.shape[-1]
    tm = _tile(M, tm, SUBLANE)
    tn = _tile(N, tn, LANE)
    return pl.pallas_call(
        _mm_kernel, grid=(M // tm, N // tn),
        in_specs=[pl.BlockSpec((tm, K), lambda i, j: (i, 0)),
                  pl.BlockSpec((None, K, tn), lambda i, j: (layer, 0, j))],
        out_specs=pl.BlockSpec((tm, tn), lambda i, j: (i, j)),
        out_shape=jax.ShapeDtypeStruct((M, N), out_dtype),
        compiler_params=_cparams(("parallel", "arbitrary")), name=name)(a, w)


def _wt_spec(layer, row0, tn, K):
    return pl.BlockSpec((None, pl.Element(tn), pl.Element(K)),
                        lambda i, j: (layer, pl.multiple_of(row0 + j * tn, SUBLANE), 0))


def _second_group_tile(rows, tn, nj):
    return pl.BlockSpec((rows, tn), lambda i, j: (0, jnp.where(i == 0, j, nj - 1)))


def matmul_nt(a, a2, wt, layer, row0, n, out_dtype, out2_dtype, tm=2048, tn=512, name="mm_nt"):
    M, K = a.shape
    M2 = a2.shape[0]
    assert row0 % SUBLANE == 0
    tm = _tile(M, tm, SUBLANE)
    tn = _tile(n, tn, LANE)
    nj = n // tn
    return pl.pallas_call(
        _mm_nt2_kernel, grid=(M // tm, nj),
        in_specs=[pl.BlockSpec((tm, K), lambda i, j: (i, 0)), pl.BlockSpec((M2, K), lambda i, j: (0, 0)),
                  _wt_spec(layer, row0, tn, K)],
        out_specs=[pl.BlockSpec((tm, tn), lambda i, j: (i, j)), _second_group_tile(M2, tn, nj)],
        out_shape=[jax.ShapeDtypeStruct((M, n), out_dtype), jax.ShapeDtypeStruct((M2, n), out2_dtype)],
        compiler_params=_cparams(("arbitrary", "arbitrary")), name=name)(a, a2, wt)


def matmul_nt_gate(a, a2, wt, layer, n, glr_row0, gate_up, gate_b, out_dtype, out2_dtype, tm=2048, tn=512,
                   name="mm_nt_gate"):
    M, K = a.shape
    M2 = a2.shape[0]
    depth, rank, gw = gate_up.shape
    assert glr_row0 % SUBLANE == 0 and rank % SUBLANE == 0
    tm = _tile(M, tm, SUBLANE)
    tn = _tile(n, tn, LANE)
    nj = n // tn
    return pl.pallas_call(
        _mm_nt_gate2_kernel, grid=(M // tm, nj),
        in_specs=[pl.BlockSpec((tm, K), lambda i, j: (i, 0)), pl.BlockSpec((M2, K), lambda i, j: (0, 0)),
                  _wt_spec(layer, 0, tn, K),
                  pl.BlockSpec((None, pl.Element(rank), pl.Element(K)), lambda i, j: (layer, glr_row0, 0)),
                  pl.BlockSpec((None, rank, gw), lambda i, j: (layer, 0, 0)),
                  pl.BlockSpec((None, 1, gw), lambda i, j: (layer, 0, 0))],
        out_specs=[pl.BlockSpec((tm, tn), lambda i, j: (i, j)), pl.BlockSpec((tm, gw), lambda i, j: (i, 0)),
                   _second_group_tile(M2, tn, nj), pl.BlockSpec((M2, gw), lambda i, j: (0, 0))],
        out_shape=[jax.ShapeDtypeStruct((M, n), out_dtype), jax.ShapeDtypeStruct((M, gw), F32),
                   jax.ShapeDtypeStruct((M2, n), out2_dtype), jax.ShapeDtypeStruct((M2, gw), F32)],
        compiler_params=_cparams(("arbitrary", "arbitrary")), name=name,
    )(a, a2, wt, wt, gate_up, gate_b.reshape(depth, 1, gw))


def _res_mm_norm_kernel(a_ref, w_ref, r_ref, g_ref, o_ref, h_ref, wb_scr):
    @pl.when(pl.program_id(0) == 0)
    def _():
        for c, w in _col_chunks(wb_scr.shape[1]):
            wb_scr[:, c:c + w] = w_ref[:, c:c + w].astype(BF16)

    n_cols = o_ref.shape[1]
    for r, n in _row_chunks(o_ref.shape[0]):
        rows = slice(r, r + n)
        a = a_ref[rows, :].astype(BF16)
        ss = jnp.zeros((n, 1), F32)
        for c, w in _col_chunks(n_cols):
            x = r_ref[rows, c:c + w] + _dot(a, wb_scr[:, c:c + w])
            o_ref[rows, c:c + w] = x
            ss = ss + jnp.sum(x * x, axis=-1, keepdims=True)
        scale = lax.rsqrt(ss / n_cols + EPS)
        for c, w in _col_chunks(n_cols):
            h_ref[rows, c:c + w] = (o_ref[rows, c:c + w] * scale * g_ref[:, c:c + w]).astype(h_ref.dtype)


def res_matmul_norm(a, w, res, gain, layer, tm=512, name="res_mm_norm"):
    M, K = a.shape
    N = w.shape[-1]
    tm = _tile(M, tm, SUBLANE)
    once = pl.Buffered(1)
    return pl.pallas_call(
        _res_mm_norm_kernel, grid=(M // tm,),
        in_specs=[pl.BlockSpec((tm, K), lambda i: (i, 0)),
                  pl.BlockSpec((None, K, N), lambda i: (layer, 0, 0), pipeline_mode=once),
                  pl.BlockSpec((tm, N), lambda i: (i, 0)),
                  pl.BlockSpec((1, N), lambda i: (0, 0))],
        out_specs=[pl.BlockSpec((tm, N), lambda i: (i, 0)), pl.BlockSpec((tm, N), lambda i: (i, 0))],
        out_shape=[jax.ShapeDtypeStruct((M, N), F32), jax.ShapeDtypeStruct((M, N), BF16)],
        scratch_shapes=[pltpu.VMEM((K, N), BF16)],
        compiler_params=_cparams(("arbitrary",)), name=name)(a, w, res, gain.reshape(1, N))


def res_matmul(a_parts, w, res, layer, a2_parts=None, res2=None, tm=2048, tn=512, name="res_mm"):
    M = a_parts[0].shape[0]
    N = w.shape[-1]
    tm = _tile(M, tm, SUBLANE)
    tn = _tile(N, tn, LANE)
    nj = N // tn
    two = a2_parts is not None
    in_specs, args, row0 = [], [], 0
    for a in a_parts:
        kp = a.shape[1]
        assert row0 % SUBLANE == 0
        in_specs += [pl.BlockSpec((tm, kp), lambda i, j: (i, 0), pipeline_mode=pl.Buffered(1)),
                     pl.BlockSpec((None, pl.Element(kp), pl.Element(tn)),
                                  lambda i, j, r0=row0: (layer, r0, pl.multiple_of(j * tn, LANE)))]
        args += [a, w]
        row0 += kp
    assert row0 == w.shape[1]
    in_specs.append(pl.BlockSpec((tm, tn), lambda i, j: (i, j)))
    args.append(res)
    out_specs = [pl.BlockSpec((tm, tn), lambda i, j: (i, j))]
    out_shape = [jax.ShapeDtypeStruct((M, N), F32)]
    if two:
        M2 = res2.shape[0]
        in_specs += [pl.BlockSpec((M2, a.shape[1]), lambda i, j: (0, 0)) for a in a2_parts]
        in_specs.append(_second_group_tile(M2, tn, nj))
        args += list(a2_parts) + [res2]
        out_specs.append(_second_group_tile(M2, tn, nj))
        out_shape.append(jax.ShapeDtypeStruct((M2, N), F32))
    out = pl.pallas_call(
        functools.partial(_res_mm_kernel, n_parts=len(a_parts), two_groups=two), grid=(M // tm, nj),
        in_specs=in_specs, out_specs=out_specs, out_shape=out_shape,
        compiler_params=_cparams(("arbitrary", "arbitrary")), name=name)(*args)
    return out if two else out[0]


def _gla_chunk(qs, ks, vs, lgs, sts, md, lp, consts):
    row, rsub_l, lsub, ones = consts
    c = min(GLA_SUB, lp)
    dk = qs[0].shape[1]
    nblk = lp // c
    probs = range(len(qs))
    bs = list(lgs)
    sh = 1
    while sh < lp:
        bs = [b + jnp.where(row >= sh, pltpu.roll(b, sh, 0), 0.0) for b in bs]
        sh *= 2
    os_ = [_dot_nt((qs[i] * jnp.exp(bs[i])).astype(md), sts[i].astype(md)) for i in probs]
    rsums = []
    for q, k, b in zip(qs, ks, bs):
        ws = []
        for i in range(nblk):
            qb, kb, bb = (x[i * c:(i + 1) * c] for x in (q, k, b))
            ws += [qb * kb[s:s + 1, :] * jnp.exp(bb - bb[s:s + 1, :]) for s in range(c)]
        rsums.append(_dot(jnp.concatenate(ws, axis=0).astype(md), ones))
    a_rows = []
    for rsum in rsums:
        rows_p = []
        for i in range(nblk):
            a_i = jnp.zeros((c, LANE), F32)
            for s in range(c):
                a_i = jnp.where(lsub == i * c + s, rsum[(i * c + s) * c:(i * c + s + 1) * c], a_i)
            rows_p.append(jnp.where(lsub <= i * c + rsub_l, a_i, 0.0))
        a_rows.append(rows_p)
    n = c
    while n < lp:
        for p in range(0, lp, 2 * n):
            offs = []
            for q, k, b in zip(qs, ks, bs):
                ref = b[p + n - 1:p + n, :]
                qt = q[p + n:p + 2 * n] * jnp.exp(b[p + n:p + 2 * n] - ref)
                kt = k[p:p + n] * jnp.exp(ref - b[p:p + n])
                parts = [jnp.zeros((p, dk), F32)] if p else []
                ktp = jnp.concatenate(parts + [kt, jnp.zeros((LANE - p - n, dk), F32)], axis=0)
                offs.append(_dot_nt(_pad_rows(qt, max(n, 16)).astype(md), ktp.astype(md)))
            for rows_p, off in zip(a_rows, offs):
                for t in range(n // c):
                    rows_p[(p + n) // c + t] = rows_p[(p + n) // c + t] + off[t * c:(t + 1) * c]
        n *= 2
    vks = [_pad_rows(v, LANE) for v in vs]
    os_ = [os_[i] + _dot(jnp.concatenate(a_rows[i], axis=0).astype(md), vks[i].astype(md)) for i in probs]
    new_sts = []
    for i in probs:
        bend = bs[i][lp - 1:lp, :]
        kd = _pad_rows(ks[i] * jnp.exp(bend - bs[i]), LANE)
        new_sts.append(sts[i] * jnp.exp(bend) + _dot(vks[i].T.astype(md), kd.astype(md)))
    return os_, new_sts


def _gla_kernel(*refs, lin, lp, n_chunks, heads, scale, has_s0):
    if has_s0:
        q_ref, k_ref, v_ref, r_ref, lg_ref, gn_ref, s0_ref, o_ref, st_ref, s_scr = refs
    else:
        q_ref, k_ref, v_ref, r_ref, lg_ref, gn_ref, o_ref, st_ref, s_scr = refs
    t = pl.program_id(1)
    bb = q_ref.shape[0]
    dk = q_ref.shape[-1] // heads
    dv = v_ref.shape[-1] // heads
    c = min(GLA_SUB, lp)
    md = _mxu_dtype(lp)
    pairs = [(s, h) for s in range(bb) for h in range(heads)]

    @pl.when(t == 0)
    def _():
        for s, h in pairs:
            s_scr[s, h] = s0_ref[s, h].T if has_s0 else jnp.zeros((dv, dk), F32)

    row = lax.broadcasted_iota(jnp.int32, (lp, dk), 0)
    consts = (row, lax.broadcasted_iota(jnp.int32, (c, LANE), 0),
              lax.broadcasted_iota(jnp.int32, (c, LANE), 1), jnp.ones((dk, LANE), md))

    def chunk(ci, carry):
        rows = pl.ds(pl.multiple_of(ci * lin, lin), lin)
        kc = [slice(h * dk, (h + 1) * dk) for _, h in pairs]
        vc = [slice(h * dv, (h + 1) * dv) for _, h in pairs]
        qs = [_pad_rows(q_ref[s, rows, kc[i]].astype(F32), lp) * scale for i, (s, _) in enumerate(pairs)]
        ks = [_pad_rows(k_ref[s, rows, kc[i]].astype(F32), lp) for i, (s, _) in enumerate(pairs)]
        vs = [_pad_rows(v_ref[s, rows, vc[i]].astype(F32), lp) for i, (s, _) in enumerate(pairs)]
        lgs = [_pad_rows(lg_ref[s, rows, kc[i]], lp) for i, (s, _) in enumerate(pairs)]
        os_, sts = _gla_chunk(qs, ks, vs, lgs, [s_scr[s, h] for s, h in pairs], md, lp, consts)
        for i, (s, h) in enumerate(pairs):
            s_scr[s, h] = sts[i]
            o = os_[i][:lin]
            ms = jnp.mean(o * o, axis=-1, keepdims=True)
            y = o * lax.rsqrt(ms + EPS) * gn_ref[...]
            r = r_ref[s, rows, vc[i]].astype(F32)
            o_ref[s, rows, vc[i]] = (y * (r * _sigmoid(r))).astype(o_ref.dtype)
        return carry

    lax.fori_loop(0, n_chunks, chunk, 0)

    @pl.when(t == pl.num_programs(1) - 1)
    def _():
        for s, h in pairs:
            st_ref[s, h] = s_scr[s, h].T


def gla(zg, lg, gnorm, s0, layer, heads, dk, dv, out_dtype, seqs_per_step=1):
    B, T, _ = zg.shape
    depth = gnorm.shape[0]
    qw, vw = heads * dk, heads * dv
    assert vw % qw == 0
    lin = min(GLA_CHUNK, T)
    assert T % lin == 0
    lp = max(lin, 16)
    tt = _tile(T, 512, lin)
    bb = _tile(B, seqs_per_step, 1)
    in_specs = [pl.BlockSpec((bb, tt, qw), lambda b, t: (b, t, 0)),
                pl.BlockSpec((bb, tt, qw), lambda b, t: (b, t, 1)),
                pl.BlockSpec((bb, tt, vw), lambda b, t: (b, t, 2 * qw // vw)),
                pl.BlockSpec((bb, tt, vw), lambda b, t: (b, t, 2 * qw // vw + 1)),
                pl.BlockSpec((bb, tt, qw), lambda b, t: (b, t, 0)),
                pl.BlockSpec((None, 1, dv), lambda b, t: (layer, 0, 0))]
    args = [zg, zg, zg, zg, lg, gnorm.reshape(depth, 1, dv)]
    if s0 is not None:
        in_specs.append(pl.BlockSpec((None, bb, heads, dk, dv), lambda b, t: (layer, b, 0, 0, 0)))
        args.append(s0)
    kern = functools.partial(_gla_kernel, lin=lin, lp=lp, n_chunks=tt // lin, heads=heads, scale=dk ** -0.5,
                             has_s0=s0 is not None)
    return pl.pallas_call(
        kern, grid=(B // bb, T // tt), in_specs=in_specs,
        out_specs=[pl.BlockSpec((bb, tt, vw), lambda b, t: (b, t, 0)),
                   pl.BlockSpec((bb, heads, dk, dv), lambda b, t: (b, 0, 0, 0))],
        out_shape=[jax.ShapeDtypeStruct((B, T, vw), out_dtype),
                   jax.ShapeDtypeStruct((B, heads, dk, dv), F32)],
        scratch_shapes=[pltpu.VMEM((bb, heads, dv, dk), F32)],
        compiler_params=_cparams(("parallel", "arbitrary")), name="gla",
    )(*args)


def _swa_kernel(sink_ref, q_ref, kp_ref, kc_ref, vp_ref, vc_ref, bias_ref, o_ref, *,
                layer, heads, group, hd, scale):
    tq = q_ref.shape[0]
    md = _mxu_dtype(tq)
    k = jnp.concatenate([kp_ref[...].astype(md), kc_ref[...].astype(md)], axis=0)
    v = jnp.concatenate([vp_ref[...].astype(md), vc_ref[...].astype(md)], axis=0)
    hs = range(heads)
    kvc = [slice((h // group) * hd, (h // group + 1) * hd) for h in hs]
    ss = [_dot_nt((q_ref[:, h * hd:(h + 1) * hd].astype(F32) * scale).astype(md), k[:, kvc[h]]) + bias_ref[h]
          for h in hs]
    sinks = [sink_ref[layer, h] for h in hs]
    ms = [jnp.maximum(jnp.max(ss[h], axis=-1, keepdims=True), sinks[h]) for h in hs]
    es = [jnp.exp(ss[h] - ms[h]) for h in hs]
    dens = [jnp.sum(es[h], axis=-1, keepdims=True) + jnp.exp(sinks[h] - ms[h]) for h in hs]
    outs = [_dot(es[h].astype(md), v[:, kvc[h]]) / dens[h] for h in hs]
    per = LANE // hd
    for j in range(heads // per):
        o_ref[:, j * LANE:(j + 1) * LANE] = jnp.concatenate(
            outs[j * per:(j + 1) * per], axis=-1).astype(o_ref.dtype)


def _t5_bucket(dist):
    n = jnp.maximum(dist, 0)
    max_exact = N_BUCKETS // 2
    nf = jnp.maximum(n, 1).astype(F32)
    large = max_exact + (jnp.log(nf / max_exact) / math.log(MAX_DIST / max_exact)
                         * (N_BUCKETS - max_exact)).astype(jnp.int32)
    large = jnp.minimum(large, N_BUCKETS - 1)
    return jnp.where(n < max_exact, n, large)


def _swa_bias_table(rel_bias, tq, p, with_first):
    i = jnp.arange(tq)[:, None]
    j = jnp.arange(p + tq)[None, :]
    dist = p + i - j
    valid = (dist >= 0) & (dist <= WINDOW)
    onehot = (_t5_bucket(dist)[:, :, None] == jnp.arange(N_BUCKETS)).astype(F32)
    bias = jnp.einsum("ijb,bh->hij", onehot, rel_bias.astype(F32), precision=lax.Precision.HIGHEST)
    regular = jnp.where(valid[None], bias, NEG)
    if not with_first:
        return regular[None]
    return jnp.stack([jnp.where((valid & (j >= p))[None], bias, NEG), regular])


def swa(z3, prev_k, prev_v, sinks, bias, layer, kv_heads, hd, tq, out_dtype):
    B, T, _ = z3.shape
    heads = sinks.shape[1]
    qw, kw = heads * hd, kv_heads * hd
    assert qw % kw == 0
    ko, vo = qw // kw, qw // kw + 1
    nb = T // tq
    prompt = prev_k is None
    p = bias.shape[3] - tq
    if prompt:
        assert p == tq and bias.shape[0] == 2
        prev_specs = [pl.BlockSpec((None, p, kw), lambda b, n: (b, jnp.maximum(n - 1, 0), ko)),
                      pl.BlockSpec((None, p, kw), lambda b, n: (b, jnp.maximum(n - 1, 0), vo))]
        prev_k = prev_v = z3
        bias_spec = pl.BlockSpec((None,) + bias.shape[1:], lambda b, n: (jnp.minimum(n, 1), 0, 0, 0))
    else:
        assert nb == 1 and bias.shape[0] == 1
        prev_specs = [pl.BlockSpec((None, None, p, kw), lambda b, n: (layer, b, 0, 0))] * 2
        bias_spec = pl.BlockSpec((None,) + bias.shape[1:], lambda b, n: (0, 0, 0, 0))
    kern = functools.partial(_swa_kernel, layer=layer, heads=heads, group=heads // kv_heads, hd=hd,
                             scale=hd ** -0.5)
    return pl.pallas_call(
        kern, grid=(B, nb),
        in_specs=[pl.BlockSpec(memory_space=pltpu.SMEM),
                  pl.BlockSpec((None, tq, qw), lambda b, n: (b, n, 0)),
                  prev_specs[0],
                  pl.BlockSpec((None, tq, kw), lambda b, n: (b, n, ko)),
                  prev_specs[1],
                  pl.BlockSpec((None, tq, kw), lambda b, n: (b, n, vo)),
                  bias_spec],
        out_specs=pl.BlockSpec((None, tq, qw), lambda b, n: (b, n, 0)),
        out_shape=jax.ShapeDtypeStruct((B, T, qw), out_dtype),
        compiler_params=_cparams(("parallel", "arbitrary")), name="swa",
    )(sinks, z3, prev_k, z3, prev_v, z3, bias)


def _swa_decode_kernel(q_ref, kc_ref, kn_ref, vc_ref, vn_ref, tab_ref, o_ref, *, kv_heads, group, hd, scale):
    bb, T, _ = q_ref.shape
    kw = kc_ref.shape[-1]
    npad = tab_ref.shape[-1] - kc_ref.shape[1] - T
    md = _mxu_dtype(group * T)
    per = LANE // hd
    zpad = jnp.zeros((npad, kw), F32)
    pairs = [(s, g) for s in range(bb) for g in range(kv_heads)]
    ks = [jnp.concatenate([kc_ref[s].astype(F32), kn_ref[s].astype(F32), zpad], axis=0).astype(md) for s in range(bb)]
    vs = [jnp.concatenate([vc_ref[s].astype(F32), vn_ref[s].astype(F32), zpad], axis=0).astype(md) for s in range(bb)]
    scs = []
    for s, g in pairs:
        qs = jnp.concatenate([q_ref[s, :, (g * group + j) * hd:(g * group + j + 1) * hd].astype(F32)
                              for j in range(group)], axis=0) * scale
        scs.append(_dot_nt(qs.astype(md), ks[s][:, g * hd:(g + 1) * hd]) + tab_ref[g])
    es = [jnp.exp(sc - jnp.max(sc, axis=-1, keepdims=True)) for sc in scs]
    dens = [jnp.sum(e, axis=-1, keepdims=True) for e in es]
    os_ = [_dot(e.astype(md), vs[s][:, g * hd:(g + 1) * hd]) / den for (s, g), e, den in zip(pairs, es, dens)]
    for s in range(bb):
        outs = [os_[s * kv_heads + g][j * T:(j + 1) * T] for g in range(kv_heads) for j in range(group)]
        for j in range(len(outs) // per):
            o_ref[s, :, j * LANE:(j + 1) * LANE] = jnp.concatenate(
                outs[j * per:(j + 1) * per], axis=-1).astype(o_ref.dtype)


def _swa_decode_table(rel_bias, sinks, T, p, kv_heads):
    depth, heads = sinks.shape
    group = heads // kv_heads
    base = _swa_bias_table(rel_bias, T, p, False)[0].reshape(kv_heads, group * T, p + T)
    npad = _round_up(p + T + 1, 16) - (p + T)
    sink_col = jnp.repeat(sinks.astype(F32).reshape(depth, kv_heads, group), T, axis=2)[..., None]
    parts = [jnp.broadcast_to(base[None], (depth,) + base.shape), sink_col,
             jnp.full((depth, kv_heads, group * T, npad - 1), NEG, F32)]
    return jnp.concatenate(parts, axis=-1)


def swa_decode(z3, cache_k, cache_v, table, layer, kv_heads, hd, out_dtype, seqs_per_step=8):
    B, T, zw = z3.shape
    kw = kv_heads * hd
    qw = zw - 2 * kw
    p = cache_k.shape[2]
    bb = _tile(B, seqs_per_step, 1)
    kern = functools.partial(_swa_decode_kernel, kv_heads=kv_heads, group=qw // kw, hd=hd, scale=hd ** -0.5)
    c_spec = pl.BlockSpec((None, bb, p, kw), lambda b: (layer, b, 0, 0))
    return pl.pallas_call(
        kern, grid=(B // bb,),
        in_specs=[pl.BlockSpec((bb, T, qw), lambda b: (b, 0, 0)),
                  c_spec, pl.BlockSpec((bb, T, kw), lambda b: (b, 0, qw // kw)),
                  c_spec, pl.BlockSpec((bb, T, kw), lambda b: (b, 0, qw // kw + 1)),
                  pl.BlockSpec((None,) + table.shape[1:], lambda b: (layer, 0, 0, 0))],
        out_specs=pl.BlockSpec((bb, T, qw), lambda b: (b, 0, 0)),
        out_shape=jax.ShapeDtypeStruct((B, T, qw), out_dtype),
        compiler_params=_cparams(("parallel",)), name="swa_decode",
    )(z3, cache_k, z3, cache_v, z3, table)


class _ConvTaps:
    def __init__(self, i, j, cols, tm, prev_ref, st_ref, carry_scr, seq_len):
        self.j, self.cols, self.seq_len = j, cols, seq_len
        self.prev_ref, self.st_ref, self.carry_scr = prev_ref, st_ref, carry_scr
        self.single = tm <= seq_len
        if self.single:
            @pl.when((i % (seq_len // tm)) == 0)
            def _():
                carry_scr[j, :, cols] = prev_ref[0, :, cols]

            self.before = carry_scr[j, :, cols]

    def chunk(self, x, r):
        n, w = x.shape
        if self.single:
            before = self.before
            row8 = lax.broadcasted_iota(jnp.int32, (SUBLANE, w), 0)

            def back(s):
                xr = pltpu.roll(x, s, 0)
                top = jnp.where(row8 < s, pltpu.roll(before, s, 0), xr[0:SUBLANE])
                return jnp.concatenate([top, xr[SUBLANE:]], axis=0)

            self.before = x[n - SUBLANE:n]
            return back(1), back(2)
        t = self.seq_len
        assert r % t == 0 and n % t == 0
        seqs = slice(r // t, (r + n) // t)
        tix = lax.rem(lax.broadcasted_iota(jnp.int32, (n, w), 0), t)
        s1 = jnp.broadcast_to(self.prev_ref[seqs, SUBLANE - 1:SUBLANE, self.cols], (n // t, t, w)).reshape(n, w)
        s0 = jnp.broadcast_to(self.prev_ref[seqs, SUBLANE - 2:SUBLANE - 1, self.cols], (n // t, t, w)).reshape(n, w)
        self.st_ref[seqs, :, self.cols] = x.reshape(n // t, t, w)[:, t - SUBLANE:, :]
        x1 = jnp.where(tix == 0, s1, pltpu.roll(x, 1, 0))
        x2 = jnp.where(tix == 0, s0, jnp.where(tix == 1, s1, pltpu.roll(x, 2, 0)))
        return x1, x2

    def finish(self):
        if self.single:
            self.carry_scr[self.j, :, self.cols] = self.before
            self.st_ref[0, :, self.cols] = self.before


def _row_blocking(B, T, tm):
    if T >= tm:
        tm = _tile(T, tm, SUBLANE)
        return tm, 1, (lambda i: i // (T // tm))
    assert T % SUBLANE == 0
    nseq = _tile(B, max(tm // T, 1), 1)
    return nseq * T, nseq, (lambda i: i)


def _conv_branch_kernel(cb_ref, cc_ref, ch_ref, prev_ref, w_ref, o_ref, st_ref, carry_scr, *, seq_len):
    i, j = pl.program_id(0), pl.program_id(1)
    taps = _ConvTaps(i, j, slice(None), o_ref.shape[0], prev_ref, st_ref, carry_scr, seq_len)
    for r, n in _row_chunks(o_ref.shape[0]):
        rows = slice(r, r + n)
        u = cc_ref[rows, :].astype(F32) * ch_ref[rows, :].astype(F32)
        u1, u2 = taps.chunk(u, r)
        y = u2 * w_ref[0:1, :] + u1 * w_ref[1:2, :] + u * w_ref[2:3, :]
        o_ref[rows, :] = (cb_ref[rows, :].astype(F32) * y).astype(o_ref.dtype)
    taps.finish()


def conv_branch(z, prev8, conv_w, layer, B, T, out_dtype, tm=1024, tc=512):
    M = B * T
    width = conv_w.shape[2]
    tc = _tile(width, tc, LANE)
    nc = width // tc
    tm, nseq, seq_of = _row_blocking(B, T, tm)
    n_st = (M // tm) * nseq
    z_specs = [pl.BlockSpec((tm, tc), (lambda i, j, o=n * nc: (i, o + j))) for n in range(3)]
    out, st = pl.pallas_call(
        functools.partial(_conv_branch_kernel, seq_len=T), grid=(M // tm, nc),
        in_specs=z_specs + [pl.BlockSpec((nseq, SUBLANE, tc), lambda i, j: (seq_of(i), 0, j)),
                            pl.BlockSpec((None, CONV_W, tc), lambda i, j: (layer, 0, j))],
        out_specs=[pl.BlockSpec((tm, tc), lambda i, j: (i, j)),
                   pl.BlockSpec((nseq, SUBLANE, tc), lambda i, j: (i, 0, j))],
        out_shape=[jax.ShapeDtypeStruct((M, width), out_dtype),
                   jax.ShapeDtypeStruct((n_st, SUBLANE, width), F32)],
        scratch_shapes=[pltpu.VMEM((nc, SUBLANE, tc), F32)],
        compiler_params=_cparams(("arbitrary", "arbitrary")), name="conv_branch",
    )(z, z, z, prev8, conv_w)
    return out, st.reshape(B, n_st // B, SUBLANE, width)[:, -1]


def _ffn_up_kernel(h_ref, wu_ref, wg_ref, prev_ref, cw_ref, cb_ref, act_ref, st_ref, carry_scr, *, seq_len):
    i, j = pl.program_id(0), pl.program_id(1)
    tm, tn = act_ref.shape
    for c, w in _col_chunks(tn):
        cols = slice(c, c + w)
        wu = wu_ref[:, cols].astype(BF16)
        wg = wg_ref[:, cols].astype(BF16)
        taps = _ConvTaps(i, j, cols, tm, prev_ref, st_ref, carry_scr, seq_len)
        for r, n in _row_chunks(tm):
            h = h_ref[r:r + n, :]
            g = _dot(h, wg)
            g1, g2 = taps.chunk(g, r)
            gc = g2 * cw_ref[0:1, cols] + g1 * cw_ref[1:2, cols] + g * cw_ref[2:3, cols] + cb_ref[:, cols]
            act_ref[r:r + n, cols] = (gc * _sigmoid(gc) * _dot(h, wu)).astype(act_ref.dtype)
        taps.finish()


def ffn_first_half(h, w_up, layer, col0, width, prev8, conv_w, conv_b, B, T, out_dtype, tm=2048, tn=512):
    M, K = h.shape
    dff = w_up.shape[2] // 2
    tn = min(tn, width)
    assert col0 % LANE == 0 and dff % LANE == 0 and width % tn == 0
    nj = width // tn
    tm, nseq, seq_of = _row_blocking(B, T, tm)
    n_st = (M // tm) * nseq

    def col(j, base=col0):
        return pl.multiple_of(base + j * tn, LANE)

    def win(rows):
        return pl.BlockSpec((pl.Element(rows), pl.Element(tn)), lambda i, j: (0, col(j)))

    act, st = pl.pallas_call(
        functools.partial(_ffn_up_kernel, seq_len=T), grid=(M // tm, nj),
        in_specs=[pl.BlockSpec((tm, K), lambda i, j: (i, 0)),
                  pl.BlockSpec((None, pl.Element(K), pl.Element(tn)), lambda i, j: (layer, 0, col(j))),
                  pl.BlockSpec((None, pl.Element(K), pl.Element(tn)), lambda i, j: (layer, 0, col(j, dff + col0))),
                  pl.BlockSpec((pl.Element(nseq), pl.Element(SUBLANE), pl.Element(tn)),
                               lambda i, j: (seq_of(i) * nseq, 0, col(j))),
                  win(CONV_W), win(1)],
        out_specs=[pl.BlockSpec((tm, tn), lambda i, j: (i, j)),
                   pl.BlockSpec((nseq, SUBLANE, tn), lambda i, j: (i, 0, j))],
        out_shape=[jax.ShapeDtypeStruct((M, width), out_dtype),
                   jax.ShapeDtypeStruct((n_st, SUBLANE, width), F32)],
        scratch_shapes=[pltpu.VMEM((nj, SUBLANE, tn), F32)],
        compiler_params=_cparams(("arbitrary", "arbitrary")), name="ffn_up",
    )(h, w_up, w_up, prev8, conv_w, conv_b)
    return act, st.reshape(B, n_st // B, SUBLANE, width)[:, -1]


def _merge_kernel(a_ref, b_ref, c_ref, wb_ref, g0_ref, g1_ref, g2_ref, o_ref):
    xs = (a_ref, b_ref, c_ref)
    gs = (g0_ref, g1_ref, g2_ref)
    for c, w in _col_chunks(o_ref.shape[1]):
        cols = slice(c, c + w)
        ws = [wb_ref[n, :, cols].astype(BF16) for n in range(3)]
        for r, m in _row_chunks(o_ref.shape[0]):
            rows = slice(r, r + m)

            def term(n):
                return _sigmoid(gs[n][rows, cols].astype(F32)) * _dot(xs[n][rows, :].astype(BF16), ws[n])

            o_ref[rows, cols] = ((term(0) + term(1)) + term(2)).astype(o_ref.dtype)


def merge(br_a, br_b, br_c, wb, layer, zg, out_dtype, tm=1024, tn=512):
    M, mix = br_a.shape
    d = wb.shape[3]
    tm = _tile(M, tm, SUBLANE)
    tn = _tile(d, tn, LANE)
    br_spec = pl.BlockSpec((tm, mix), lambda i, j: (i, 0))
    g_specs = [pl.BlockSpec((tm, tn), (lambda i, j, o=n * d // tn: (i, o + j))) for n in range(3)]
    return pl.pallas_call(
        _merge_kernel, grid=(M // tm, d // tn),
        in_specs=[br_spec, br_spec, br_spec,
                  pl.BlockSpec((None, 3, mix, tn), lambda i, j: (layer, 0, 0, j))] + g_specs,
        out_specs=pl.BlockSpec((tm, tn), lambda i, j: (i, j)),
        out_shape=jax.ShapeDtypeStruct((M, d), out_dtype),
        compiler_params=_cparams(("parallel", "arbitrary")), name="merge",
    )(br_a, br_b, br_c, wb, zg, zg, zg)


def _xattn_kernel(q_ref, k_ref, v_ref, o_ref, *, heads, hd, scale):
    md = _mxu_dtype(q_ref.shape[0])
    for h in range(heads):
        sl = slice(h * hd, (h + 1) * hd)
        s = _dot_nt(q_ref[:, sl].astype(md), k_ref[:, sl].astype(md)) * scale
        e = jnp.exp(s - jnp.max(s, axis=-1, keepdims=True))
        den = jnp.sum(e, axis=-1, keepdims=True)
        o_ref[:, sl] = (_dot(e.astype(md), v_ref[:, sl].astype(md)) / den).astype(o_ref.dtype)


def xattn(q3, mem_k, mem_v, layer, heads, out_dtype, tq=512):
    B, T, w = q3.shape
    nm = mem_k.shape[-2]
    hd = w // heads
    tq = _tile(T, tq, SUBLANE)
    kern = functools.partial(_xattn_kernel, heads=heads, hd=hd, scale=hd ** -0.5)
    if mem_k.ndim == 4:
        m_spec = pl.BlockSpec((None, None, nm, w), lambda b, t: (layer, b, 0, 0))
    else:
        m_spec = pl.BlockSpec((None, nm, w), lambda b, t: (b, 0, 0))
    return pl.pallas_call(
        kern, grid=(B, T // tq),
        in_specs=[pl.BlockSpec((None, tq, w), lambda b, t: (b, t, 0)), m_spec, m_spec],
        out_specs=pl.BlockSpec((None, tq, w), lambda b, t: (b, t, 0)),
        out_shape=jax.ShapeDtypeStruct((B, T, w), out_dtype),
        compiler_params=_cparams(("parallel", "arbitrary")), name="xattn",
    )(q3, mem_k, mem_v)


def _xattn_decode_kernel(q_ref, k_ref, v_ref, o_ref, *, heads, hd, scale):
    bb, T, _ = q_ref.shape
    nm = k_ref.shape[1] // heads
    rows = max(T, 16)
    pairs = [(s, h) for s in range(bb) for h in range(heads)]
    scs = [_dot_nt(_pad_rows(q_ref[s, :, h * hd:(h + 1) * hd].astype(F32), rows).astype(BF16),
                   k_ref[s, pl.ds(h, nm, stride=heads), :].astype(BF16)) * scale for s, h in pairs]
    es = [jnp.exp(sc - jnp.max(sc, axis=-1, keepdims=True)) for sc in scs]
    dens = [jnp.sum(e, axis=-1, keepdims=True) for e in es]
    for (s, h), e, den in zip(pairs, es, dens):
        v = v_ref[s, pl.ds(h, nm, stride=heads), :].astype(BF16)
        o_ref[s, :, h * hd:(h + 1) * hd] = (_dot(e.astype(BF16), v) / den)[:T].astype(o_ref.dtype)


def xattn_decode(q3, mem_k, mem_v, layer, heads, out_dtype, seqs_per_step=8):
    B, T, w = q3.shape
    hd = w // heads
    bb = _tile(B, seqs_per_step, 1)
    kern = functools.partial(_xattn_decode_kernel, heads=heads, hd=hd, scale=hd ** -0.5)
    m_spec = pl.BlockSpec((None, bb) + mem_k.shape[2:], lambda b: (layer, b, 0, 0))
    return pl.pallas_call(
        kern, grid=(B // bb,),
        in_specs=[pl.BlockSpec((bb, T, w), lambda b: (b, 0, 0)), m_spec, m_spec],
        out_specs=pl.BlockSpec((bb, T, w), lambda b: (b, 0, 0)),
        out_shape=jax.ShapeDtypeStruct((B, T, w), out_dtype),
        compiler_params=_cparams(("parallel",)), name="xattn_decode",
    )(q3, mem_k, mem_v)


def _round_up(n, m):
    return -(-n // m) * m


def kernel(x_prompt, x_sample, state_gla, cache_swa_k, cache_swa_v, state_conv, state_ffn, cache_mem_k, cache_mem_v, mem_prompt, norm_mix, w_in, gla_gate_up, gla_gate_b, gla_norm, swa_sinks, rel_bias, conv_w, w_branch, w_out, norm_x, wx_q, wx_k, wx_v, wx_o, norm_ffn, ffn_up, ffn_conv_w, ffn_conv_b, ffn_down, norm_final):
    Bp, Tp, D = x_prompt.shape
    Bs, Ts, _ = x_sample.shape
    depth = w_in.shape[0]
    gla_heads, gla_dk, gla_dv = state_gla.shape[2:]
    swa_kv, swa_hd = cache_swa_k.shape[3:]
    swa_heads = swa_sinks.shape[1]
    x_heads, x_hd = cache_mem_k.shape[3:]
    n_mem = mem_prompt.shape[1]
    mix = conv_w.shape[2]
    dff = ffn_down.shape[1]
    rank = gla_gate_up.shape[1]
    gqk = gla_heads * gla_dk
    sqw, skw = swa_heads * swa_hd, swa_kv * swa_hd
    win = cache_swa_k.shape[2]
    xw = x_heads * x_hd
    assert win == WINDOW and mix == gla_heads * gla_dv == sqw

    w_in_t = jnp.swapaxes(w_in, 1, 2)
    a_w = 2 * gqk + 2 * mix
    s_w = sqw + 2 * skw
    o_glr, o_swa = a_w, a_w + rank
    o_conv = o_swa + s_w
    o_gates = o_conv + 3 * mix
    ffn_main = dff // 512 * 512
    ffn_parts = [(0, ffn_main)] + ([(ffn_main, dff - ffn_main)] if dff > ffn_main else [])

    bias_p = _swa_bias_table(rel_bias, min(Tp, WINDOW), WINDOW, True)
    table_s = _swa_decode_table(rel_bias, swa_sinks, Ts, win, swa_kv)
    cache_k4 = cache_swa_k.reshape(depth, Bs, win, skw)
    cache_v4 = cache_swa_v.reshape(depth, Bs, win, skw)
    cmem_k4 = cache_mem_k.reshape(depth, Bs, n_mem * x_heads, x_hd)
    cmem_v4 = cache_mem_v.reshape(depth, Bs, n_mem * x_heads, x_hd)
    mem2 = mem_prompt.reshape(Bp * n_mem, D)

    def first8(st):
        return jnp.pad(st, ((0, 0), (SUBLANE - (CONV_W - 1), 0), (0, 0)))

    def in_proj(i, xa, xb, dt_a, dt_b):
        ha, hb = rmsnorm(xa, norm_mix[i], BF16), rmsnorm(xb, norm_mix[i], BF16)
        zg_a, lg_a, zg_b, lg_b = matmul_nt_gate(ha, hb, w_in_t, i, a_w, o_glr, gla_gate_up, gla_gate_b, dt_a, dt_b,
                                                name="in_proj_gla")
        zs_a, zs_b = matmul_nt(ha, hb, w_in_t, i, o_swa, s_w, dt_a, dt_b, tn=s_w, name="in_proj_swa")
        zc_a, zc_b = matmul_nt(ha, hb, w_in_t, i, o_conv, 3 * mix, dt_a, dt_b, name="in_proj_conv")
        zt_a, zt_b = matmul_nt(ha, hb, w_in_t, i, o_gates, 3 * D, dt_a, dt_b, name="in_proj_gates")
        return (zg_a, lg_a, zs_a, zc_a, zt_a), (zg_b, lg_b, zs_b, zc_b, zt_b)

    def trunk_layer(i, x, zs_all, B, T, decode, s0, conv_prev8, ffn_prev8, mem_k, mem_v, act_dtype):
        M = B * T
        zg, lg, zs, zc, zt = zs_all
        zs3 = zs.reshape(B, T, s_w)
        br_a, gla_st = gla(zg.reshape(B, T, a_w), lg.reshape(B, T, gqk), gla_norm, s0, i,
                           gla_heads, gla_dk, gla_dv, act_dtype, seqs_per_step=4 if decode else 1)
        if decode:
            br_b = swa_decode(zs3, cache_k4, cache_v4, table_s, i, swa_kv, swa_hd, act_dtype)
        else:
            br_b = swa(zs3, None, None, swa_sinks, bias_p, i, swa_kv, swa_hd, min(T, WINDOW), act_dtype)
        br_c, conv_st = conv_branch(zc, conv_prev8, conv_w, i, B, T, act_dtype)
        merged = merge(br_a.reshape(M, mix), br_b.reshape(M, mix), br_c, w_branch, i, zt, BF16)
        x, hx = res_matmul_norm(merged, w_out, x, norm_x[i], i, name="out_proj")
        q = matmul(hx, wx_q, i, act_dtype, name="xq_proj")
        if decode:
            xo = xattn_decode(q.reshape(B, T, xw), mem_k, mem_v, i, x_heads, act_dtype)
        else:
            xo = xattn(q.reshape(B, T, xw), mem_k, mem_v, i, x_heads, act_dtype)
        x, hf = res_matmul_norm(xo.reshape(M, xw), wx_o, x, norm_ffn[i], i, name="xo_proj")
        acts, ffn_sts = zip(*[ffn_first_half(hf, ffn_up, i, c0, wd, ffn_prev8, ffn_conv_w[i],
                                             ffn_conv_b[i].reshape(1, dff), B, T, BF16) for c0, wd in ffn_parts])
        ffn_st = jnp.concatenate(ffn_sts, axis=-1)
        k_new = zs3[:, T - min(T, WINDOW):, sqw:sqw + skw].astype(F32)
        v_new = zs3[:, T - min(T, WINDOW):, sqw + skw:].astype(F32)
        return x, list(acts), (gla_st, k_new, v_new, conv_st[:, SUBLANE - (CONV_W - 1):],
                               ffn_st[:, SUBLANE - (CONV_W - 1):])

    xp = x_prompt.reshape(Bp * Tp, D)
    xs = x_sample.reshape(Bs * Ts, D)
    act_p = BF16
    act_s = BF16 if Ts % 16 == 0 else F32
    st_p, st_s = [], []
    for i in range(depth):
        mem_k = matmul(mem2, wx_k, i, F32, name="mem_k").reshape(Bp, n_mem, xw)
        mem_v = matmul(mem2, wx_v, i, F32, name="mem_v").reshape(Bp, n_mem, xw)
        z_p, z_s = in_proj(i, xp, xs, act_p, act_s)
        xp, acts_p, sp = trunk_layer(i, xp, z_p, Bp, Tp, False, None,
                                     jnp.zeros((Bp, SUBLANE, mix), F32), jnp.zeros((Bp, SUBLANE, dff), F32),
                                     mem_k, mem_v, act_p)
        xs, acts_s, ss = trunk_layer(i, xs, z_s, Bs, Ts, True, state_gla, first8(state_conv[i]),
                                     first8(state_ffn[i]), cmem_k4, cmem_v4, act_s)
        xp, xs = res_matmul(acts_p, ffn_down, xp, i, a2_parts=acts_s, res2=xs, tm=2048, tn=256, name="ffn_down")
        st_p.append(dict(gla=sp[0], k=sp[1].reshape(Bp, WINDOW, swa_kv, swa_hd),
                         v=sp[2].reshape(Bp, WINDOW, swa_kv, swa_hd), conv=sp[3], ffn=sp[4],
                         mem_k=mem_k.reshape(Bp, n_mem, x_heads, x_hd),
                         mem_v=mem_v.reshape(Bp, n_mem, x_heads, x_hd)))
        kc = jnp.concatenate([cache_k4[i], ss[1]], axis=1)[:, Ts:]
        vc = jnp.concatenate([cache_v4[i], ss[2]], axis=1)[:, Ts:]
        st_s.append(dict(gla=ss[0], k=kc.reshape(Bs, win, swa_kv, swa_hd), v=vc.reshape(Bs, win, swa_kv, swa_hd),
                         conv=ss[3], ffn=ss[4]))
    y_prompt = rmsnorm(xp, norm_final, F32).reshape(Bp, Tp, D)
    y_sample = rmsnorm(xs, norm_final, F32).reshape(Bs, Ts, D)

    def stack(lst, key):
        return jnp.stack([s[key] for s in lst])

    return (y_prompt, y_sample, stack(st_p, "gla"), stack(st_s, "gla"), stack(st_p, "k"), stack(st_p, "v"),
            stack(st_s, "k"), stack(st_s, "v"), stack(st_p, "conv"), stack(st_s, "conv"),
            stack(st_p, "ffn"), stack(st_s, "ffn"), stack(st_p, "mem_k"), stack(st_p, "mem_v"))
```

```python
import functools
import math

import jax
import jax.numpy as jnp
from jax import lax
from jax.experimental import pallas as pl
from jax.experimental.pallas import tpu as pltpu

F32 = jnp.float32
BF16 = jnp.bfloat16

EPS = 1e-6
GLA_TAU = 16.0
GLA_CHUNK = 64
GLA_SUB = 8
WINDOW = 128
N_BUCKETS = 32
MAX_DIST = 128
CONV_W = 3
NEG = -1e30

LANE = 128
SUBLANE = 8
MXU_COLS = 256
MXU_ROWS = 512
VMEM_LIMIT = 56 * 1024 * 1024


def _cparams(sem):
    return pltpu.CompilerParams(dimension_semantics=sem, vmem_limit_bytes=VMEM_LIMIT)


def _tile(n, pref, unit):
    if n <= pref:
        return n
    t = (pref // unit) * unit
    while t > unit and n % t:
        t -= unit
    assert n % t == 0, (n, pref, unit)
    return t


def _mxu_dtype(rows):
    return BF16 if rows % 16 == 0 else F32


def _sigmoid(x):
    return 1.0 / (1.0 + jnp.exp(-x))


def _dot(a, b):
    return jnp.dot(a, b, preferred_element_type=F32)


def _dot_nt(a, b):
    return lax.dot_general(a, b, (((1,), (1,)), ((), ())), preferred_element_type=F32)


def _col_chunks(n):
    if n % LANE:
        return [(0, n)]
    return [(c, min(MXU_COLS, n - c)) for c in range(0, n, MXU_COLS)]


def _row_chunks(n):
    m = MXU_ROWS if n % MXU_ROWS == 0 else n
    return [(r, m) for r in range(0, n, m)]


def _pad_rows(x, rows):
    if x.shape[0] == rows:
        return x
    return jnp.concatenate([x, jnp.zeros((rows - x.shape[0], x.shape[1]), x.dtype)], axis=0)


def _norm_kernel(x_ref, g_ref, o_ref):
    x = x_ref[...]
    ms = jnp.mean(x * x, axis=-1, keepdims=True)
    o_ref[...] = (x * lax.rsqrt(ms + EPS) * g_ref[...]).astype(o_ref.dtype)


def rmsnorm(x, gain, out_dtype, tm=512):
    M, K = x.shape
    tm = _tile(M, tm, SUBLANE)
    return pl.pallas_call(
        _norm_kernel, grid=(M // tm,),
        in_specs=[pl.BlockSpec((tm, K), lambda i: (i, 0)), pl.BlockSpec((1, K), lambda i: (0, 0))],
        out_specs=pl.BlockSpec((tm, K), lambda i: (i, 0)),
        out_shape=jax.ShapeDtypeStruct((M, K), out_dtype),
        compiler_params=_cparams(("parallel",)), name="rmsnorm")(x, gain.reshape(1, K))


def _mm_kernel(a_ref, w_ref, o_ref):
    for c, w in _col_chunks(o_ref.shape[1]):
        wc = w_ref[:, c:c + w].astype(BF16)
        for r, n in _row_chunks(o_ref.shape[0]):
            o_ref[r:r + n, c:c + w] = _dot(a_ref[r:r + n, :].astype(BF16), wc).astype(o_ref.dtype)


def _mm_nt_kernel(a_ref, wt_ref, o_ref):
    for c, w in _col_chunks(o_ref.shape[1]):
        wc = wt_ref[c:c + w, :].astype(BF16)
        for r, n in _row_chunks(o_ref.shape[0]):
            o_ref[r:r + n, c:c + w] = _dot_nt(a_ref[r:r + n, :], wc).astype(o_ref.dtype)


def _log_decay(a_ref, wglr_ref, up_ref, gb_ref, lg_ref):
    wg = _pad_rows(wglr_ref[...], LANE).astype(BF16)
    up = _pad_rows(up_ref[...], LANE).astype(BF16)
    for r, n in _row_chunks(lg_ref.shape[0]):
        glr = _dot_nt(a_ref[r:r + n, :], wg)
        z = _dot(glr.astype(BF16), up) + gb_ref[...]
        lg_ref[r:r + n, :] = -(jnp.maximum(-z, 0.0) + jnp.log1p(jnp.exp(-jnp.abs(z)))) / GLA_TAU


def _mm_nt_gate_kernel(a_ref, wt_ref, wglr_ref, up_ref, gb_ref, o_ref, lg_ref):
    _mm_nt_kernel(a_ref, wt_ref, o_ref)

    @pl.when(pl.program_id(1) == 0)
    def _():
        _log_decay(a_ref, wglr_ref, up_ref, gb_ref, lg_ref)


def _mm_nt2_kernel(a_ref, a2_ref, wt_ref, o_ref, o2_ref):
    _mm_nt_kernel(a_ref, wt_ref, o_ref)

    @pl.when(pl.program_id(0) == 0)
    def _():
        _mm_nt_kernel(a2_ref, wt_ref, o2_ref)


def _mm_nt_gate2_kernel(a_ref, a2_ref, wt_ref, wglr_ref, up_ref, gb_ref, o_ref, lg_ref, o2_ref, lg2_ref):
    _mm_nt_gate_kernel(a_ref, wt_ref, wglr_ref, up_ref, gb_ref, o_ref, lg_ref)

    @pl.when(pl.program_id(0) == 0)
    def _():
        _mm_nt_kernel(a2_ref, wt_ref, o2_ref)

    @pl.when((pl.program_id(0) == 0) & (pl.program_id(1) == 0))
    def _():
        _log_decay(a2_ref, wglr_ref, up_ref, gb_ref, lg2_ref)


def _res_mm_body(a_refs, w_refs, r_ref, o_ref):
    for c, w in _col_chunks(o_ref.shape[1]):
        wcs = [w_ref[:, c:c + w].astype(BF16) for w_ref in w_refs]
        for r, n in _row_chunks(o_ref.shape[0]):
            acc = r_ref[r:r + n, c:c + w]
            for a_ref, wc in zip(a_refs, wcs):
                acc = acc + _dot(a_ref[r:r + n, :].astype(BF16), wc)
            o_ref[r:r + n, c:c + w] = acc


def _res_mm_kernel(*refs, n_parts, two_groups):
    a_refs, w_refs = refs[0:2 * n_parts:2], refs[1:2 * n_parts:2]
    if not two_groups:
        _res_mm_body(a_refs, w_refs, refs[-2], refs[-1])
        return
    _res_mm_body(a_refs, w_refs, refs[2 * n_parts], refs[-2])

    @pl.when(pl.program_id(0) == 0)
    def _():
        _res_mm_body(refs[2 * n_parts + 1:3 * n_parts + 1], w_refs, refs[3 * n_parts + 1], refs[-1])


def matmul(a, w, layer, out_dtype, tm=2048, tn=512, name="mm"):
    M, K = a.shape
    N = w.shape[-1]
    tm = _tile(M, tm, SUBLANE)
    tn = _tile(N, tn, LANE)
    return pl.pallas_call(
        _mm_kernel, grid=(M // tm, N // tn),
        in_specs=[pl.BlockSpec((tm, K), lambda i, j: (i, 0)),
                  pl.BlockSpec((None, K, tn), lambda i, j: (layer, 0, j))],
        out_specs=pl.BlockSpec((tm, tn), lambda i, j: (i, j)),
        out_shape=jax.ShapeDtypeStruct((M, N), out_dtype),
        compiler_params=_cparams(("parallel", "arbitrary")), name=name)(a, w)


def _wt_spec(layer, row0, tn, K):
    return pl.BlockSpec((None, pl.Element(tn), pl.Element(K)),
                        lambda i, j: (layer, pl.multiple_of(row0 + j * tn, SUBLANE), 0))


def _second_group_tile(rows, tn, nj):
    return pl.BlockSpec((rows, tn), lambda i, j: (0, jnp.where(i == 0, j, nj - 1)))


def matmul_nt(a, a2, wt, layer, row0, n, out_dtype, out2_dtype, tm=2048, tn=512, name="mm_nt"):
    M, K = a.shape
    M2 = a2.shape[0]
    assert row0 % SUBLANE == 0
    tm = _tile(M, tm, SUBLANE)
    tn = _tile(n, tn, LANE)
    nj = n // tn
    return pl.pallas_call(
        _mm_nt2_kernel, grid=(M // tm, nj),
        in_specs=[pl.BlockSpec((tm, K), lambda i, j: (i, 0)), pl.BlockSpec((M2, K), lambda i, j: (0, 0)),
                  _wt_spec(layer, row0, tn, K)],
        out_specs=[pl.BlockSpec((tm, tn), lambda i, j: (i, j)), _second_group_tile(M2, tn, nj)],
        out_shape=[jax.ShapeDtypeStruct((M, n), out_dtype), jax.ShapeDtypeStruct((M2, n), out2_dtype)],
        compiler_params=_cparams(("arbitrary", "arbitrary")), name=name)(a, a2, wt)


def matmul_nt_gate(a, a2, wt, layer, n, glr_row0, gate_up, gate_b, out_dtype, out2_dtype, tm=2048, tn=512,
                   name="mm_nt_gate"):
    M, K = a.shape
    M2 = a2.shape[0]
    depth, rank, gw = gate_up.shape
    assert glr_row0 % SUBLANE == 0 and rank % SUBLANE == 0
    tm = _tile(M, tm, SUBLANE)
    tn = _tile(n, tn, LANE)
    nj = n // tn
    return pl.pallas_call(
        _mm_nt_gate2_kernel, grid=(M // tm, nj),
        in_specs=[pl.BlockSpec((tm, K), lambda i, j: (i, 0)), pl.BlockSpec((M2, K), lambda i, j: (0, 0)),
                  _wt_spec(layer, 0, tn, K),
                  pl.BlockSpec((None, pl.Element(rank), pl.Element(K)), lambda i, j: (layer, glr_row0, 0)),
                  pl.BlockSpec((None, rank, gw), lambda i, j: (layer, 0, 0)),
                  pl.BlockSpec((None, 1, gw), lambda i, j: (layer, 0, 0))],
        out_specs=[pl.BlockSpec((tm, tn), lambda i, j: (i, j)), pl.BlockSpec((tm, gw), lambda i, j: (i, 0)),
                   _second_group_tile(M2, tn, nj), pl.BlockSpec((M2, gw), lambda i, j: (0, 0))],
        out_shape=[jax.ShapeDtypeStruct((M, n), out_dtype), jax.ShapeDtypeStruct((M, gw), F32),
                   jax.ShapeDtypeStruct((M2, n), out2_dtype), jax.ShapeDtypeStruct((M2, gw), F32)],
        compiler_params=_cparams(("arbitrary", "arbitrary")), name=name,
    )(a, a2, wt, wt, gate_up, gate_b.reshape(depth, 1, gw))


def _res_mm_norm_kernel(a_ref, w_ref, r_ref, g_ref, o_ref, h_ref, wb_scr):
    @pl.when(pl.program_id(0) == 0)
    def _():
        for c, w in _col_chunks(wb_scr.shape[1]):
            wb_scr[:, c:c + w] = w_ref[:, c:c + w].astype(BF16)

    n_cols = o_ref.shape[1]
    for r, n in _row_chunks(o_ref.shape[0]):
        rows = slice(r, r + n)
        a = a_ref[rows, :].astype(BF16)
        ss = jnp.zeros((n, 1), F32)
        for c, w in _col_chunks(n_cols):
            x = r_ref[rows, c:c + w] + _dot(a, wb_scr[:, c:c + w])
            o_ref[rows, c:c + w] = x
            ss = ss + jnp.sum(x * x, axis=-1, keepdims=True)
        scale = lax.rsqrt(ss / n_cols + EPS)
        for c, w in _col_chunks(n_cols):
            h_ref[rows, c:c + w] = (o_ref[rows, c:c + w] * scale * g_ref[:, c:c + w]).astype(h_ref.dtype)


def res_matmul_norm(a, w, res, gain, layer, tm=512, name="res_mm_norm"):
    M, K = a.shape
    N = w.shape[-1]
    tm = _tile(M, tm, SUBLANE)
    once = pl.Buffered(1)
    return pl.pallas_call(
        _res_mm_norm_kernel, grid=(M // tm,),
        in_specs=[pl.BlockSpec((tm, K), lambda i: (i, 0)),
                  pl.BlockSpec((None, K, N), lambda i: (layer, 0, 0), pipeline_mode=once),
                  pl.BlockSpec((tm, N), lambda i: (i, 0)),
                  pl.BlockSpec((1, N), lambda i: (0, 0))],
        out_specs=[pl.BlockSpec((tm, N), lambda i: (i, 0)), pl.BlockSpec((tm, N), lambda i: (i, 0))],
        out_shape=[jax.ShapeDtypeStruct((M, N), F32), jax.ShapeDtypeStruct((M, N), BF16)],
        scratch_shapes=[pltpu.VMEM((K, N), BF16)],
        compiler_params=_cparams(("arbitrary",)), name=name)(a, w, res, gain.reshape(1, N))


def res_matmul(a_parts, w, res, layer, a2_parts=None, res2=None, tm=2048, tn=512, name="res_mm"):
    M = a_parts[0].shape[0]
    N = w.shape[-1]
    tm = _tile(M, tm, SUBLANE)
    tn = _tile(N, tn, LANE)
    nj = N // tn
    two = a2_parts is not None
    in_specs, args, row0 = [], [], 0
    for a in a_parts:
        kp = a.shape[1]
        assert row0 % SUBLANE == 0
        in_specs += [pl.BlockSpec((tm, kp), lambda i, j: (i, 0), pipeline_mode=pl.Buffered(1)),
                     pl.BlockSpec((None, pl.Element(kp), pl.Element(tn)),
                                  lambda i, j, r0=row0: (layer, r0, pl.multiple_of(j * tn, LANE)))]
        args += [a, w]
        row0 += kp
    assert row0 == w.shape[1]
    in_specs.append(pl.BlockSpec((tm, tn), lambda i, j: (i, j)))
    args.append(res)
    out_specs = [pl.BlockSpec((tm, tn), lambda i, j: (i, j))]
    out_shape = [jax.ShapeDtypeStruct((M, N), F32)]
    if two:
        M2 = res2.shape[0]
        in_specs += [pl.BlockSpec((M2, a.shape[1]), lambda i, j: (0, 0)) for a in a2_parts]
        in_specs.append(_second_group_tile(M2, tn, nj))
        args += list(a2_parts) + [res2]
        out_specs.append(_second_group_tile(M2, tn, nj))
        out_shape.append(jax.ShapeDtypeStruct((M2, N), F32))
    out = pl.pallas_call(
        functools.partial(_res_mm_kernel, n_parts=len(a_parts), two_groups=two), grid=(M // tm, nj),
        in_specs=in_specs, out_specs=out_specs, out_shape=out_shape,
        compiler_params=_cparams(("arbitrary", "arbitrary")), name=name)(*args)
    return out if two else out[0]


def _gla_chunk(qs, ks, vs, lgs, sts, md, lp, consts):
    row, rsub_l, lsub, ones = consts
    c = min(GLA_SUB, lp)
    dk = qs[0].shape[1]
    nblk = lp // c
    probs = range(len(qs))
    bs = list(lgs)
    sh = 1
    while sh < lp:
        bs = [b + jnp.where(row >= sh, pltpu.roll(b, sh, 0), 0.0) for b in bs]
        sh *= 2
    os_ = [_dot_nt((qs[i] * jnp.exp(bs[i])).astype(md), sts[i].astype(md)) for i in probs]
    rsums = []
    for q, k, b in zip(qs, ks, bs):
        ws = []
        for i in range(nblk):
            qb, kb, bb = (x[i * c:(i + 1) * c] for x in (q, k, b))
            ws += [qb * kb[s:s + 1, :] * jnp.exp(bb - bb[s:s + 1, :]) for s in range(c)]
        rsums.append(_dot(jnp.concatenate(ws, axis=0).astype(md), ones))
    a_rows = []
    for rsum in rsums:
        rows_p = []
        for i in range(nblk):
            a_i = jnp.zeros((c, LANE), F32)
            for s in range(c):
                a_i = jnp.where(lsub == i * c + s, rsum[(i * c + s) * c:(i * c + s + 1) * c], a_i)
            rows_p.append(jnp.where(lsub <= i * c + rsub_l, a_i, 0.0))
        a_rows.append(rows_p)
    n = c
    while n < lp:
        for p in range(0, lp, 2 * n):
            offs = []
            for q, k, b in zip(qs, ks, bs):
                ref = b[p + n - 1:p + n, :]
                qt = q[p + n:p + 2 * n] * jnp.exp(b[p + n:p + 2 * n] - ref)
                kt = k[p:p + n] * jnp.exp(ref - b[p:p + n])
                parts = [jnp.zeros((p, dk), F32)] if p else []
                ktp = jnp.concatenate(parts + [kt, jnp.zeros((LANE - p - n, dk), F32)], axis=0)
                offs.append(_dot_nt(_pad_rows(qt, max(n, 16)).astype(md), ktp.astype(md)))
            for rows_p, off in zip(a_rows, offs):
                for t in range(n // c):
                    rows_p[(p + n) // c + t] = rows_p[(p + n) // c + t] + off[t * c:(t + 1) * c]
        n *= 2
    vks = [_pad_rows(v, LANE) for v in vs]
    os_ = [os_[i] + _dot(jnp.concatenate(a_rows[i], axis=0).astype(md), vks[i].astype(md)) for i in probs]
    new_sts = []
    for i in probs:
        bend = bs[i][lp - 1:lp, :]
        kd = _pad_rows(ks[i] * jnp.exp(bend - bs[i]), LANE)
        new_sts.append(sts[i] * jnp.exp(bend) + _dot(vks[i].T.astype(md), kd.astype(md)))
    return os_, new_sts


def _gla_kernel(*refs, lin, lp, n_chunks, heads, scale, has_s0):
    if has_s0:
        q_ref, k_ref, v_ref, r_ref, lg_ref, gn_ref, s0_ref, o_ref, st_ref, s_scr = refs
    else:
        q_ref, k_ref, v_ref, r_ref, lg_ref, gn_ref, o_ref, st_ref, s_scr = refs
    t = pl.program_id(1)
    bb = q_ref.shape[0]
    dk = q_ref.shape[-1] // heads
    dv = v_ref.shape[-1] // heads
    c = min(GLA_SUB, lp)
    md = _mxu_dtype(lp)
    pairs = [(s, h) for s in range(bb) for h in range(heads)]

    @pl.when(t == 0)
    def _():
        for s, h in pairs:
            s_scr[s, h] = s0_ref[s, h].T if has_s0 else jnp.zeros((dv, dk), F32)

    row = lax.broadcasted_iota(jnp.int32, (lp, dk), 0)
    consts = (row, lax.broadcasted_iota(jnp.int32, (c, LANE), 0),
              lax.broadcasted_iota(jnp.int32, (c, LANE), 1), jnp.ones((dk, LANE), md))

    def chunk(ci, carry):
        rows = pl.ds(pl.multiple_of(ci * lin, lin), lin)
        kc = [slice(h * dk, (h + 1) * dk) for _, h in pairs]
        vc = [slice(h * dv, (h + 1) * dv) for _, h in pairs]
        qs = [_pad_rows(q_ref[s, rows, kc[i]].astype(F32), lp) * scale for i, (s, _) in enumerate(pairs)]
        ks = [_pad_rows(k_ref[s, rows, kc[i]].astype(F32), lp) for i, (s, _) in enumerate(pairs)]
        vs = [_pad_rows(v_ref[s, rows, vc[i]].astype(F32), lp) for i, (s, _) in enumerate(pairs)]
        lgs = [_pad_rows(lg_ref[s, rows, kc[i]], lp) for i, (s, _) in enumerate(pairs)]
        os_, sts = _gla_chunk(qs, ks, vs, lgs, [s_scr[s, h] for s, h in pairs], md, lp, consts)
        for i, (s, h) in enumerate(pairs):
            s_scr[s, h] = sts[i]
            o = os_[i][:lin]
            ms = jnp.mean(o * o, axis=-1, keepdims=True)
            y = o * lax.rsqrt(ms + EPS) * gn_ref[...]
            r = r_ref[s, rows, vc[i]].astype(F32)
            o_ref[s, rows, vc[i]] = (y * (r * _sigmoid(r))).astype(o_ref.dtype)
        return carry

    lax.fori_loop(0, n_chunks, chunk, 0)

    @pl.when(t == pl.num_programs(1) - 1)
    def _():
        for s, h in pairs:
            st_ref[s, h] = s_scr[s, h].T


def gla(zg, lg, gnorm, s0, layer, heads, dk, dv, out_dtype, seqs_per_step=1):
    B, T, _ = zg.shape
    depth = gnorm.shape[0]
    qw, vw = heads * dk, heads * dv
    assert vw % qw == 0
    lin = min(GLA_CHUNK, T)
    assert T % lin == 0
    lp = max(lin, 16)
    tt = _tile(T, 512, lin)
    bb = _tile(B, seqs_per_step, 1)
    in_specs = [pl.BlockSpec((bb, tt, qw), lambda b, t: (b, t, 0)),
                pl.BlockSpec((bb, tt, qw), lambda b, t: (b, t, 1)),
                pl.BlockSpec((bb, tt, vw), lambda b, t: (b, t, 2 * qw // vw)),
                pl.BlockSpec((bb, tt, vw), lambda b, t: (b, t, 2 * qw // vw + 1)),
                pl.BlockSpec((bb, tt, qw), lambda b, t: (b, t, 0)),
                pl.BlockSpec((None, 1, dv), lambda b, t: (layer, 0, 0))]
    args = [zg, zg, zg, zg, lg, gnorm.reshape(depth, 1, dv)]
    if s0 is not None:
        in_specs.append(pl.BlockSpec((None, bb, heads, dk, dv), lambda b, t: (layer, b, 0, 0, 0)))
        args.append(s0)
    kern = functools.partial(_gla_kernel, lin=lin, lp=lp, n_chunks=tt // lin, heads=heads, scale=dk ** -0.5,
                             has_s0=s0 is not None)
    return pl.pallas_call(
        kern, grid=(B // bb, T // tt), in_specs=in_specs,
        out_specs=[pl.BlockSpec((bb, tt, vw), lambda b, t: (b, t, 0)),
                   pl.BlockSpec((bb, heads, dk, dv), lambda b, t: (b, 0, 0, 0))],
        out_shape=[jax.ShapeDtypeStruct((B, T, vw), out_dtype),
                   jax.ShapeDtypeStruct((B, heads, dk, dv), F32)],
        scratch_shapes=[pltpu.VMEM((bb, heads, dv, dk), F32)],
        compiler_params=_cparams(("parallel", "arbitrary")), name="gla",
    )(*args)


def _swa_kernel(sink_ref, q_ref, kp_ref, kc_ref, vp_ref, vc_ref, bias_ref, o_ref, *,
                layer, heads, group, hd, scale):
    tq = q_ref.shape[0]
    md = _mxu_dtype(tq)
    k = jnp.concatenate([kp_ref[...].astype(md), kc_ref[...].astype(md)], axis=0)
    v = jnp.concatenate([vp_ref[...].astype(md), vc_ref[...].astype(md)], axis=0)
    hs = range(heads)
    kvc = [slice((h // group) * hd, (h // group + 1) * hd) for h in hs]
    ss = [_dot_nt((q_ref[:, h * hd:(h + 1) * hd].astype(F32) * scale).astype(md), k[:, kvc[h]]) + bias_ref[h]
          for h in hs]
    sinks = [sink_ref[layer, h] for h in hs]
    ms = [jnp.maximum(jnp.max(ss[h], axis=-1, keepdims=True), sinks[h]) for h in hs]
    es = [jnp.exp(ss[h] - ms[h]) for h in hs]
    dens = [jnp.sum(es[h], axis=-1, keepdims=True) + jnp.exp(sinks[h] - ms[h]) for h in hs]
    outs = [_dot(es[h].astype(md), v[:, kvc[h]]) / dens[h] for h in hs]
    per = LANE // hd
    for j in range(heads // per):
        o_ref[:, j * LANE:(j + 1) * LANE] = jnp.concatenate(
            outs[j * per:(j + 1) * per], axis=-1).astype(o_ref.dtype)


def _t5_bucket(dist):
    n = jnp.maximum(dist, 0)
    max_exact = N_BUCKETS // 2
    nf = jnp.maximum(n, 1).astype(F32)
    large = max_exact + (jnp.log(nf / max_exact) / math.log(MAX_DIST / max_exact)
                         * (N_BUCKETS - max_exact)).astype(jnp.int32)
    large = jnp.minimum(large, N_BUCKETS - 1)
    return jnp.where(n < max_exact, n, large)


def _swa_bias_table(rel_bias, tq, p, with_first):
    i = jnp.arange(tq)[:, None]
    j = jnp.arange(p + tq)[None, :]
    dist = p + i - j
    valid = (dist >= 0) & (dist <= WINDOW)
    onehot = (_t5_bucket(dist)[:, :, None] == jnp.arange(N_BUCKETS)).astype(F32)
    bias = jnp.einsum("ijb,bh->hij", onehot, rel_bias.astype(F32), precision=lax.Precision.HIGHEST)
    regular = jnp.where(valid[None], bias, NEG)
    if not with_first:
        return regular[None]
    return jnp.stack([jnp.where((valid & (j >= p))[None], bias, NEG), regular])


def swa(z3, prev_k, prev_v, sinks, bias, layer, kv_heads, hd, tq, out_dtype):
    B, T, _ = z3.shape
    heads = sinks.shape[1]
    qw, kw = heads * hd, kv_heads * hd
    assert qw % kw == 0
    ko, vo = qw // kw, qw // kw + 1
    nb = T // tq
    prompt = prev_k is None
    p = bias.shape[3] - tq
    if prompt:
        assert p == tq and bias.shape[0] == 2
        prev_specs = [pl.BlockSpec((None, p, kw), lambda b, n: (b, jnp.maximum(n - 1, 0), ko)),
                      pl.BlockSpec((None, p, kw), lambda b, n: (b, jnp.maximum(n - 1, 0), vo))]
        prev_k = prev_v = z3
        bias_spec = pl.BlockSpec((None,) + bias.shape[1:], lambda b, n: (jnp.minimum(n, 1), 0, 0, 0))
    else:
        assert nb == 1 and bias.shape[0] == 1
        prev_specs = [pl.BlockSpec((None, None, p, kw), lambda b, n: (layer, b, 0, 0))] * 2
        bias_spec = pl.BlockSpec((None,) + bias.shape[1:], lambda b, n: (0, 0, 0, 0))
    kern = functools.partial(_swa_kernel, layer=layer, heads=heads, group=heads // kv_heads, hd=hd,
                             scale=hd ** -0.5)
    return pl.pallas_call(
        kern, grid=(B, nb),
        in_specs=[pl.BlockSpec(memory_space=pltpu.SMEM),
                  pl.BlockSpec((None, tq, qw), lambda b, n: (b, n, 0)),
                  prev_specs[0],
                  pl.BlockSpec((None, tq, kw), lambda b, n: (b, n, ko)),
                  prev_specs[1],
                  pl.BlockSpec((None, tq, kw), lambda b, n: (b, n, vo)),
                  bias_spec],
        out_specs=pl.BlockSpec((None, tq, qw), lambda b, n: (b, n, 0)),
        out_shape=jax.ShapeDtypeStruct((B, T, qw), out_dtype),
        compiler_params=_cparams(("parallel", "arbitrary")), name="swa",
    )(sinks, z3, prev_k, z3, prev_v, z3, bias)


def _swa_decode_kernel(q_ref, kc_ref, kn_ref, vc_ref, vn_ref, tab_ref, o_ref, *, kv_heads, group, hd, scale):
    bb, T, _ = q_ref.shape
    kw = kc_ref.shape[-1]
    npad = tab_ref.shape[-1] - kc_ref.shape[1] - T
    md = _mxu_dtype(group * T)
    per = LANE // hd
    zpad = jnp.zeros((npad, kw), F32)
    pairs = [(s, g) for s in range(bb) for g in range(kv_heads)]
    ks = [jnp.concatenate([kc_ref[s].astype(F32), kn_ref[s].astype(F32), zpad], axis=0).astype(md) for s in range(bb)]
    vs = [jnp.concatenate([vc_ref[s].astype(F32), vn_ref[s].astype(F32), zpad], axis=0).astype(md) for s in range(bb)]
    scs = []
    for s, g in pairs:
        qs = jnp.concatenate([q_ref[s, :, (g * group + j) * hd:(g * group + j + 1) * hd].astype(F32)
                              for j in range(group)], axis=0) * scale
        scs.append(_dot_nt(qs.astype(md), ks[s][:, g * hd:(g + 1) * hd]) + tab_ref[g])
    es = [jnp.exp(sc - jnp.max(sc, axis=-1, keepdims=True)) for sc in scs]
    dens = [jnp.sum(e, axis=-1, keepdims=True) for e in es]
    os_ = [_dot(e.astype(md), vs[s][:, g * hd:(g + 1) * hd]) / den for (s, g), e, den in zip(pairs, es, dens)]
    for s in range(bb):
        outs = [os_[s * kv_heads + g][j * T:(j + 1) * T] for g in range(kv_heads) for j in range(group)]
        for j in range(len(outs) // per):
            o_ref[s, :, j * LANE:(j + 1) * LANE] = jnp.concatenate(
                outs[j * per:(j + 1) * per], axis=-1).astype(o_ref.dtype)


def _swa_decode_table(rel_bias, sinks, T, p, kv_heads):
    depth, heads = sinks.shape
    group = heads // kv_heads
    base = _swa_bias_table(rel_bias, T, p, False)[0].reshape(kv_heads, group * T, p + T)
    npad = _round_up(p + T + 1, 16) - (p + T)
    sink_col = jnp.repeat(sinks.astype(F32).reshape(depth, kv_heads, group), T, axis=2)[..., None]
    parts = [jnp.broadcast_to(base[None], (depth,) + base.shape), sink_col,
             jnp.full((depth, kv_heads, group * T, npad - 1), NEG, F32)]
    return jnp.concatenate(parts, axis=-1)


def swa_decode(z3, cache_k, cache_v, table, layer, kv_heads, hd, out_dtype, seqs_per_step=8):
    B, T, zw = z3.shape
    kw = kv_heads * hd
    qw = zw - 2 * kw
    p = cache_k.shape[2]
    bb = _tile(B, seqs_per_step, 1)
    kern = functools.partial(_swa_decode_kernel, kv_heads=kv_heads, group=qw // kw, hd=hd, scale=hd ** -0.5)
    c_spec = pl.BlockSpec((None, bb, p, kw), lambda b: (layer, b, 0, 0))
    return pl.pallas_call(
        kern, grid=(B // bb,),
        in_specs=[pl.BlockSpec((bb, T, qw), lambda b: (b, 0, 0)),
                  c_spec, pl.BlockSpec((bb, T, kw), lambda b: (b, 0, qw // kw)),
                  c_spec, pl.BlockSpec((bb, T, kw), lambda b: (b, 0, qw // kw + 1)),
                  pl.BlockSpec((None,) + table.shape[1:], lambda b: (layer, 0, 0, 0))],
        out_specs=pl.BlockSpec((bb, T, qw), lambda b: (b, 0, 0)),
        out_shape=jax.ShapeDtypeStruct((B, T, qw), out_dtype),
        compiler_params=_cparams(("parallel",)), name="swa_decode",
    )(z3, cache_k, z3, cache_v, z3, table)


class _ConvTaps:
    def __init__(self, i, j, cols, tm, prev_ref, st_ref, carry_scr, seq_len):
        self.j, self.cols, self.seq_len = j, cols, seq_len
        self.prev_ref, self.st_ref, self.carry_scr = prev_ref, st_ref, carry_scr
        self.single = tm <= seq_len
        if self.single:
            @pl.when((i % (seq_len // tm)) == 0)
            def _():
                carry_scr[j, :, cols] = prev_ref[0, :, cols]

            self.before = carry_scr[j, :, cols]

    def chunk(self, x, r):
        n, w = x.shape
        if self.single:
            before = self.before
            row8 = lax.broadcasted_iota(jnp.int32, (SUBLANE, w), 0)

            def back(s):
                xr = pltpu.roll(x, s, 0)
                top = jnp.where(row8 < s, pltpu.roll(before, s, 0), xr[0:SUBLANE])
                return jnp.concatenate([top, xr[SUBLANE:]], axis=0)

            self.before = x[n - SUBLANE:n]
            return back(1), back(2)
        t = self.seq_len
        assert r % t == 0 and n % t == 0
        seqs = slice(r // t, (r + n) // t)
        tix = lax.rem(lax.broadcasted_iota(jnp.int32, (n, w), 0), t)
        s1 = jnp.broadcast_to(self.prev_ref[seqs, SUBLANE - 1:SUBLANE, self.cols], (n // t, t, w)).reshape(n, w)
        s0 = jnp.broadcast_to(self.prev_ref[seqs, SUBLANE - 2:SUBLANE - 1, self.cols], (n // t, t, w)).reshape(n, w)
        self.st_ref[seqs, :, self.cols] = x.reshape(n // t, t, w)[:, t - SUBLANE:, :]
        x1 = jnp.where(tix == 0, s1, pltpu.roll(x, 1, 0))
        x2 = jnp.where(tix == 0, s0, jnp.where(tix == 1, s1, pltpu.roll(x, 2, 0)))
        return x1, x2

    def finish(self):
        if self.single:
            self.carry_scr[self.j, :, self.cols] = self.before
            self.st_ref[0, :, self.cols] = self.before


def _row_blocking(B, T, tm):
    if T >= tm:
        tm = _tile(T, tm, SUBLANE)
        return tm, 1, (lambda i: i // (T // tm))
    assert T % SUBLANE == 0
    nseq = _tile(B, max(tm // T, 1), 1)
    return nseq * T, nseq, (lambda i: i)


def _conv_branch_kernel(cb_ref, cc_ref, ch_ref, prev_ref, w_ref, o_ref, st_ref, carry_scr, *, seq_len):
    i, j = pl.program_id(0), pl.program_id(1)
    taps = _ConvTaps(i, j, slice(None), o_ref.shape[0], prev_ref, st_ref, carry_scr, seq_len)
    for r, n in _row_chunks(o_ref.shape[0]):
        rows = slice(r, r + n)
        u = cc_ref[rows, :].astype(F32) * ch_ref[rows, :].astype(F32)
        u1, u2 = taps.chunk(u, r)
        y = u2 * w_ref[0:1, :] + u1 * w_ref[1:2, :] + u * w_ref[2:3, :]
        o_ref[rows, :] = (cb_ref[rows, :].astype(F32) * y).astype(o_ref.dtype)
    taps.finish()


def conv_branch(z, prev8, conv_w, layer, B, T, out_dtype, tm=1024, tc=512):
    M = B * T
    width = conv_w.shape[2]
    tc = _tile(width, tc, LANE)
    nc = width // tc
    tm, nseq, seq_of = _row_blocking(B, T, tm)
    n_st = (M // tm) * nseq
    z_specs = [pl.BlockSpec((tm, tc), (lambda i, j, o=n * nc: (i, o + j))) for n in range(3)]
    out, st = pl.pallas_call(
        functools.partial(_conv_branch_kernel, seq_len=T), grid=(M // tm, nc),
        in_specs=z_specs + [pl.BlockSpec((nseq, SUBLANE, tc), lambda i, j: (seq_of(i), 0, j)),
                            pl.BlockSpec((None, CONV_W, tc), lambda i, j: (layer, 0, j))],
        out_specs=[pl.BlockSpec((tm, tc), lambda i, j: (i, j)),
                   pl.BlockSpec((nseq, SUBLANE, tc), lambda i, j: (i, 0, j))],
        out_shape=[jax.ShapeDtypeStruct((M, width), out_dtype),
                   jax.ShapeDtypeStruct((n_st, SUBLANE, width), F32)],
        scratch_shapes=[pltpu.VMEM((nc, SUBLANE, tc), F32)],
        compiler_params=_cparams(("arbitrary", "arbitrary")), name="conv_branch",
    )(z, z, z, prev8, conv_w)
    return out, st.reshape(B, n_st // B, SUBLANE, width)[:, -1]


def _ffn_up_tile(i, j, h_ref, wu_ref, wg_ref, prev_ref, cw_ref, cb_ref, act_ref, st_ref, carry_scr, seq_len):
    tm, tn = act_ref.shape
    for c, w in _col_chunks(tn):
        cols = slice(c, c + w)
        wu = wu_ref[:, cols].astype(BF16)
        wg = wg_ref[:, cols].astype(BF16)
        taps = _ConvTaps(i, j, cols, tm, prev_ref, st_ref, carry_scr, seq_len)
        for r, n in _row_chunks(tm):
            h = h_ref[r:r + n, :]
            g = _dot(h, wg)
            g1, g2 = taps.chunk(g, r)
            gc = g2 * cw_ref[0:1, cols] + g1 * cw_ref[1:2, cols] + g * cw_ref[2:3, cols] + cb_ref[:, cols]
            act_ref[r:r + n, cols] = (gc * _sigmoid(gc) * _dot(h, wu)).astype(act_ref.dtype)
        taps.finish()


def _ffn_up_kernel(h_ref, h2_ref, wu_ref, wg_ref, prev_ref, prev2_ref, cw_ref, cb_ref,
                   act_ref, st_ref, act2_ref, st2_ref, carry_scr, *, seq_len, seq_len2):
    i, j = pl.program_id(0), pl.program_id(1)
    _ffn_up_tile(i, j, h_ref, wu_ref, wg_ref, prev_ref, cw_ref, cb_ref, act_ref, st_ref, carry_scr, seq_len)

    @pl.when(i == 0)
    def _():
        _ffn_up_tile(0, j, h2_ref, wu_ref, wg_ref, prev2_ref, cw_ref, cb_ref, act2_ref, st2_ref, None, seq_len2)


def ffn_first_half(h, h2, w_up, layer, col0, width, prev8, prev8_2, conv_w, conv_b, B, T, B2, T2, out_dtype,
                   tm=2048, tn=512):
    M, K = h.shape
    M2 = h2.shape[0]
    dff = w_up.shape[2] // 2
    tn = min(tn, width)
    assert col0 % LANE == 0 and dff % LANE == 0 and width % tn == 0 and M2 == B2 * T2 and T2 % SUBLANE == 0
    nj = width // tn
    tm, nseq, seq_of = _row_blocking(B, T, tm)
    n_st = (M // tm) * nseq

    def col(j, base=col0):
        return pl.multiple_of(base + j * tn, LANE)

    def parked(i, j):
        return jnp.where(i == 0, j, nj - 1)

    def win(rows):
        return pl.BlockSpec((pl.Element(rows), pl.Element(tn)), lambda i, j: (0, col(j)))

    def state_win(n, first, tile):
        return pl.BlockSpec((pl.Element(n), pl.Element(SUBLANE), pl.Element(tn)),
                            lambda i, j: (first(i), 0, col(tile(i, j))))

    act, st, act2, st2 = pl.pallas_call(
        functools.partial(_ffn_up_kernel, seq_len=T, seq_len2=T2), grid=(M // tm, nj),
        in_specs=[pl.BlockSpec((tm, K), lambda i, j: (i, 0)),
                  pl.BlockSpec((M2, K), lambda i, j: (0, 0)),
                  pl.BlockSpec((None, pl.Element(K), pl.Element(tn)), lambda i, j: (layer, 0, col(j))),
                  pl.BlockSpec((None, pl.Element(K), pl.Element(tn)), lambda i, j: (layer, 0, col(j, dff + col0))),
                  state_win(nseq, lambda i: seq_of(i) * nseq, lambda i, j: j),
                  state_win(B2, lambda i: 0, parked),
                  win(CONV_W), win(1)],
        out_specs=[pl.BlockSpec((tm, tn), lambda i, j: (i, j)),
                   pl.BlockSpec((nseq, SUBLANE, tn), lambda i, j: (i, 0, j)),
                   _second_group_tile(M2, tn, nj),
                   pl.BlockSpec((B2, SUBLANE, tn), lambda i, j: (0, 0, parked(i, j)))],
        out_shape=[jax.ShapeDtypeStruct((M, width), out_dtype),
                   jax.ShapeDtypeStruct((n_st, SUBLANE, width), F32),
                   jax.ShapeDtypeStruct((M2, width), out_dtype),
                   jax.ShapeDtypeStruct((B2, SUBLANE, width), F32)],
        scratch_shapes=[pltpu.VMEM((nj, SUBLANE, tn), F32)],
        compiler_params=_cparams(("arbitrary", "arbitrary")), name="ffn_up",
    )(h, h2, w_up, w_up, prev8, prev8_2, conv_w, conv_b)
    return (act, st.reshape(B, n_st // B, SUBLANE, width)[:, -1]), (act2, st2)


def _merge_kernel(a_ref, b_ref, c_ref, wb_ref, g0_ref, g1_ref, g2_ref, o_ref):
    xs = (a_ref, b_ref, c_ref)
    gs = (g0_ref, g1_ref, g2_ref)
    for c, w in _col_chunks(o_ref.shape[1]):
        cols = slice(c, c + w)
        ws = [wb_ref[n, :, cols].astype(BF16) for n in range(3)]
        for r, m in _row_chunks(o_ref.shape[0]):
            rows = slice(r, r + m)

            def term(n):
                return _sigmoid(gs[n][rows, cols].astype(F32)) * _dot(xs[n][rows, :].astype(BF16), ws[n])

            o_ref[rows, cols] = ((term(0) + term(1)) + term(2)).astype(o_ref.dtype)


def merge(br_a, br_b, br_c, wb, layer, zg, out_dtype, tm=1024, tn=512):
    M, mix = br_a.shape
    d = wb.shape[3]
    tm = _tile(M, tm, SUBLANE)
    tn = _tile(d, tn, LANE)
    br_spec = pl.BlockSpec((tm, mix), lambda i, j: (i, 0))
    g_specs = [pl.BlockSpec((tm, tn), (lambda i, j, o=n * d // tn: (i, o + j))) for n in range(3)]
    return pl.pallas_call(
        _merge_kernel, grid=(M // tm, d // tn),
        in_specs=[br_spec, br_spec, br_spec,
                  pl.BlockSpec((None, 3, mix, tn), lambda i, j: (layer, 0, 0, j))] + g_specs,
        out_specs=pl.BlockSpec((tm, tn), lambda i, j: (i, j)),
        out_shape=jax.ShapeDtypeStruct((M, d), out_dtype),
        compiler_params=_cparams(("parallel", "arbitrary")), name="merge",
    )(br_a, br_b, br_c, wb, zg, zg, zg)


def _xattn_kernel(q_ref, k_ref, v_ref, o_ref, *, heads, hd, scale):
    md = _mxu_dtype(q_ref.shape[0])
    for h in range(heads):
        sl = slice(h * hd, (h + 1) * hd)
        s = _dot_nt(q_ref[:, sl].astype(md), k_ref[:, sl].astype(md)) * scale
        e = jnp.exp(s - jnp.max(s, axis=-1, keepdims=True))
        den = jnp.sum(e, axis=-1, keepdims=True)
        o_ref[:, sl] = (_dot(e.astype(md), v_ref[:, sl].astype(md)) / den).astype(o_ref.dtype)


def xattn(q3, mem_k, mem_v, layer, heads, out_dtype, tq=512):
    B, T, w = q3.shape
    nm = mem_k.shape[-2]
    hd = w // heads
    tq = _tile(T, tq, SUBLANE)
    kern = functools.partial(_xattn_kernel, heads=heads, hd=hd, scale=hd ** -0.5)
    if mem_k.ndim == 4:
        m_spec = pl.BlockSpec((None, None, nm, w), lambda b, t: (layer, b, 0, 0))
    else:
        m_spec = pl.BlockSpec((None, nm, w), lambda b, t: (b, 0, 0))
    return pl.pallas_call(
        kern, grid=(B, T // tq),
        in_specs=[pl.BlockSpec((None, tq, w), lambda b, t: (b, t, 0)), m_spec, m_spec],
        out_specs=pl.BlockSpec((None, tq, w), lambda b, t: (b, t, 0)),
        out_shape=jax.ShapeDtypeStruct((B, T, w), out_dtype),
        compiler_params=_cparams(("parallel", "arbitrary")), name="xattn",
    )(q3, mem_k, mem_v)


def _xattn_decode_kernel(q_ref, k_ref, v_ref, o_ref, *, heads, hd, scale):
    bb, T, _ = q_ref.shape
    nm = k_ref.shape[1] // heads
    rows = max(T, 16)
    pairs = [(s, h) for s in range(bb) for h in range(heads)]
    scs = [_dot_nt(_pad_rows(q_ref[s, :, h * hd:(h + 1) * hd].astype(F32), rows).astype(BF16),
                   k_ref[s, pl.ds(h, nm, stride=heads), :].astype(BF16)) * scale for s, h in pairs]
    es = [jnp.exp(sc - jnp.max(sc, axis=-1, keepdims=True)) for sc in scs]
    dens = [jnp.sum(e, axis=-1, keepdims=True) for e in es]
    for (s, h), e, den in zip(pairs, es, dens):
        v = v_ref[s, pl.ds(h, nm, stride=heads), :].astype(BF16)
        o_ref[s, :, h * hd:(h + 1) * hd] = (_dot(e.astype(BF16), v) / den)[:T].astype(o_ref.dtype)


def xattn_decode(q3, mem_k, mem_v, layer, heads, out_dtype, seqs_per_step=8):
    B, T, w = q3.shape
    hd = w // heads
    bb = _tile(B, seqs_per_step, 1)
    kern = functools.partial(_xattn_decode_kernel, heads=heads, hd=hd, scale=hd ** -0.5)
    m_spec = pl.BlockSpec((None, bb) + mem_k.shape[2:], lambda b: (layer, b, 0, 0))
    return pl.pallas_call(
        kern, grid=(B // bb,),
        in_specs=[pl.BlockSpec((bb, T, w), lambda b: (b, 0, 0)), m_spec, m_spec],
        out_specs=pl.BlockSpec((bb, T, w), lambda b: (b, 0, 0)),
        out_shape=jax.ShapeDtypeStruct((B, T, w), out_dtype),
        compiler_params=_cparams(("parallel",)), name="xattn_decode",
    )(q3, mem_k, mem_v)


def _round_up(n, m):
    return -(-n // m) * m


def kernel(x_prompt, x_sample, state_gla, cache_swa_k, cache_swa_v, state_conv, state_ffn, cache_mem_k, cache_mem_v, mem_prompt, norm_mix, w_in, gla_gate_up, gla_gate_b, gla_norm, swa_sinks, rel_bias, conv_w, w_branch, w_out, norm_x, wx_q, wx_k, wx_v, wx_o, norm_ffn, ffn_up, ffn_conv_w, ffn_conv_b, ffn_down, norm_final):
    Bp, Tp, D = x_prompt.shape
    Bs, Ts, _ = x_sample.shape
    depth = w_in.shape[0]
    gla_heads, gla_dk, gla_dv = state_gla.shape[2:]
    swa_kv, swa_hd = cache_swa_k.shape[3:]
    swa_heads = swa_sinks.shape[1]
    x_heads, x_hd = cache_mem_k.shape[3:]
    n_mem = mem_prompt.shape[1]
    mix = conv_w.shape[2]
    dff = ffn_down.shape[1]
    rank = gla_gate_up.shape[1]
    gqk = gla_heads * gla_dk
    sqw, skw = swa_heads * swa_hd, swa_kv * swa_hd
    win = cache_swa_k.shape[2]
    xw = x_heads * x_hd
    assert win == WINDOW and mix == gla_heads * gla_dv == sqw

    w_in_t = jnp.swapaxes(w_in, 1, 2)
    a_w = 2 * gqk + 2 * mix
    s_w = sqw + 2 * skw
    o_glr, o_swa = a_w, a_w + rank
    o_conv = o_swa + s_w
    o_gates = o_conv + 3 * mix
    ffn_main = dff // 512 * 512
    ffn_parts = [(0, ffn_main)] + ([(ffn_main, dff - ffn_main)] if dff > ffn_main else [])

    bias_p = _swa_bias_table(rel_bias, min(Tp, WINDOW), WINDOW, True)
    table_s = _swa_decode_table(rel_bias, swa_sinks, Ts, win, swa_kv)
    cache_k4 = cache_swa_k.reshape(depth, Bs, win, skw)
    cache_v4 = cache_swa_v.reshape(depth, Bs, win, skw)
    cmem_k4 = cache_mem_k.reshape(depth, Bs, n_mem * x_heads, x_hd)
    cmem_v4 = cache_mem_v.reshape(depth, Bs, n_mem * x_heads, x_hd)
    mem2 = mem_prompt.reshape(Bp * n_mem, D)

    def first8(st):
        return jnp.pad(st, ((0, 0), (SUBLANE - (CONV_W - 1), 0), (0, 0)))

    def in_proj(i, xa, xb, dt_a, dt_b):
        ha, hb = rmsnorm(xa, norm_mix[i], BF16), rmsnorm(xb, norm_mix[i], BF16)
        zg_a, lg_a, zg_b, lg_b = matmul_nt_gate(ha, hb, w_in_t, i, a_w, o_glr, gla_gate_up, gla_gate_b, dt_a, dt_b,
                                                name="in_proj_gla")
        zs_a, zs_b = matmul_nt(ha, hb, w_in_t, i, o_swa, s_w, dt_a, dt_b, tn=s_w, name="in_proj_swa")
        zc_a, zc_b = matmul_nt(ha, hb, w_in_t, i, o_conv, 3 * mix, dt_a, dt_b, name="in_proj_conv")
        zt_a, zt_b = matmul_nt(ha, hb, w_in_t, i, o_gates, 3 * D, dt_a, dt_b, name="in_proj_gates")
        return (zg_a, lg_a, zs_a, zc_a, zt_a), (zg_b, lg_b, zs_b, zc_b, zt_b)

    def trunk_layer(i, x, zs_all, B, T, decode, s0, conv_prev8, mem_k, mem_v, act_dtype):
        M = B * T
        zg, lg, zs, zc, zt = zs_all
        zs3 = zs.reshape(B, T, s_w)
        br_a, gla_st = gla(zg.reshape(B, T, a_w), lg.reshape(B, T, gqk), gla_norm, s0, i,
                           gla_heads, gla_dk, gla_dv, act_dtype, seqs_per_step=4 if decode else 1)
        if decode:
            br_b = swa_decode(zs3, cache_k4, cache_v4, table_s, i, swa_kv, swa_hd, act_dtype)
        else:
            br_b = swa(zs3, None, None, swa_sinks, bias_p, i, swa_kv, swa_hd, min(T, WINDOW), act_dtype)
        br_c, conv_st = conv_branch(zc, conv_prev8, conv_w, i, B, T, act_dtype)
        merged = merge(br_a.reshape(M, mix), br_b.reshape(M, mix), br_c, w_branch, i, zt, BF16)
        x, hx = res_matmul_norm(merged, w_out, x, norm_x[i], i, name="out_proj")
        q = matmul(hx, wx_q, i, act_dtype, name="xq_proj")
        if decode:
            xo = xattn_decode(q.reshape(B, T, xw), mem_k, mem_v, i, x_heads, act_dtype)
        else:
            xo = xattn(q.reshape(B, T, xw), mem_k, mem_v, i, x_heads, act_dtype)
        x, hf = res_matmul_norm(xo.reshape(M, xw), wx_o, x, norm_ffn[i], i, name="xo_proj")
        k_new = zs3[:, T - min(T, WINDOW):, sqw:sqw + skw].astype(F32)
        v_new = zs3[:, T - min(T, WINDOW):, sqw + skw:].astype(F32)
        return x, hf, (gla_st, k_new, v_new, conv_st[:, SUBLANE - (CONV_W - 1):])

    xp = x_prompt.reshape(Bp * Tp, D)
    xs = x_sample.reshape(Bs * Ts, D)
    act_p = BF16
    act_s = BF16 if Ts % 16 == 0 else F32
    st_p, st_s = [], []
    for i in range(depth):
        mem_k = matmul(mem2, wx_k, i, F32, name="mem_k").reshape(Bp, n_mem, xw)
        mem_v = matmul(mem2, wx_v, i, F32, name="mem_v").reshape(Bp, n_mem, xw)
        z_p, z_s = in_proj(i, xp, xs, act_p, act_s)
        xp, hf_p, sp = trunk_layer(i, xp, z_p, Bp, Tp, False, None, jnp.zeros((Bp, SUBLANE, mix), F32),
                                   mem_k, mem_v, act_p)
        xs, hf_s, ss = trunk_layer(i, xs, z_s, Bs, Ts, True, state_gla, first8(state_conv[i]),
                                   cmem_k4, cmem_v4, act_s)
        ffn = [ffn_first_half(hf_p, hf_s, ffn_up, i, c0, wd, jnp.zeros((Bp, SUBLANE, dff), F32),
                              first8(state_ffn[i]), ffn_conv_w[i], ffn_conv_b[i].reshape(1, dff),
                              Bp, Tp, Bs, Ts, BF16) for c0, wd in ffn_parts]
        xp, xs = res_matmul([f[0][0] for f in ffn], ffn_down, xp, i, a2_parts=[f[1][0] for f in ffn], res2=xs,
                            tm=2048, tn=256, name="ffn_down")
        sp += (jnp.concatenate([f[0][1] for f in ffn], axis=-1)[:, SUBLANE - (CONV_W - 1):],)
        ss += (jnp.concatenate([f[1][1] for f in ffn], axis=-1)[:, SUBLANE - (CONV_W - 1):],)
        st_p.append(dict(gla=sp[0], k=sp[1].reshape(Bp, WINDOW, swa_kv, swa_hd),
                         v=sp[2].reshape(Bp, WINDOW, swa_kv, swa_hd), conv=sp[3], ffn=sp[4],
                         mem_k=mem_k.reshape(Bp, n_mem, x_heads, x_hd),
                         mem_v=mem_v.reshape(Bp, n_mem, x_heads, x_hd)))
        kc = jnp.concatenate([cache_k4[i], ss[1]], axis=1)[:, Ts:]
        vc = jnp.concatenate([cache_v4[i], ss[2]], axis=1)[:, Ts:]
        st_s.append(dict(gla=ss[0], k=kc.reshape(Bs, win, swa_kv, swa_hd), v=vc.reshape(Bs, win, swa_kv, swa_hd),
                         conv=ss[3], ffn=ss[4]))
    y_prompt = rmsnorm(xp, norm_final, F32).reshape(Bp, Tp, D)
    y_sample = rmsnorm(xs, norm_final, F32).reshape(Bs, Ts, D)

    def stack(lst, key):
        return jnp.stack([s[key] for s in lst])

    return (y_prompt, y_sample, stack(st_p, "gla"), stack(st_s, "gla"), stack(st_p, "k"), stack(st_p, "v"),
            stack(st_s, "k"), stack(st_s, "v"), stack(st_p, "conv"), stack(st_s, "conv"),
            stack(st_p, "ffn"), stack(st_s, "ffn"), stack(st_p, "mem_k"), stack(st_p, "mem_v"))
```

```python
import functools
import math

import jax
import jax.numpy as jnp
from jax import lax
from jax.experimental import pallas as pl
from jax.experimental.pallas import tpu as pltpu

F32 = jnp.float32
BF16 = jnp.bfloat16

EPS = 1e-6
GLA_TAU = 16.0
GLA_CHUNK = 64
GLA_SUB = 8
WINDOW = 128
N_BUCKETS = 32
MAX_DIST = 128
CONV_W = 3
NEG = -1e30

LANE = 128
SUBLANE = 8
MXU_COLS = 256
MXU_ROWS = 512
VMEM_LIMIT = 56 * 1024 * 1024


def _cparams(sem):
    return pltpu.CompilerParams(dimension_semantics=sem, vmem_limit_bytes=VMEM_LIMIT)


def _tile(n, pref, unit):
    if n <= pref:
        return n
    t = (pref // unit) * unit
    while t > unit and n % t:
        t -= unit
    assert n % t == 0, (n, pref, unit)
    return t


def _mxu_dtype(rows):
    return BF16 if rows % 16 == 0 else F32


def _sigmoid(x):
    return 1.0 / (1.0 + jnp.exp(-x))


def _dot(a, b):
    return jnp.dot(a, b, preferred_element_type=F32)


def _dot_nt(a, b):
    return lax.dot_general(a, b, (((1,), (1,)), ((), ())), preferred_element_type=F32)


def _col_chunks(n):
    if n % LANE:
        return [(0, n)]
    return [(c, min(MXU_COLS, n - c)) for c in range(0, n, MXU_COLS)]


def _row_chunks(n):
    m = MXU_ROWS if n % MXU_ROWS == 0 else n
    return [(r, m) for r in range(0, n, m)]


def _pad_rows(x, rows):
    if x.shape[0] == rows:
        return x
    return jnp.concatenate([x, jnp.zeros((rows - x.shape[0], x.shape[1]), x.dtype)], axis=0)


def _norm_kernel(x_ref, g_ref, o_ref):
    x = x_ref[...]
    ms = jnp.mean(x * x, axis=-1, keepdims=True)
    o_ref[...] = (x * lax.rsqrt(ms + EPS) * g_ref[...]).astype(o_ref.dtype)


def rmsnorm(x, gain, out_dtype, tm=512):
    M, K = x.shape
    tm = _tile(M, tm, SUBLANE)
    return pl.pallas_call(
        _norm_kernel, grid=(M // tm,),
        in_specs=[pl.BlockSpec((tm, K), lambda i: (i, 0)), pl.BlockSpec((1, K), lambda i: (0, 0))],
        out_specs=pl.BlockSpec((tm, K), lambda i: (i, 0)),
        out_shape=jax.ShapeDtypeStruct((M, K), out_dtype),
        compiler_params=_cparams(("parallel",)), name="rmsnorm")(x, gain.reshape(1, K))


def _mm_kernel(a_ref, w_ref, o_ref):
    for c, w in _col_chunks(o_ref.shape[1]):
        wc = w_ref[:, c:c + w].astype(BF16)
        for r, n in _row_chunks(o_ref.shape[0]):
            o_ref[r:r + n, c:c + w] = _dot(a_ref[r:r + n, :].astype(BF16), wc).astype(o_ref.dtype)


def _mm_nt_kernel(a_ref, wt_ref, o_ref):
    for c, w in _col_chunks(o_ref.shape[1]):
        wc = wt_ref[c:c + w, :].astype(BF16)
        for r, n in _row_chunks(o_ref.shape[0]):
            o_ref[r:r + n, c:c + w] = _dot_nt(a_ref[r:r + n, :], wc).astype(o_ref.dtype)


def _log_decay(a_ref, wglr_ref, up_ref, gb_ref, lg_ref):
    wg = _pad_rows(wglr_ref[...], LANE).astype(BF16)
    up = _pad_rows(up_ref[...], LANE).astype(BF16)
    for r, n in _row_chunks(lg_ref.shape[0]):
        glr = _dot_nt(a_ref[r:r + n, :], wg)
        z = _dot(glr.astype(BF16), up) + gb_ref[...]
        lg_ref[r:r + n, :] = -(jnp.maximum(-z, 0.0) + jnp.log1p(jnp.exp(-jnp.abs(z)))) / GLA_TAU


def _mm_nt_gate_kernel(a_ref, wt_ref, wglr_ref, up_ref, gb_ref, o_ref, lg_ref):
    _mm_nt_kernel(a_ref, wt_ref, o_ref)

    @pl.when(pl.program_id(1) == 0)
    def _():
        _log_decay(a_ref, wglr_ref, up_ref, gb_ref, lg_ref)


def _mm_nt2_kernel(a_ref, a2_ref, wt_ref, o_ref, o2_ref):
    _mm_nt_kernel(a_ref, wt_ref, o_ref)

    @pl.when(pl.program_id(0) == 0)
    def _():
        _mm_nt_kernel(a2_ref, wt_ref, o2_ref)


def _mm_nt_gate2_kernel(a_ref, a2_ref, wt_ref, wglr_ref, up_ref, gb_ref, o_ref, lg_ref, o2_ref, lg2_ref):
    _mm_nt_gate_kernel(a_ref, wt_ref, wglr_ref, up_ref, gb_ref, o_ref, lg_ref)

    @pl.when(pl.program_id(0) == 0)
    def _():
        _mm_nt_kernel(a2_ref, wt_ref, o2_ref)

    @pl.when((pl.program_id(0) == 0) & (pl.program_id(1) == 0))
    def _():
        _log_decay(a2_ref, wglr_ref, up_ref, gb_ref, lg2_ref)


def _res_mm_body(a_refs, w_refs, r_ref, o_ref):
    for c, w in _col_chunks(o_ref.shape[1]):
        wcs = [w_ref[:, c:c + w].astype(BF16) for w_ref in w_refs]
        for r, n in _row_chunks(o_ref.shape[0]):
            acc = r_ref[r:r + n, c:c + w]
            for a_ref, wc in zip(a_refs, wcs):
                acc = acc + _dot(a_ref[r:r + n, :].astype(BF16), wc)
            o_ref[r:r + n, c:c + w] = acc


def _res_mm_kernel(*refs, n_parts, two_groups):
    a_refs, w_refs = refs[0:2 * n_parts:2], refs[1:2 * n_parts:2]
    if not two_groups:
        _res_mm_body(a_refs, w_refs, refs[-2], refs[-1])
        return
    _res_mm_body(a_refs, w_refs, refs[2 * n_parts], refs[-2])

    @pl.when(pl.program_id(0) == 0)
    def _():
        _res_mm_body(refs[2 * n_parts + 1:3 * n_parts + 1], w_refs, refs[3 * n_parts + 1], refs[-1])


def matmul(a, w, layer, out_dtype, tm=2048, tn=512, name="mm"):
    M, K = a.shape
    N = w.shape[-1]
    tm = _tile(M, tm, SUBLANE)
    tn = _tile(N, tn, LANE)
    return pl.pallas_call(
        _mm_kernel, grid=(M // tm, N // tn),
        in_specs=[pl.BlockSpec((tm, K), lambda i, j: (i, 0)),
                  pl.BlockSpec((None, K, tn), lambda i, j: (layer, 0, j))],
        out_specs=pl.BlockSpec((tm, tn), lambda i, j: (i, j)),
        out_shape=jax.ShapeDtypeStruct((M, N), out_dtype),
        compiler_params=_cparams(("parallel", "arbitrary")), name=name)(a, w)


def _wt_spec(layer, row0, tn, K):
    return pl.BlockSpec((None, pl.Element(tn), pl.Element(K)),
                        lambda i, j: (layer, pl.multiple_of(row0 + j * tn, SUBLANE), 0))


def _second_group_tile(rows, tn, nj):
    return pl.BlockSpec((rows, tn), lambda i, j: (0, jnp.where(i == 0, j, nj - 1)))


def matmul_nt(a, a2, wt, layer, row0, n, out_dtype, out2_dtype, tm=2048, tn=512, name="mm_nt"):
    M, K = a.shape
    M2 = a2.shape[0]
    assert row0 % SUBLANE == 0
    tm = _tile(M, tm, SUBLANE)
    tn = _tile(n, tn, LANE)
    nj = n // tn
    return pl.pallas_call(
        _mm_nt2_kernel, grid=(M // tm, nj),
        in_specs=[pl.BlockSpec((tm, K), lambda i, j: (i, 0)), pl.BlockSpec((M2, K), lambda i, j: (0, 0)),
                  _wt_spec(layer, row0, tn, K)],
        out_specs=[pl.BlockSpec((tm, tn), lambda i, j: (i, j)), _second_group_tile(M2, tn, nj)],
        out_shape=[jax.ShapeDtypeStruct((M, n), out_dtype), jax.ShapeDtypeStruct((M2, n), out2_dtype)],
        compiler_params=_cparams(("arbitrary", "arbitrary")), name=name)(a, a2, wt)


def matmul_nt_gate(a, a2, wt, layer, n, glr_row0, gate_up, gate_b, out_dtype, out2_dtype, tm=2048, tn=512,
                   name="mm_nt_gate"):
    M, K = a.shape
    M2 = a2.shape[0]
    depth, rank, gw = gate_up.shape
    assert glr_row0 % SUBLANE == 0 and rank % SUBLANE == 0
    tm = _tile(M, tm, SUBLANE)
    tn = _tile(n, tn, LANE)
    nj = n // tn
    return pl.pallas_call(
        _mm_nt_gate2_kernel, grid=(M // tm, nj),
        in_specs=[pl.BlockSpec((tm, K), lambda i, j: (i, 0)), pl.BlockSpec((M2, K), lambda i, j: (0, 0)),
                  _wt_spec(layer, 0, tn, K),
                  pl.BlockSpec((None, pl.Element(rank), pl.Element(K)), lambda i, j: (layer, glr_row0, 0)),
                  pl.BlockSpec((None, rank, gw), lambda i, j: (layer, 0, 0)),
                  pl.BlockSpec((None, 1, gw), lambda i, j: (layer, 0, 0))],
        out_specs=[pl.BlockSpec((tm, tn), lambda i, j: (i, j)), pl.BlockSpec((tm, gw), lambda i, j: (i, 0)),
                   _second_group_tile(M2, tn, nj), pl.BlockSpec((M2, gw), lambda i, j: (0, 0))],
        out_shape=[jax.ShapeDtypeStruct((M, n), out_dtype), jax.ShapeDtypeStruct((M, gw), F32),
                   jax.ShapeDtypeStruct((M2, n), out2_dtype), jax.ShapeDtypeStruct((M2, gw), F32)],
        compiler_params=_cparams(("arbitrary", "arbitrary")), name=name,
    )(a, a2, wt, wt, gate_up, gate_b.reshape(depth, 1, gw))


def _res_mm_norm_kernel(a_ref, w_ref, r_ref, g_ref, o_ref, h_ref, wb_scr):
    @pl.when(pl.program_id(0) == 0)
    def _():
        for c, w in _col_chunks(wb_scr.shape[1]):
            wb_scr[:, c:c + w] = w_ref[:, c:c + w].astype(BF16)

    n_cols = o_ref.shape[1]
    for r, n in _row_chunks(o_ref.shape[0]):
        rows = slice(r, r + n)
        a = a_ref[rows, :].astype(BF16)
        ss = jnp.zeros((n, 1), F32)
        for c, w in _col_chunks(n_cols):
            x = r_ref[rows, c:c + w] + _dot(a, wb_scr[:, c:c + w])
            o_ref[rows, c:c + w] = x
            ss = ss + jnp.sum(x * x, axis=-1, keepdims=True)
        scale = lax.rsqrt(ss / n_cols + EPS)
        for c, w in _col_chunks(n_cols):
            h_ref[rows, c:c + w] = (o_ref[rows, c:c + w] * scale * g_ref[:, c:c + w]).astype(h_ref.dtype)


def res_matmul_norm(a, w, res, gain, layer, tm=512, name="res_mm_norm"):
    M, K = a.shape
    N = w.shape[-1]
    tm = _tile(M, tm, SUBLANE)
    once = pl.Buffered(1)
    return pl.pallas_call(
        _res_mm_norm_kernel, grid=(M // tm,),
        in_specs=[pl.BlockSpec((tm, K), lambda i: (i, 0)),
                  pl.BlockSpec((None, K, N), lambda i: (layer, 0, 0), pipeline_mode=once),
                  pl.BlockSpec((tm, N), lambda i: (i, 0)),
                  pl.BlockSpec((1, N), lambda i: (0, 0))],
        out_specs=[pl.BlockSpec((tm, N), lambda i: (i, 0)), pl.BlockSpec((tm, N), lambda i: (i, 0))],
        out_shape=[jax.ShapeDtypeStruct((M, N), F32), jax.ShapeDtypeStruct((M, N), BF16)],
        scratch_shapes=[pltpu.VMEM((K, N), BF16)],
        compiler_params=_cparams(("arbitrary",)), name=name)(a, w, res, gain.reshape(1, N))


def res_matmul(a_parts, w, res, layer, a2_parts=None, res2=None, tm=2048, tn=512, name="res_mm"):
    M = a_parts[0].shape[0]
    N = w.shape[-1]
    tm = _tile(M, tm, SUBLANE)
    tn = _tile(N, tn, LANE)
    nj = N // tn
    two = a2_parts is not None
    in_specs, args, row0 = [], [], 0
    for a in a_parts:
        kp = a.shape[1]
        assert row0 % SUBLANE == 0
        in_specs += [pl.BlockSpec((tm, kp), lambda i, j: (i, 0), pipeline_mode=pl.Buffered(1)),
                     pl.BlockSpec((None, pl.Element(kp), pl.Element(tn)),
                                  lambda i, j, r0=row0: (layer, r0, pl.multiple_of(j * tn, LANE)))]
        args += [a, w]
        row0 += kp
    assert row0 == w.shape[1]
    in_specs.append(pl.BlockSpec((tm, tn), lambda i, j: (i, j)))
    args.append(res)
    out_specs = [pl.BlockSpec((tm, tn), lambda i, j: (i, j))]
    out_shape = [jax.ShapeDtypeStruct((M, N), F32)]
    if two:
        M2 = res2.shape[0]
        in_specs += [pl.BlockSpec((M2, a.shape[1]), lambda i, j: (0, 0)) for a in a2_parts]
        in_specs.append(_second_group_tile(M2, tn, nj))
        args += list(a2_parts) + [res2]
        out_specs.append(_second_group_tile(M2, tn, nj))
        out_shape.append(jax.ShapeDtypeStruct((M2, N), F32))
    out = pl.pallas_call(
        functools.partial(_res_mm_kernel, n_parts=len(a_parts), two_groups=two), grid=(M // tm, nj),
        in_specs=in_specs, out_specs=out_specs, out_shape=out_shape,
        compiler_params=_cparams(("arbitrary", "arbitrary")), name=name)(*args)
    return out if two else out[0]


def _gla_chunk(qs, ks, vs, lgs, sts, md, lp, consts):
    row, rsub_l, lsub, ones = consts
    c = min(GLA_SUB, lp)
    dk = qs[0].shape[1]
    nblk = lp // c
    probs = range(len(qs))
    bs = list(lgs)
    sh = 1
    while sh < lp:
        bs = [b + jnp.where(row >= sh, pltpu.roll(b, sh, 0), 0.0) for b in bs]
        sh *= 2
    os_ = [_dot_nt((qs[i] * jnp.exp(bs[i])).astype(md), sts[i].astype(md)) for i in probs]
    rsums = []
    for q, k, b in zip(qs, ks, bs):
        ws = []
        for i in range(nblk):
            qb, kb, bb = (x[i * c:(i + 1) * c] for x in (q, k, b))
            ws += [qb * kb[s:s + 1, :] * jnp.exp(bb - bb[s:s + 1, :]) for s in range(c)]
        rsums.append(_dot(jnp.concatenate(ws, axis=0).astype(md), ones))
    a_rows = []
    for rsum in rsums:
        rows_p = []
        for i in range(nblk):
            a_i = jnp.zeros((c, LANE), F32)
            for s in range(c):
                a_i = jnp.where(lsub == i * c + s, rsum[(i * c + s) * c:(i * c + s + 1) * c], a_i)
            rows_p.append(jnp.where(lsub <= i * c + rsub_l, a_i, 0.0))
        a_rows.append(rows_p)
    n = c
    while n < lp:
        for p in range(0, lp, 2 * n):
            offs = []
            for q, k, b in zip(qs, ks, bs):
                ref = b[p + n - 1:p + n, :]
                qt = q[p + n:p + 2 * n] * jnp.exp(b[p + n:p + 2 * n] - ref)
                kt = k[p:p + n] * jnp.exp(ref - b[p:p + n])
                parts = [jnp.zeros((p, dk), F32)] if p else []
                ktp = jnp.concatenate(parts + [kt, jnp.zeros((LANE - p - n, dk), F32)], axis=0)
                offs.append(_dot_nt(_pad_rows(qt, max(n, 16)).astype(md), ktp.astype(md)))
            for rows_p, off in zip(a_rows, offs):
                for t in range(n // c):
                    rows_p[(p + n) // c + t] = rows_p[(p + n) // c + t] + off[t * c:(t + 1) * c]
        n *= 2
    vks = [_pad_rows(v, LANE) for v in vs]
    os_ = [os_[i] + _dot(jnp.concatenate(a_rows[i], axis=0).astype(md), vks[i].astype(md)) for i in probs]
    new_sts = []
    for i in probs:
        bend = bs[i][lp - 1:lp, :]
        kd = _pad_rows(ks[i] * jnp.exp(bend - bs[i]), LANE)
        new_sts.append(sts[i] * jnp.exp(bend) + _dot(vks[i].T.astype(md), kd.astype(md)))
    return os_, new_sts


def _gla_kernel(*refs, lin, lp, n_chunks, heads, scale, has_s0):
    if has_s0:
        q_ref, k_ref, v_ref, r_ref, lg_ref, gn_ref, s0_ref, o_ref, st_ref, s_scr = refs
    else:
        q_ref, k_ref, v_ref, r_ref, lg_ref, gn_ref, o_ref, st_ref, s_scr = refs
    t = pl.program_id(1)
    bb = q_ref.shape[0]
    dk = q_ref.shape[-1] // heads
    dv = v_ref.shape[-1] // heads
    c = min(GLA_SUB, lp)
    md = _mxu_dtype(lp)
    pairs = [(s, h) for s in range(bb) for h in range(heads)]

    @pl.when(t == 0)
    def _():
        for s, h in pairs:
            s_scr[s, h] = s0_ref[s, h].T if has_s0 else jnp.zeros((dv, dk), F32)

    row = lax.broadcasted_iota(jnp.int32, (lp, dk), 0)
    consts = (row, lax.broadcasted_iota(jnp.int32, (c, LANE), 0),
              lax.broadcasted_iota(jnp.int32, (c, LANE), 1), jnp.ones((dk, LANE), md))

    def chunk(ci, carry):
        rows = pl.ds(pl.multiple_of(ci * lin, lin), lin)
        kc = [slice(h * dk, (h + 1) * dk) for _, h in pairs]
        vc = [slice(h * dv, (h + 1) * dv) for _, h in pairs]
        qs = [_pad_rows(q_ref[s, rows, kc[i]].astype(F32), lp) * scale for i, (s, _) in enumerate(pairs)]
        ks = [_pad_rows(k_ref[s, rows, kc[i]].astype(F32), lp) for i, (s, _) in enumerate(pairs)]
        vs = [_pad_rows(v_ref[s, rows, vc[i]].astype(F32), lp) for i, (s, _) in enumerate(pairs)]
        lgs = [_pad_rows(lg_ref[s, rows, kc[i]], lp) for i, (s, _) in enumerate(pairs)]
        os_, sts = _gla_chunk(qs, ks, vs, lgs, [s_scr[s, h] for s, h in pairs], md, lp, consts)
        for i, (s, h) in enumerate(pairs):
            s_scr[s, h] = sts[i]
            o = os_[i][:lin]
            ms = jnp.mean(o * o, axis=-1, keepdims=True)
            y = o * lax.rsqrt(ms + EPS) * gn_ref[...]
            r = r_ref[s, rows, vc[i]].astype(F32)
            o_ref[s, rows, vc[i]] = (y * (r * _sigmoid(r))).astype(o_ref.dtype)
        return carry

    lax.fori_loop(0, n_chunks, chunk, 0)

    @pl.when(t == pl.num_programs(1) - 1)
    def _():
        for s, h in pairs:
            st_ref[s, h] = s_scr[s, h].T


def gla(zg, lg, gnorm, s0, layer, heads, dk, dv, out_dtype, seqs_per_step=1):
    B, T, _ = zg.shape
    depth = gnorm.shape[0]
    qw, vw = heads * dk, heads * dv
    assert vw % qw == 0
    lin = min(GLA_CHUNK, T)
    assert T % lin == 0
    lp = max(lin, 16)
    tt = _tile(T, 512, lin)
    bb = _tile(B, seqs_per_step, 1)
    in_specs = [pl.BlockSpec((bb, tt, qw), lambda b, t: (b, t, 0)),
                pl.BlockSpec((bb, tt, qw), lambda b, t: (b, t, 1)),
                pl.BlockSpec((bb, tt, vw), lambda b, t: (b, t, 2 * qw // vw)),
                pl.BlockSpec((bb, tt, vw), lambda b, t: (b, t, 2 * qw // vw + 1)),
                pl.BlockSpec((bb, tt, qw), lambda b, t: (b, t, 0)),
                pl.BlockSpec((None, 1, dv), lambda b, t: (layer, 0, 0))]
    args = [zg, zg, zg, zg, lg, gnorm.reshape(depth, 1, dv)]
    if s0 is not None:
        in_specs.append(pl.BlockSpec((None, bb, heads, dk, dv), lambda b, t: (layer, b, 0, 0, 0)))
        args.append(s0)
    kern = functools.partial(_gla_kernel, lin=lin, lp=lp, n_chunks=tt // lin, heads=heads, scale=dk ** -0.5,
                             has_s0=s0 is not None)
    return pl.pallas_call(
        kern, grid=(B // bb, T // tt), in_specs=in_specs,
        out_specs=[pl.BlockSpec((bb, tt, vw), lambda b, t: (b, t, 0)),
                   pl.BlockSpec((bb, heads, dk, dv), lambda b, t: (b, 0, 0, 0))],
        out_shape=[jax.ShapeDtypeStruct((B, T, vw), out_dtype),
                   jax.ShapeDtypeStruct((B, heads, dk, dv), F32)],
        scratch_shapes=[pltpu.VMEM((bb, heads, dv, dk), F32)],
        compiler_params=_cparams(("parallel", "arbitrary")), name="gla",
    )(*args)


def _swa_kernel(sink_ref, q_ref, kp_ref, kc_ref, vp_ref, vc_ref, bias_ref, o_ref, *,
                layer, heads, group, hd, scale):
    tq = q_ref.shape[0]
    md = _mxu_dtype(tq)
    k = jnp.concatenate([kp_ref[...].astype(md), kc_ref[...].astype(md)], axis=0)
    v = jnp.concatenate([vp_ref[...].astype(md), vc_ref[...].astype(md)], axis=0)
    hs = range(heads)
    kvc = [slice((h // group) * hd, (h // group + 1) * hd) for h in hs]
    ss = [_dot_nt((q_ref[:, h * hd:(h + 1) * hd].astype(F32) * scale).astype(md), k[:, kvc[h]]) + bias_ref[h]
          for h in hs]
    sinks = [sink_ref[layer, h] for h in hs]
    ms = [jnp.maximum(jnp.max(ss[h], axis=-1, keepdims=True), sinks[h]) for h in hs]
    es = [jnp.exp(ss[h] - ms[h]) for h in hs]
    dens = [jnp.sum(es[h], axis=-1, keepdims=True) + jnp.exp(sinks[h] - ms[h]) for h in hs]
    outs = [_dot(es[h].astype(md), v[:, kvc[h]]) / dens[h] for h in hs]
    per = LANE // hd
    for j in range(heads // per):
        o_ref[:, j * LANE:(j + 1) * LANE] = jnp.concatenate(
            outs[j * per:(j + 1) * per], axis=-1).astype(o_ref.dtype)


def _t5_bucket(dist):
    n = jnp.maximum(dist, 0)
    max_exact = N_BUCKETS // 2
    nf = jnp.maximum(n, 1).astype(F32)
    large = max_exact + (jnp.log(nf / max_exact) / math.log(MAX_DIST / max_exact)
                         * (N_BUCKETS - max_exact)).astype(jnp.int32)
    large = jnp.minimum(large, N_BUCKETS - 1)
    return jnp.where(n < max_exact, n, large)


def _swa_bias_table(rel_bias, tq, p, with_first):
    i = jnp.arange(tq)[:, None]
    j = jnp.arange(p + tq)[None, :]
    dist = p + i - j
    valid = (dist >= 0) & (dist <= WINDOW)
    onehot = (_t5_bucket(dist)[:, :, None] == jnp.arange(N_BUCKETS)).astype(F32)
    bias = jnp.einsum("ijb,bh->hij", onehot, rel_bias.astype(F32), precision=lax.Precision.HIGHEST)
    regular = jnp.where(valid[None], bias, NEG)
    if not with_first:
        return regular[None]
    return jnp.stack([jnp.where((valid & (j >= p))[None], bias, NEG), regular])


def swa(z3, prev_k, prev_v, sinks, bias, layer, kv_heads, hd, tq, out_dtype):
    B, T, _ = z3.shape
    heads = sinks.shape[1]
    qw, kw = heads * hd, kv_heads * hd
    assert qw % kw == 0
    ko, vo = qw // kw, qw // kw + 1
    nb = T // tq
    prompt = prev_k is None
    p = bias.shape[3] - tq
    if prompt:
        assert p == tq and bias.shape[0] == 2
        prev_specs = [pl.BlockSpec((None, p, kw), lambda b, n: (b, jnp.maximum(n - 1, 0), ko)),
                      pl.BlockSpec((None, p, kw), lambda b, n: (b, jnp.maximum(n - 1, 0), vo))]
        prev_k = prev_v = z3
        bias_spec = pl.BlockSpec((None,) + bias.shape[1:], lambda b, n: (jnp.minimum(n, 1), 0, 0, 0))
    else:
        assert nb == 1 and bias.shape[0] == 1
        prev_specs = [pl.BlockSpec((None, None, p, kw), lambda b, n: (layer, b, 0, 0))] * 2
        bias_spec = pl.BlockSpec((None,) + bias.shape[1:], lambda b, n: (0, 0, 0, 0))
    kern = functools.partial(_swa_kernel, layer=layer, heads=heads, group=heads // kv_heads, hd=hd,
                             scale=hd ** -0.5)
    return pl.pallas_call(
        kern, grid=(B, nb),
        in_specs=[pl.BlockSpec(memory_space=pltpu.SMEM),
                  pl.BlockSpec((None, tq, qw), lambda b, n: (b, n, 0)),
                  prev_specs[0],
                  pl.BlockSpec((None, tq, kw), lambda b, n: (b, n, ko)),
                  prev_specs[1],
                  pl.BlockSpec((None, tq, kw), lambda b, n: (b, n, vo)),
                  bias_spec],
        out_specs=pl.BlockSpec((None, tq, qw), lambda b, n: (b, n, 0)),
        out_shape=jax.ShapeDtypeStruct((B, T, qw), out_dtype),
        compiler_params=_cparams(("parallel", "arbitrary")), name="swa",
    )(sinks, z3, prev_k, z3, prev_v, z3, bias)


def _swa_decode_kernel(q_ref, kc_ref, kn_ref, vc_ref, vn_ref, tab_ref, o_ref, *, kv_heads, group, hd, scale):
    bb, T, _ = q_ref.shape
    kw = kc_ref.shape[-1]
    npad = tab_ref.shape[-1] - kc_ref.shape[1] - T
    md = _mxu_dtype(group * T)
    per = LANE // hd
    zpad = jnp.zeros((npad, kw), F32)
    pairs = [(s, g) for s in range(bb) for g in range(kv_heads)]
    ks = [jnp.concatenate([kc_ref[s].astype(F32), kn_ref[s].astype(F32), zpad], axis=0).astype(md) for s in range(bb)]
    vs = [jnp.concatenate([vc_ref[s].astype(F32), vn_ref[s].astype(F32), zpad], axis=0).astype(md) for s in range(bb)]
    scs = []
    for s, g in pairs:
        qs = jnp.concatenate([q_ref[s, :, (g * group + j) * hd:(g * group + j + 1) * hd].astype(F32)
                              for j in range(group)], axis=0) * scale
        scs.append(_dot_nt(qs.astype(md), ks[s][:, g * hd:(g + 1) * hd]) + tab_ref[g])
    es = [jnp.exp(sc - jnp.max(sc, axis=-1, keepdims=True)) for sc in scs]
    dens = [jnp.sum(e, axis=-1, keepdims=True) for e in es]
    os_ = [_dot(e.astype(md), vs[s][:, g * hd:(g + 1) * hd]) / den for (s, g), e, den in zip(pairs, es, dens)]
    for s in range(bb):
        outs = [os_[s * kv_heads + g][j * T:(j + 1) * T] for g in range(kv_heads) for j in range(group)]
        for j in range(len(outs) // per):
            o_ref[s, :, j * LANE:(j + 1) * LANE] = jnp.concatenate(
                outs[j * per:(j + 1) * per], axis=-1).astype(o_ref.dtype)


def _swa_decode_table(rel_bias, sinks, T, p, kv_heads):
    depth, heads = sinks.shape
    group = heads // kv_heads
    base = _swa_bias_table(rel_bias, T, p, False)[0].reshape(kv_heads, group * T, p + T)
    npad = _round_up(p + T + 1, 16) - (p + T)
    sink_col = jnp.repeat(sinks.astype(F32).reshape(depth, kv_heads, group), T, axis=2)[..., None]
    parts = [jnp.broadcast_to(base[None], (depth,) + base.shape), sink_col,
             jnp.full((depth, kv_heads, group * T, npad - 1), NEG, F32)]
    return jnp.concatenate(parts, axis=-1)


def swa_decode(z3, cache_k, cache_v, table, layer, kv_heads, hd, out_dtype, seqs_per_step=8):
    B, T, zw = z3.shape
    kw = kv_heads * hd
    qw = zw - 2 * kw
    p = cache_k.shape[2]
    bb = _tile(B, seqs_per_step, 1)
    kern = functools.partial(_swa_decode_kernel, kv_heads=kv_heads, group=qw // kw, hd=hd, scale=hd ** -0.5)
    c_spec = pl.BlockSpec((None, bb, p, kw), lambda b: (layer, b, 0, 0))
    return pl.pallas_call(
        kern, grid=(B // bb,),
        in_specs=[pl.BlockSpec((bb, T, qw), lambda b: (b, 0, 0)),
                  c_spec, pl.BlockSpec((bb, T, kw), lambda b: (b, 0, qw // kw)),
                  c_spec, pl.BlockSpec((bb, T, kw), lambda b: (b, 0, qw // kw + 1)),
                  pl.BlockSpec((None,) + table.shape[1:], lambda b: (layer, 0, 0, 0))],
        out_specs=pl.BlockSpec((bb, T, qw), lambda b: (b, 0, 0)),
        out_shape=jax.ShapeDtypeStruct((B, T, qw), out_dtype),
        compiler_params=_cparams(("parallel",)), name="swa_decode",
    )(z3, cache_k, z3, cache_v, z3, table)


class _ConvTaps:
    def __init__(self, i, j, cols, tm, prev_ref, st_ref, carry_scr, seq_len):
        self.j, self.cols, self.seq_len = j, cols, seq_len
        self.prev_ref, self.st_ref, self.carry_scr = prev_ref, st_ref, carry_scr
        self.single = tm <= seq_len
        if self.single:
            @pl.when((i % (seq_len // tm)) == 0)
            def _():
                carry_scr[j, :, cols] = prev_ref[0, :, cols]

            self.before = carry_scr[j, :, cols]

    def chunk(self, x, r):
        n, w = x.shape
        if self.single:
            before = self.before
            row8 = lax.broadcasted_iota(jnp.int32, (SUBLANE, w), 0)

            def back(s):
                xr = pltpu.roll(x, s, 0)
                top = jnp.where(row8 < s, pltpu.roll(before, s, 0), xr[0:SUBLANE])
                return jnp.concatenate([top, xr[SUBLANE:]], axis=0)

            self.before = x[n - SUBLANE:n]
            return back(1), back(2)
        t = self.seq_len
        assert r % t == 0 and n % t == 0
        seqs = slice(r // t, (r + n) // t)
        tix = lax.rem(lax.broadcasted_iota(jnp.int32, (n, w), 0), t)
        s1 = jnp.broadcast_to(self.prev_ref[seqs, SUBLANE - 1:SUBLANE, self.cols], (n // t, t, w)).reshape(n, w)
        s0 = jnp.broadcast_to(self.prev_ref[seqs, SUBLANE - 2:SUBLANE - 1, self.cols], (n // t, t, w)).reshape(n, w)
        self.st_ref[seqs, :, self.cols] = x.reshape(n // t, t, w)[:, t - SUBLANE:, :]
        x1 = jnp.where(tix == 0, s1, pltpu.roll(x, 1, 0))
        x2 = jnp.where(tix == 0, s0, jnp.where(tix == 1, s1, pltpu.roll(x, 2, 0)))
        return x1, x2

    def finish(self):
        if self.single:
            self.carry_scr[self.j, :, self.cols] = self.before
            self.st_ref[0, :, self.cols] = self.before


def _row_blocking(B, T, tm):
    if T >= tm:
        tm = _tile(T, tm, SUBLANE)
        return tm, 1, (lambda i: i // (T // tm))
    assert T % SUBLANE == 0
    nseq = _tile(B, max(tm // T, 1), 1)
    return nseq * T, nseq, (lambda i: i)


def _conv_branch_kernel(cb_ref, cc_ref, ch_ref, prev_ref, w_ref, o_ref, st_ref, carry_scr, *, seq_len):
    i, j = pl.program_id(0), pl.program_id(1)
    taps = _ConvTaps(i, j, slice(None), o_ref.shape[0], prev_ref, st_ref, carry_scr, seq_len)
    for r, n in _row_chunks(o_ref.shape[0]):
        rows = slice(r, r + n)
        u = cc_ref[rows, :].astype(F32) * ch_ref[rows, :].astype(F32)
        u1, u2 = taps.chunk(u, r)
        y = u2 * w_ref[0:1, :] + u1 * w_ref[1:2, :] + u * w_ref[2:3, :]
        o_ref[rows, :] = (cb_ref[rows, :].astype(F32) * y).astype(o_ref.dtype)
    taps.finish()


def conv_branch(z, prev8, conv_w, layer, B, T, out_dtype, tm=1024, tc=512):
    M = B * T
    width = conv_w.shape[2]
    tc = _tile(width, tc, LANE)
    nc = width // tc
    tm, nseq, seq_of = _row_blocking(B, T, tm)
    n_st = (M // tm) * nseq
    z_specs = [pl.BlockSpec((tm, tc), (lambda i, j, o=n * nc: (i, o + j))) for n in range(3)]
    out, st = pl.pallas_call(
        functools.partial(_conv_branch_kernel, seq_len=T), grid=(M // tm, nc),
        in_specs=z_specs + [pl.BlockSpec((nseq, SUBLANE, tc), lambda i, j: (seq_of(i), 0, j)),
                            pl.BlockSpec((None, CONV_W, tc), lambda i, j: (layer, 0, j))],
        out_specs=[pl.BlockSpec((tm, tc), lambda i, j: (i, j)),
                   pl.BlockSpec((nseq, SUBLANE, tc), lambda i, j: (i, 0, j))],
        out_shape=[jax.ShapeDtypeStruct((M, width), out_dtype),
                   jax.ShapeDtypeStruct((n_st, SUBLANE, width), F32)],
        scratch_shapes=[pltpu.VMEM((nc, SUBLANE, tc), F32)],
        compiler_params=_cparams(("arbitrary", "arbitrary")), name="conv_branch",
    )(z, z, z, prev8, conv_w)
    return out, st.reshape(B, n_st // B, SUBLANE, width)[:, -1]


def _ffn_up_tile(i, j, h_ref, wu_ref, wg_ref, prev_ref, cw_ref, cb_ref, act_ref, st_ref, carry_scr, seq_len):
    tm, tn = act_ref.shape
    for c, w in _col_chunks(tn):
        cols = slice(c, c + w)
        wu = wu_ref[:, cols].astype(BF16)
        wg = wg_ref[:, cols].astype(BF16)
        taps = _ConvTaps(i, j, cols, tm, prev_ref, st_ref, carry_scr, seq_len)
        for r, n in _row_chunks(tm):
            h = h_ref[r:r + n, :]
            g = _dot(h, wg)
            g1, g2 = taps.chunk(g, r)
            gc = g2 * cw_ref[0:1, cols] + g1 * cw_ref[1:2, cols] + g * cw_ref[2:3, cols] + cb_ref[:, cols]
            act_ref[r:r + n, cols] = (gc * _sigmoid(gc) * _dot(h, wu)).astype(act_ref.dtype)
        taps.finish()


def _ffn_up_kernel(h_ref, h2_ref, wu_ref, wg_ref, prev_ref, prev2_ref, cw_ref, cb_ref,
                   act_ref, st_ref, act2_ref, st2_ref, carry_scr, *, seq_len, seq_len2):
    i, j = pl.program_id(0), pl.program_id(1)
    _ffn_up_tile(i, j, h_ref, wu_ref, wg_ref, prev_ref, cw_ref, cb_ref, act_ref, st_ref, carry_scr, seq_len)

    @pl.when(i == 0)
    def _():
        _ffn_up_tile(0, j, h2_ref, wu_ref, wg_ref, prev2_ref, cw_ref, cb_ref, act2_ref, st2_ref, None, seq_len2)


def ffn_first_half(h, h2, w_up, layer, col0, width, prev8, prev8_2, conv_w, conv_b, B, T, B2, T2, out_dtype,
                   tm=2048, tn=512):
    M, K = h.shape
    M2 = h2.shape[0]
    dff = w_up.shape[2] // 2
    tn = min(tn, width)
    assert col0 % LANE == 0 and dff % LANE == 0 and width % tn == 0 and M2 == B2 * T2 and T2 % SUBLANE == 0
    nj = width // tn
    tm, nseq, seq_of = _row_blocking(B, T, tm)
    n_st = (M // tm) * nseq

    def col(j, base=col0):
        return pl.multiple_of(base + j * tn, LANE)

    def parked(i, j):
        return jnp.where(i == 0, j, nj - 1)

    def win(rows):
        return pl.BlockSpec((pl.Element(rows), pl.Element(tn)), lambda i, j: (0, col(j)))

    def state_win(n, first, tile):
        return pl.BlockSpec((pl.Element(n), pl.Element(SUBLANE), pl.Element(tn)),
                            lambda i, j: (first(i), 0, col(tile(i, j))))

    act, st, act2, st2 = pl.pallas_call(
        functools.partial(_ffn_up_kernel, seq_len=T, seq_len2=T2), grid=(M // tm, nj),
        in_specs=[pl.BlockSpec((tm, K), lambda i, j: (i, 0)),
                  pl.BlockSpec((M2, K), lambda i, j: (0, 0)),
                  pl.BlockSpec((None, pl.Element(K), pl.Element(tn)), lambda i, j: (layer, 0, col(j))),
                  pl.BlockSpec((None, pl.Element(K), pl.Element(tn)), lambda i, j: (layer, 0, col(j, dff + col0))),
                  state_win(nseq, lambda i: seq_of(i) * nseq, lambda i, j: j),
                  state_win(B2, lambda i: 0, parked),
                  win(CONV_W), win(1)],
        out_specs=[pl.BlockSpec((tm, tn), lambda i, j: (i, j)),
                   pl.BlockSpec((nseq, SUBLANE, tn), lambda i, j: (i, 0, j)),
                   _second_group_tile(M2, tn, nj),
                   pl.BlockSpec((B2, SUBLANE, tn), lambda i, j: (0, 0, parked(i, j)))],
        out_shape=[jax.ShapeDtypeStruct((M, width), out_dtype),
                   jax.ShapeDtypeStruct((n_st, SUBLANE, width), F32),
                   jax.ShapeDtypeStruct((M2, width), out_dtype),
                   jax.ShapeDtypeStruct((B2, SUBLANE, width), F32)],
        scratch_shapes=[pltpu.VMEM((nj, SUBLANE, tn), F32)],
        compiler_params=_cparams(("arbitrary", "arbitrary")), name="ffn_up",
    )(h, h2, w_up, w_up, prev8, prev8_2, conv_w, conv_b)
    return (act, st.reshape(B, n_st // B, SUBLANE, width)[:, -1]), (act2, st2)


def _merge_kernel(a_ref, b_ref, c_ref, wb_ref, g0_ref, g1_ref, g2_ref, o_ref):
    xs = (a_ref, b_ref, c_ref)
    gs = (g0_ref, g1_ref, g2_ref)
    for c, w in _col_chunks(o_ref.shape[1]):
        cols = slice(c, c + w)
        ws = [wb_ref[n, :, cols].astype(BF16) for n in range(3)]
        for r, m in _row_chunks(o_ref.shape[0]):
            rows = slice(r, r + m)

            def term(n):
                return _sigmoid(gs[n][rows, cols].astype(F32)) * _dot(xs[n][rows, :].astype(BF16), ws[n])

            o_ref[rows, cols] = ((term(0) + term(1)) + term(2)).astype(o_ref.dtype)


def merge(br_a, br_b, br_c, wb, layer, zg, gate_col0, out_dtype, tm=2048, tn=256):
    M, mix = br_a.shape
    d = wb.shape[3]
    tm = _tile(M, tm, SUBLANE)
    tn = _tile(d, tn, LANE)
    assert gate_col0 % tn == 0
    br_spec = pl.BlockSpec((tm, mix), lambda i, j: (i, 0))
    g_specs = [pl.BlockSpec((tm, tn), (lambda i, j, o=(gate_col0 + n * d) // tn: (i, o + j))) for n in range(3)]
    return pl.pallas_call(
        _merge_kernel, grid=(M // tm, d // tn),
        in_specs=[br_spec, br_spec, br_spec,
                  pl.BlockSpec((None, 3, mix, tn), lambda i, j: (layer, 0, 0, j))] + g_specs,
        out_specs=pl.BlockSpec((tm, tn), lambda i, j: (i, j)),
        out_shape=jax.ShapeDtypeStruct((M, d), out_dtype),
        compiler_params=_cparams(("parallel", "arbitrary")), name="merge",
    )(br_a, br_b, br_c, wb, zg, zg, zg)


def _xattn_kernel(q_ref, k_ref, v_ref, o_ref, *, heads, hd, scale):
    md = _mxu_dtype(q_ref.shape[0])
    for h in range(heads):
        sl = slice(h * hd, (h + 1) * hd)
        s = _dot_nt(q_ref[:, sl].astype(md), k_ref[:, sl].astype(md)) * scale
        e = jnp.exp(s - jnp.max(s, axis=-1, keepdims=True))
        den = jnp.sum(e, axis=-1, keepdims=True)
        o_ref[:, sl] = (_dot(e.astype(md), v_ref[:, sl].astype(md)) / den).astype(o_ref.dtype)


def xattn(q3, mem_k, mem_v, layer, heads, out_dtype, tq=512):
    B, T, w = q3.shape
    nm = mem_k.shape[-2]
    hd = w // heads
    tq = _tile(T, tq, SUBLANE)
    kern = functools.partial(_xattn_kernel, heads=heads, hd=hd, scale=hd ** -0.5)
    if mem_k.ndim == 4:
        m_spec = pl.BlockSpec((None, None, nm, w), lambda b, t: (layer, b, 0, 0))
    else:
        m_spec = pl.BlockSpec((None, nm, w), lambda b, t: (b, 0, 0))
    return pl.pallas_call(
        kern, grid=(B, T // tq),
        in_specs=[pl.BlockSpec((None, tq, w), lambda b, t: (b, t, 0)), m_spec, m_spec],
        out_specs=pl.BlockSpec((None, tq, w), lambda b, t: (b, t, 0)),
        out_shape=jax.ShapeDtypeStruct((B, T, w), out_dtype),
        compiler_params=_cparams(("parallel", "arbitrary")), name="xattn",
    )(q3, mem_k, mem_v)


def _xattn_decode_kernel(q_ref, k_ref, v_ref, o_ref, *, heads, hd, scale):
    bb, T, _ = q_ref.shape
    nm = k_ref.shape[1] // heads
    rows = max(T, 16)
    pairs = [(s, h) for s in range(bb) for h in range(heads)]
    scs = [_dot_nt(_pad_rows(q_ref[s, :, h * hd:(h + 1) * hd].astype(F32), rows).astype(BF16),
                   k_ref[s, pl.ds(h, nm, stride=heads), :].astype(BF16)) * scale for s, h in pairs]
    es = [jnp.exp(sc - jnp.max(sc, axis=-1, keepdims=True)) for sc in scs]
    dens = [jnp.sum(e, axis=-1, keepdims=True) for e in es]
    for (s, h), e, den in zip(pairs, es, dens):
        v = v_ref[s, pl.ds(h, nm, stride=heads), :].astype(BF16)
        o_ref[s, :, h * hd:(h + 1) * hd] = (_dot(e.astype(BF16), v) / den)[:T].astype(o_ref.dtype)


def xattn_decode(q3, mem_k, mem_v, layer, heads, out_dtype, seqs_per_step=8):
    B, T, w = q3.shape
    hd = w // heads
    bb = _tile(B, seqs_per_step, 1)
    kern = functools.partial(_xattn_decode_kernel, heads=heads, hd=hd, scale=hd ** -0.5)
    m_spec = pl.BlockSpec((None, bb) + mem_k.shape[2:], lambda b: (layer, b, 0, 0))
    return pl.pallas_call(
        kern, grid=(B // bb,),
        in_specs=[pl.BlockSpec((bb, T, w), lambda b: (b, 0, 0)), m_spec, m_spec],
        out_specs=pl.BlockSpec((bb, T, w), lambda b: (b, 0, 0)),
        out_shape=jax.ShapeDtypeStruct((B, T, w), out_dtype),
        compiler_params=_cparams(("parallel",)), name="xattn_decode",
    )(q3, mem_k, mem_v)


def _round_up(n, m):
    return -(-n // m) * m


def kernel(x_prompt, x_sample, state_gla, cache_swa_k, cache_swa_v, state_conv, state_ffn, cache_mem_k, cache_mem_v, mem_prompt, norm_mix, w_in, gla_gate_up, gla_gate_b, gla_norm, swa_sinks, rel_bias, conv_w, w_branch, w_out, norm_x, wx_q, wx_k, wx_v, wx_o, norm_ffn, ffn_up, ffn_conv_w, ffn_conv_b, ffn_down, norm_final):
    Bp, Tp, D = x_prompt.shape
    Bs, Ts, _ = x_sample.shape
    depth = w_in.shape[0]
    gla_heads, gla_dk, gla_dv = state_gla.shape[2:]
    swa_kv, swa_hd = cache_swa_k.shape[3:]
    swa_heads = swa_sinks.shape[1]
    x_heads, x_hd = cache_mem_k.shape[3:]
    n_mem = mem_prompt.shape[1]
    mix = conv_w.shape[2]
    dff = ffn_down.shape[1]
    rank = gla_gate_up.shape[1]
    gqk = gla_heads * gla_dk
    sqw, skw = swa_heads * swa_hd, swa_kv * swa_hd
    win = cache_swa_k.shape[2]
    xw = x_heads * x_hd
    assert win == WINDOW and mix == gla_heads * gla_dv == sqw

    w_in_t = jnp.swapaxes(w_in, 1, 2)
    a_w = 2 * gqk + 2 * mix
    s_w = sqw + 2 * skw
    o_glr, o_swa = a_w, a_w + rank
    o_conv = o_swa + s_w
    o_gates = o_conv + 3 * mix
    ffn_main = dff // 512 * 512
    ffn_parts = [(0, ffn_main)] + ([(ffn_main, dff - ffn_main)] if dff > ffn_main else [])

    bias_p = _swa_bias_table(rel_bias, min(Tp, WINDOW), WINDOW, True)
    table_s = _swa_decode_table(rel_bias, swa_sinks, Ts, win, swa_kv)
    cache_k4 = cache_swa_k.reshape(depth, Bs, win, skw)
    cache_v4 = cache_swa_v.reshape(depth, Bs, win, skw)
    cmem_k4 = cache_mem_k.reshape(depth, Bs, n_mem * x_heads, x_hd)
    cmem_v4 = cache_mem_v.reshape(depth, Bs, n_mem * x_heads, x_hd)
    mem2 = mem_prompt.reshape(Bp * n_mem, D)

    def first8(st):
        return jnp.pad(st, ((0, 0), (SUBLANE - (CONV_W - 1), 0), (0, 0)))

    def in_proj(i, xa, xb, dt_a, dt_b):
        ha, hb = rmsnorm(xa, norm_mix[i], BF16), rmsnorm(xb, norm_mix[i], BF16)
        zg_a, lg_a, zg_b, lg_b = matmul_nt_gate(ha, hb, w_in_t, i, a_w, o_glr, gla_gate_up, gla_gate_b, dt_a, dt_b,
                                                name="in_proj_gla")
        zs_a, zs_b = matmul_nt(ha, hb, w_in_t, i, o_swa, s_w, dt_a, dt_b, tn=s_w, name="in_proj_swa")
        zc_a, zc_b = matmul_nt(ha, hb, w_in_t, i, o_conv, 3 * mix + 3 * D, dt_a, dt_b, name="in_proj_conv_gates")
        return (zg_a, lg_a, zs_a, zc_a), (zg_b, lg_b, zs_b, zc_b)

    def trunk_layer(i, x, zs_all, B, T, decode, s0, conv_prev8, mem_k, mem_v, act_dtype):
        M = B * T
        zg, lg, zs, zc = zs_all
        zs3 = zs.reshape(B, T, s_w)
        br_a, gla_st = gla(zg.reshape(B, T, a_w), lg.reshape(B, T, gqk), gla_norm, s0, i,
                           gla_heads, gla_dk, gla_dv, act_dtype, seqs_per_step=4 if decode else 1)
        if decode:
            br_b = swa_decode(zs3, cache_k4, cache_v4, table_s, i, swa_kv, swa_hd, act_dtype)
        else:
            br_b = swa(zs3, None, None, swa_sinks, bias_p, i, swa_kv, swa_hd, min(T, WINDOW), act_dtype)
        br_c, conv_st = conv_branch(zc, conv_prev8, conv_w, i, B, T, act_dtype)
        merged = merge(br_a.reshape(M, mix), br_b.reshape(M, mix), br_c, w_branch, i, zc, 3 * mix, BF16)
        x, hx = res_matmul_norm(merged, w_out, x, norm_x[i], i, name="out_proj")
        q = matmul(hx, wx_q, i, act_dtype, name="xq_proj")
        if decode:
            xo = xattn_decode(q.reshape(B, T, xw), mem_k, mem_v, i, x_heads, act_dtype)
        else:
            xo = xattn(q.reshape(B, T, xw), mem_k, mem_v, i, x_heads, act_dtype)
        x, hf = res_matmul_norm(xo.reshape(M, xw), wx_o, x, norm_ffn[i], i, name="xo_proj")
        k_new = zs3[:, T - min(T, WINDOW):, sqw:sqw + skw].astype(F32)
        v_new = zs3[:, T - min(T, WINDOW):, sqw + skw:].astype(F32)
        return x, hf, (gla_st, k_new, v_new, conv_st[:, SUBLANE - (CONV_W - 1):])

    xp = x_prompt.reshape(Bp * Tp, D)
    xs = x_sample.reshape(Bs * Ts, D)
    act_p = BF16
    act_s = BF16 if Ts % 16 == 0 else F32
    st_p, st_s = [], []
    for i in range(depth):
        mem_k = matmul(mem2, wx_k, i, F32, name="mem_k").reshape(Bp, n_mem, xw)
        mem_v = matmul(mem2, wx_v, i, F32, name="mem_v").reshape(Bp, n_mem, xw)
        z_p, z_s = in_proj(i, xp, xs, act_p, act_s)
        xp, hf_p, sp = trunk_layer(i, xp, z_p, Bp, Tp, False, None, jnp.zeros((Bp, SUBLANE, mix), F32),
                                   mem_k, mem_v, act_p)
        xs, hf_s, ss = trunk_layer(i, xs, z_s, Bs, Ts, True, state_gla, first8(state_conv[i]),
                                   cmem_k4, cmem_v4, act_s)
        ffn = [ffn_first_half(hf_p, hf_s, ffn_up, i, c0, wd, jnp.zeros((Bp, SUBLANE, dff), F32),
                              first8(state_ffn[i]), ffn_conv_w[i], ffn_conv_b[i].reshape(1, dff),
                              Bp, Tp, Bs, Ts, BF16) for c0, wd in ffn_parts]
        xp, xs = res_matmul([f[0][0] for f in ffn], ffn_down, xp, i, a2_parts=[f[1][0] for f in ffn], res2=xs,
                            tm=2048, tn=256, name="ffn_down")
        sp += (jnp.concatenate([f[0][1] for f in ffn], axis=-1)[:, SUBLANE - (CONV_W - 1):],)
        ss += (jnp.concatenate([f[1][1] for f in ffn], axis=-1)[:, SUBLANE - (CONV_W - 1):],)
        st_p.append(dict(gla=sp[0], k=sp[1].reshape(Bp, WINDOW, swa_kv, swa_hd),
                         v=sp[2].reshape(Bp, WINDOW, swa_kv, swa_hd), conv=sp[3], ffn=sp[4],
                         mem_k=mem_k.reshape(Bp, n_mem, x_heads, x_hd),
                         mem_v=mem_v.reshape(Bp, n_mem, x_heads, x_hd)))
        kc = jnp.concatenate([cache_k4[i], ss[1]], axis=1)[:, Ts:]
        vc = jnp.concatenate([cache_v4[i], ss[2]], axis=1)[:, Ts:]
        st_s.append(dict(gla=ss[0], k=kc.reshape(Bs, win, swa_kv, swa_hd), v=vc.reshape(Bs, win, swa_kv, swa_hd),
                         conv=ss[3], ffn=ss[4]))
    y_prompt = rmsnorm(xp, norm_final, F32).reshape(Bp, Tp, D)
    y_sample = rmsnorm(xs, norm_final, F32).reshape(Bs, Ts, D)

    def stack(lst, key):
        return jnp.stack([s[key] for s in lst])

    return (y_prompt, y_sample, stack(st_p, "gla"), stack(st_s, "gla"), stack(st_p, "k"), stack(st_p, "v"),
            stack(st_s, "k"), stack(st_s, "v"), stack(st_p, "conv"), stack(st_s, "conv"),
            stack(st_p, "ffn"), stack(st_s, "ffn"), stack(st_p, "mem_k"), stack(st_p, "mem_v"))
```

```python
import functools
import math

import jax
import jax.numpy as jnp
from jax import lax
from jax.experimental import pallas as pl
from jax.experimental.pallas import tpu as pltpu

F32 = jnp.float32
BF16 = jnp.bfloat16

EPS = 1e-6
GLA_TAU = 16.0
GLA_CHUNK = 64
GLA_SUB = 8
WINDOW = 128
N_BUCKETS = 32
MAX_DIST = 128
CONV_W = 3
NEG = -1e30

LANE = 128
SUBLANE = 8
MXU_COLS = 256
MXU_ROWS = 512
VMEM_LIMIT = 56 * 1024 * 1024


def _cparams(sem):
    return pltpu.CompilerParams(dimension_semantics=sem, vmem_limit_bytes=VMEM_LIMIT)


def _tile(n, pref, unit):
    if n <= pref:
        return n
    t = (pref // unit) * unit
    while t > unit and n % t:
        t -= unit
    assert n % t == 0, (n, pref, unit)
    return t


def _mxu_dtype(rows):
    return BF16 if rows % 16 == 0 else F32


def _sigmoid(x):
    return 1.0 / (1.0 + jnp.exp(-x))


def _dot(a, b):
    return jnp.dot(a, b, preferred_element_type=F32)


def _dot_nt(a, b):
    return lax.dot_general(a, b, (((1,), (1,)), ((), ())), preferred_element_type=F32)


def _col_chunks(n):
    if n % LANE:
        return [(0, n)]
    return [(c, min(MXU_COLS, n - c)) for c in range(0, n, MXU_COLS)]


def _row_chunks(n):
    m = MXU_ROWS if n % MXU_ROWS == 0 else n
    return [(r, m) for r in range(0, n, m)]


def _pad_rows(x, rows):
    if x.shape[0] == rows:
        return x
    return jnp.concatenate([x, jnp.zeros((rows - x.shape[0], x.shape[1]), x.dtype)], axis=0)


def _norm_kernel(x_ref, g_ref, o_ref):
    x = x_ref[...]
    ms = jnp.mean(x * x, axis=-1, keepdims=True)
    o_ref[...] = (x * lax.rsqrt(ms + EPS) * g_ref[...]).astype(o_ref.dtype)


def rmsnorm(x, gain, out_dtype, tm=512):
    M, K = x.shape
    tm = _tile(M, tm, SUBLANE)
    return pl.pallas_call(
        _norm_kernel, grid=(M // tm,),
        in_specs=[pl.BlockSpec((tm, K), lambda i: (i, 0)), pl.BlockSpec((1, K), lambda i: (0, 0))],
        out_specs=pl.BlockSpec((tm, K), lambda i: (i, 0)),
        out_shape=jax.ShapeDtypeStruct((M, K), out_dtype),
        compiler_params=_cparams(("parallel",)), name="rmsnorm")(x, gain.reshape(1, K))


def _mm_kernel(a_ref, w_ref, o_ref):
    for c, w in _col_chunks(o_ref.shape[1]):
        wc = w_ref[:, c:c + w].astype(BF16)
        for r, n in _row_chunks(o_ref.shape[0]):
            o_ref[r:r + n, c:c + w] = _dot(a_ref[r:r + n, :].astype(BF16), wc).astype(o_ref.dtype)


def _mm_nt_kernel(a_ref, wt_ref, o_ref):
    for c, w in _col_chunks(o_ref.shape[1]):
        wc = wt_ref[c:c + w, :].astype(BF16)
        for r, n in _row_chunks(o_ref.shape[0]):
            o_ref[r:r + n, c:c + w] = _dot_nt(a_ref[r:r + n, :], wc).astype(o_ref.dtype)


def _log_decay(a_ref, wglr_ref, up_ref, gb_ref, lg_ref):
    wg = _pad_rows(wglr_ref[...], LANE).astype(BF16)
    up = _pad_rows(up_ref[...], LANE).astype(BF16)
    for r, n in _row_chunks(lg_ref.shape[0]):
        glr = _dot_nt(a_ref[r:r + n, :], wg)
        z = _dot(glr.astype(BF16), up) + gb_ref[...]
        lg_ref[r:r + n, :] = -(jnp.maximum(-z, 0.0) + jnp.log1p(jnp.exp(-jnp.abs(z)))) / GLA_TAU


def _mm_nt_gate_kernel(a_ref, wt_ref, wglr_ref, up_ref, gb_ref, o_ref, lg_ref):
    _mm_nt_kernel(a_ref, wt_ref, o_ref)

    @pl.when(pl.program_id(1) == 0)
    def _():
        _log_decay(a_ref, wglr_ref, up_ref, gb_ref, lg_ref)


def _mm_nt2_kernel(a_ref, a2_ref, wt_ref, o_ref, o2_ref):
    _mm_nt_kernel(a_ref, wt_ref, o_ref)

    @pl.when(pl.program_id(0) == 0)
    def _():
        _mm_nt_kernel(a2_ref, wt_ref, o2_ref)


def _mm_nt_gate2_kernel(a_ref, a2_ref, wt_ref, wglr_ref, up_ref, gb_ref, o_ref, lg_ref, o2_ref, lg2_ref):
    _mm_nt_gate_kernel(a_ref, wt_ref, wglr_ref, up_ref, gb_ref, o_ref, lg_ref)

    @pl.when(pl.program_id(0) == 0)
    def _():
        _mm_nt_kernel(a2_ref, wt_ref, o2_ref)

    @pl.when((pl.program_id(0) == 0) & (pl.program_id(1) == 0))
    def _():
        _log_decay(a2_ref, wglr_ref, up_ref, gb_ref, lg2_ref)


def _res_mm_body(a_refs, w_refs, r_ref, o_ref):
    for c, w in _col_chunks(o_ref.shape[1]):
        wcs = [w_ref[:, c:c + w].astype(BF16) for w_ref in w_refs]
        for r, n in _row_chunks(o_ref.shape[0]):
            acc = r_ref[r:r + n, c:c + w]
            for a_ref, wc in zip(a_refs, wcs):
                acc = acc + _dot(a_ref[r:r + n, :].astype(BF16), wc)
            o_ref[r:r + n, c:c + w] = acc


def _res_mm_kernel(*refs, n_parts, two_groups):
    a_refs, w_refs = refs[0:2 * n_parts:2], refs[1:2 * n_parts:2]
    if not two_groups:
        _res_mm_body(a_refs, w_refs, refs[-2], refs[-1])
        return
    _res_mm_body(a_refs, w_refs, refs[2 * n_parts], refs[-2])

    @pl.when(pl.program_id(0) == 0)
    def _():
        _res_mm_body(refs[2 * n_parts + 1:3 * n_parts + 1], w_refs, refs[3 * n_parts + 1], refs[-1])


def matmul(a, w, layer, out_dtype, tm=2048, tn=512, name="mm"):
    M, K = a.shape
    N = w.shape[-1]
    tm = _tile(M, tm, SUBLANE)
    tn = _tile(N, tn, LANE)
    return pl.pallas_call(
        _mm_kernel, grid=(M // tm, N // tn),
        in_specs=[pl.BlockSpec((tm, K), lambda i, j: (i, 0)),
                  pl.BlockSpec((None, K, tn), lambda i, j: (layer, 0, j))],
        out_specs=pl.BlockSpec((tm, tn), lambda i, j: (i, j)),
        out_shape=jax.ShapeDtypeStruct((M, N), out_dtype),
        compiler_params=_cparams(("parallel", "arbitrary")), name=name)(a, w)


def _wt_spec(layer, row0, tn, K):
    return pl.BlockSpec((None, pl.Element(tn), pl.Element(K)),
                        lambda i, j: (layer, pl.multiple_of(row0 + j * tn, SUBLANE), 0))


def _second_group_tile(rows, tn, nj):
    return pl.BlockSpec((rows, tn), lambda i, j: (0, jnp.where(i == 0, j, nj - 1)))


def matmul_nt(a, a2, wt, layer, row0, n, out_dtype, out2_dtype, tm=2048, tn=512, name="mm_nt"):
    M, K = a.shape
    M2 = a2.shape[0]
    assert row0 % SUBLANE == 0
    tm = _tile(M, tm, SUBLANE)
    tn = _tile(n, tn, LANE)
    nj = n // tn
    return pl.pallas_call(
        _mm_nt2_kernel, grid=(M // tm, nj),
        in_specs=[pl.BlockSpec((tm, K), lambda i, j: (i, 0)), pl.BlockSpec((M2, K), lambda i, j: (0, 0)),
                  _wt_spec(layer, row0, tn, K)],
        out_specs=[pl.BlockSpec((tm, tn), lambda i, j: (i, j)), _second_group_tile(M2, tn, nj)],
        out_shape=[jax.ShapeDtypeStruct((M, n), out_dtype), jax.ShapeDtypeStruct((M2, n), out2_dtype)],
        compiler_params=_cparams(("arbitrary", "arbitrary")), name=name)(a, a2, wt)


def matmul_nt_gate(a, a2, wt, layer, n, glr_row0, gate_up, gate_b, out_dtype, out2_dtype, tm=2048, tn=512,
                   name="mm_nt_gate"):
    M, K = a.shape
    M2 = a2.shape[0]
    depth, rank, gw = gate_up.shape
    assert glr_row0 % SUBLANE == 0 and rank % SUBLANE == 0
    tm = _tile(M, tm, SUBLANE)
    tn = _tile(n, tn, LANE)
    nj = n // tn
    return pl.pallas_call(
        _mm_nt_gate2_kernel, grid=(M // tm, nj),
        in_specs=[pl.BlockSpec((tm, K), lambda i, j: (i, 0)), pl.BlockSpec((M2, K), lambda i, j: (0, 0)),
                  _wt_spec(layer, 0, tn, K),
                  pl.BlockSpec((None, pl.Element(rank), pl.Element(K)), lambda i, j: (layer, glr_row0, 0)),
                  pl.BlockSpec((None, rank, gw), lambda i, j: (layer, 0, 0)),
                  pl.BlockSpec((None, 1, gw), lambda i, j: (layer, 0, 0))],
        out_specs=[pl.BlockSpec((tm, tn), lambda i, j: (i, j)), pl.BlockSpec((tm, gw), lambda i, j: (i, 0)),
                   _second_group_tile(M2, tn, nj), pl.BlockSpec((M2, gw), lambda i, j: (0, 0))],
        out_shape=[jax.ShapeDtypeStruct((M, n), out_dtype), jax.ShapeDtypeStruct((M, gw), F32),
                   jax.ShapeDtypeStruct((M2, n), out2_dtype), jax.ShapeDtypeStruct((M2, gw), F32)],
        compiler_params=_cparams(("arbitrary", "arbitrary")), name=name,
    )(a, a2, wt, wt, gate_up, gate_b.reshape(depth, 1, gw))


def _res_mm_norm_kernel(a_ref, w_ref, r_ref, g_ref, o_ref, h_ref, wb_scr):
    @pl.when(pl.program_id(0) == 0)
    def _():
        for c, w in _col_chunks(wb_scr.shape[1]):
            wb_scr[:, c:c + w] = w_ref[:, c:c + w].astype(BF16)

    n_cols = o_ref.shape[1]
    for r, n in _row_chunks(o_ref.shape[0]):
        rows = slice(r, r + n)
        a = a_ref[rows, :].astype(BF16)
        ss = jnp.zeros((n, 1), F32)
        for c, w in _col_chunks(n_cols):
            x = r_ref[rows, c:c + w] + _dot(a, wb_scr[:, c:c + w])
            o_ref[rows, c:c + w] = x
            ss = ss + jnp.sum(x * x, axis=-1, keepdims=True)
        scale = lax.rsqrt(ss / n_cols + EPS)
        for c, w in _col_chunks(n_cols):
            h_ref[rows, c:c + w] = (o_ref[rows, c:c + w] * scale * g_ref[:, c:c + w]).astype(h_ref.dtype)


def res_matmul_norm(a, w, res, gain, layer, tm=512, name="res_mm_norm"):
    M, K = a.shape
    N = w.shape[-1]
    tm = _tile(M, tm, SUBLANE)
    once = pl.Buffered(1)
    return pl.pallas_call(
        _res_mm_norm_kernel, grid=(M // tm,),
        in_specs=[pl.BlockSpec((tm, K), lambda i: (i, 0)),
                  pl.BlockSpec((None, K, N), lambda i: (layer, 0, 0), pipeline_mode=once),
                  pl.BlockSpec((tm, N), lambda i: (i, 0)),
                  pl.BlockSpec((1, N), lambda i: (0, 0))],
        out_specs=[pl.BlockSpec((tm, N), lambda i: (i, 0)), pl.BlockSpec((tm, N), lambda i: (i, 0))],
        out_shape=[jax.ShapeDtypeStruct((M, N), F32), jax.ShapeDtypeStruct((M, N), BF16)],
        scratch_shapes=[pltpu.VMEM((K, N), BF16)],
        compiler_params=_cparams(("arbitrary",)), name=name)(a, w, res, gain.reshape(1, N))


def res_matmul(a_parts, w, res, layer, a2_parts=None, res2=None, tm=2048, tn=512, name="res_mm"):
    M = a_parts[0].shape[0]
    N = w.shape[-1]
    tm = _tile(M, tm, SUBLANE)
    tn = _tile(N, tn, LANE)
    nj = N // tn
    two = a2_parts is not None
    in_specs, args, row0 = [], [], 0
    for a in a_parts:
        kp = a.shape[1]
        assert row0 % SUBLANE == 0
        in_specs += [pl.BlockSpec((tm, kp), lambda i, j: (i, 0), pipeline_mode=pl.Buffered(1)),
                     pl.BlockSpec((None, pl.Element(kp), pl.Element(tn)),
                                  lambda i, j, r0=row0: (layer, r0, pl.multiple_of(j * tn, LANE)))]
        args += [a, w]
        row0 += kp
    assert row0 == w.shape[1]
    in_specs.append(pl.BlockSpec((tm, tn), lambda i, j: (i, j)))
    args.append(res)
    out_specs = [pl.BlockSpec((tm, tn), lambda i, j: (i, j))]
    out_shape = [jax.ShapeDtypeStruct((M, N), F32)]
    if two:
        M2 = res2.shape[0]
        in_specs += [pl.BlockSpec((M2, a.shape[1]), lambda i, j: (0, 0)) for a in a2_parts]
        in_specs.append(_second_group_tile(M2, tn, nj))
        args += list(a2_parts) + [res2]
        out_specs.append(_second_group_tile(M2, tn, nj))
        out_shape.append(jax.ShapeDtypeStruct((M2, N), F32))
    out = pl.pallas_call(
        functools.partial(_res_mm_kernel, n_parts=len(a_parts), two_groups=two), grid=(M // tm, nj),
        in_specs=in_specs, out_specs=out_specs, out_shape=out_shape,
        compiler_params=_cparams(("arbitrary", "arbitrary")), name=name)(*args)
    return out if two else out[0]


def _gla_chunk(qs, ks, vs, lgs, sts, md, lp, consts):
    row, rsub_l, lsub, ones = consts
    c = min(GLA_SUB, lp)
    dk = qs[0].shape[1]
    nblk = lp // c
    probs = range(len(qs))
    bs = list(lgs)
    sh = 1
    while sh < lp:
        bs = [b + jnp.where(row >= sh, pltpu.roll(b, sh, 0), 0.0) for b in bs]
        sh *= 2
    os_ = [_dot_nt((qs[i] * jnp.exp(bs[i])).astype(md), sts[i].astype(md)) for i in probs]
    rsums = []
    for q, k, b in zip(qs, ks, bs):
        ws = []
        for i in range(nblk):
            qb, kb, bb = (x[i * c:(i + 1) * c] for x in (q, k, b))
            ws += [qb * kb[s:s + 1, :] * jnp.exp(bb - bb[s:s + 1, :]) for s in range(c)]
        rsums.append(_dot(jnp.concatenate(ws, axis=0).astype(md), ones))
    a_rows = []
    for rsum in rsums:
        rows_p = []
        for i in range(nblk):
            a_i = jnp.zeros((c, LANE), F32)
            for s in range(c):
                a_i = jnp.where(lsub == i * c + s, rsum[(i * c + s) * c:(i * c + s + 1) * c], a_i)
            rows_p.append(jnp.where(lsub <= i * c + rsub_l, a_i, 0.0))
        a_rows.append(rows_p)
    n = c
    while n < lp:
        for p in range(0, lp, 2 * n):
            offs = []
            for q, k, b in zip(qs, ks, bs):
                ref = b[p + n - 1:p + n, :]
                qt = q[p + n:p + 2 * n] * jnp.exp(b[p + n:p + 2 * n] - ref)
                kt = k[p:p + n] * jnp.exp(ref - b[p:p + n])
                parts = [jnp.zeros((p, dk), F32)] if p else []
                ktp = jnp.concatenate(parts + [kt, jnp.zeros((LANE - p - n, dk), F32)], axis=0)
                offs.append(_dot_nt(_pad_rows(qt, max(n, 16)).astype(md), ktp.astype(md)))
            for rows_p, off in zip(a_rows, offs):
                for t in range(n // c):
                    rows_p[(p + n) // c + t] = rows_p[(p + n) // c + t] + off[t * c:(t + 1) * c]
        n *= 2
    vks = [_pad_rows(v, LANE) for v in vs]
    os_ = [os_[i] + _dot(jnp.concatenate(a_rows[i], axis=0).astype(md), vks[i].astype(md)) for i in probs]
    new_sts = []
    for i in probs:
        bend = bs[i][lp - 1:lp, :]
        kd = _pad_rows(ks[i] * jnp.exp(bend - bs[i]), LANE)
        new_sts.append(sts[i] * jnp.exp(bend) + _dot(vks[i].T.astype(md), kd.astype(md)))
    return os_, new_sts


def _gla_kernel(*refs, lin, lp, n_chunks, heads, scale, has_s0):
    if has_s0:
        q_ref, k_ref, v_ref, r_ref, lg_ref, gn_ref, s0_ref, o_ref, st_ref, s_scr = refs
    else:
        q_ref, k_ref, v_ref, r_ref, lg_ref, gn_ref, o_ref, st_ref, s_scr = refs
    t = pl.program_id(1)
    bb = q_ref.shape[0]
    dk = q_ref.shape[-1] // heads
    dv = v_ref.shape[-1] // heads
    c = min(GLA_SUB, lp)
    md = _mxu_dtype(lp)
    pairs = [(s, h) for s in range(bb) for h in range(heads)]

    @pl.when(t == 0)
    def _():
        for s, h in pairs:
            s_scr[s, h] = s0_ref[s, h].T if has_s0 else jnp.zeros((dv, dk), F32)

    row = lax.broadcasted_iota(jnp.int32, (lp, dk), 0)
    consts = (row, lax.broadcasted_iota(jnp.int32, (c, LANE), 0),
              lax.broadcasted_iota(jnp.int32, (c, LANE), 1), jnp.ones((dk, LANE), md))

    def chunk(ci, carry):
        rows = pl.ds(pl.multiple_of(ci * lin, lin), lin)
        kc = [slice(h * dk, (h + 1) * dk) for _, h in pairs]
        vc = [slice(h * dv, (h + 1) * dv) for _, h in pairs]
        qs = [_pad_rows(q_ref[s, rows, kc[i]].astype(F32), lp) * scale for i, (s, _) in enumerate(pairs)]
        ks = [_pad_rows(k_ref[s, rows, kc[i]].astype(F32), lp) for i, (s, _) in enumerate(pairs)]
        vs = [_pad_rows(v_ref[s, rows, vc[i]].astype(F32), lp) for i, (s, _) in enumerate(pairs)]
        lgs = [_pad_rows(lg_ref[s, rows, kc[i]], lp) for i, (s, _) in enumerate(pairs)]
        os_, sts = _gla_chunk(qs, ks, vs, lgs, [s_scr[s, h] for s, h in pairs], md, lp, consts)
        for i, (s, h) in enumerate(pairs):
            s_scr[s, h] = sts[i]
            o = os_[i][:lin]
            ms = jnp.mean(o * o, axis=-1, keepdims=True)
            y = o * lax.rsqrt(ms + EPS) * gn_ref[...]
            r = r_ref[s, rows, vc[i]].astype(F32)
            o_ref[s, rows, vc[i]] = (y * (r * _sigmoid(r))).astype(o_ref.dtype)
        return carry

    lax.fori_loop(0, n_chunks, chunk, 0)

    @pl.when(t == pl.num_programs(1) - 1)
    def _():
        for s, h in pairs:
            st_ref[s, h] = s_scr[s, h].T


def gla(zg, lg, gnorm, s0, layer, heads, dk, dv, out_dtype, seqs_per_step=1):
    B, T, _ = zg.shape
    depth = gnorm.shape[0]
    qw, vw = heads * dk, heads * dv
    assert vw % qw == 0
    lin = min(GLA_CHUNK, T)
    assert T % lin == 0
    lp = max(lin, 16)
    tt = _tile(T, 512, lin)
    bb = _tile(B, seqs_per_step, 1)
    in_specs = [pl.BlockSpec((bb, tt, qw), lambda b, t: (b, t, 0)),
                pl.BlockSpec((bb, tt, qw), lambda b, t: (b, t, 1)),
                pl.BlockSpec((bb, tt, vw), lambda b, t: (b, t, 2 * qw // vw)),
                pl.BlockSpec((bb, tt, vw), lambda b, t: (b, t, 2 * qw // vw + 1)),
                pl.BlockSpec((bb, tt, qw), lambda b, t: (b, t, 0)),
                pl.BlockSpec((None, 1, dv), lambda b, t: (layer, 0, 0))]
    args = [zg, zg, zg, zg, lg, gnorm.reshape(depth, 1, dv)]
    if s0 is not None:
        in_specs.append(pl.BlockSpec((None, bb, heads, dk, dv), lambda b, t: (layer, b, 0, 0, 0)))
        args.append(s0)
    kern = functools.partial(_gla_kernel, lin=lin, lp=lp, n_chunks=tt // lin, heads=heads, scale=dk ** -0.5,
                             has_s0=s0 is not None)
    return pl.pallas_call(
        kern, grid=(B // bb, T // tt), in_specs=in_specs,
        out_specs=[pl.BlockSpec((bb, tt, vw), lambda b, t: (b, t, 0)),
                   pl.BlockSpec((bb, heads, dk, dv), lambda b, t: (b, 0, 0, 0))],
        out_shape=[jax.ShapeDtypeStruct((B, T, vw), out_dtype),
                   jax.ShapeDtypeStruct((B, heads, dk, dv), F32)],
        scratch_shapes=[pltpu.VMEM((bb, heads, dv, dk), F32)],
        compiler_params=_cparams(("parallel", "arbitrary")), name="gla",
    )(*args)


def _swa_kernel(sink_ref, q_ref, kp_ref, kc_ref, vp_ref, vc_ref, bias_ref, o_ref, *,
                layer, heads, group, hd, scale):
    tq = q_ref.shape[0]
    md = _mxu_dtype(tq)
    k = jnp.concatenate([kp_ref[...].astype(md), kc_ref[...].astype(md)], axis=0)
    v = jnp.concatenate([vp_ref[...].astype(md), vc_ref[...].astype(md)], axis=0)
    hs = range(heads)
    kvc = [slice((h // group) * hd, (h // group + 1) * hd) for h in hs]
    ss = [_dot_nt((q_ref[:, h * hd:(h + 1) * hd].astype(F32) * scale).astype(md), k[:, kvc[h]]) + bias_ref[h]
          for h in hs]
    sinks = [sink_ref[layer, h] for h in hs]
    ms = [jnp.maximum(jnp.max(ss[h], axis=-1, keepdims=True), sinks[h]) for h in hs]
    es = [jnp.exp(ss[h] - ms[h]) for h in hs]
    dens = [jnp.sum(es[h], axis=-1, keepdims=True) + jnp.exp(sinks[h] - ms[h]) for h in hs]
    outs = [_dot(es[h].astype(md), v[:, kvc[h]]) / dens[h] for h in hs]
    per = LANE // hd
    for j in range(heads // per):
        o_ref[:, j * LANE:(j + 1) * LANE] = jnp.concatenate(
            outs[j * per:(j + 1) * per], axis=-1).astype(o_ref.dtype)


def _t5_bucket(dist):
    n = jnp.maximum(dist, 0)
    max_exact = N_BUCKETS // 2
    nf = jnp.maximum(n, 1).astype(F32)
    large = max_exact + (jnp.log(nf / max_exact) / math.log(MAX_DIST / max_exact)
                         * (N_BUCKETS - max_exact)).astype(jnp.int32)
    large = jnp.minimum(large, N_BUCKETS - 1)
    return jnp.where(n < max_exact, n, large)


def _swa_bias_table(rel_bias, tq, p, with_first):
    i = jnp.arange(tq)[:, None]
    j = jnp.arange(p + tq)[None, :]
    dist = p + i - j
    valid = (dist >= 0) & (dist <= WINDOW)
    onehot = (_t5_bucket(dist)[:, :, None] == jnp.arange(N_BUCKETS)).astype(F32)
    bias = jnp.einsum("ijb,bh->hij", onehot, rel_bias.astype(F32), precision=lax.Precision.HIGHEST)
    regular = jnp.where(valid[None], bias, NEG)
    if not with_first:
        return regular[None]
    return jnp.stack([jnp.where((valid & (j >= p))[None], bias, NEG), regular])


def swa(z3, prev_k, prev_v, sinks, bias, layer, kv_heads, hd, tq, out_dtype):
    B, T, _ = z3.shape
    heads = sinks.shape[1]
    qw, kw = heads * hd, kv_heads * hd
    assert qw % kw == 0
    ko, vo = qw // kw, qw // kw + 1
    nb = T // tq
    prompt = prev_k is None
    p = bias.shape[3] - tq
    if prompt:
        assert p == tq and bias.shape[0] == 2
        prev_specs = [pl.BlockSpec((None, p, kw), lambda b, n: (b, jnp.maximum(n - 1, 0), ko)),
                      pl.BlockSpec((None, p, kw), lambda b, n: (b, jnp.maximum(n - 1, 0), vo))]
        prev_k = prev_v = z3
        bias_spec = pl.BlockSpec((None,) + bias.shape[1:], lambda b, n: (jnp.minimum(n, 1), 0, 0, 0))
    else:
        assert nb == 1 and bias.shape[0] == 1
        prev_specs = [pl.BlockSpec((None, None, p, kw), lambda b, n: (layer, b, 0, 0))] * 2
        bias_spec = pl.BlockSpec((None,) + bias.shape[1:], lambda b, n: (0, 0, 0, 0))
    kern = functools.partial(_swa_kernel, layer=layer, heads=heads, group=heads // kv_heads, hd=hd,
                             scale=hd ** -0.5)
    return pl.pallas_call(
        kern, grid=(B, nb),
        in_specs=[pl.BlockSpec(memory_space=pltpu.SMEM),
                  pl.BlockSpec((None, tq, qw), lambda b, n: (b, n, 0)),
                  prev_specs[0],
                  pl.BlockSpec((None, tq, kw), lambda b, n: (b, n, ko)),
                  prev_specs[1],
                  pl.BlockSpec((None, tq, kw), lambda b, n: (b, n, vo)),
                  bias_spec],
        out_specs=pl.BlockSpec((None, tq, qw), lambda b, n: (b, n, 0)),
        out_shape=jax.ShapeDtypeStruct((B, T, qw), out_dtype),
        compiler_params=_cparams(("parallel", "arbitrary")), name="swa",
    )(sinks, z3, prev_k, z3, prev_v, z3, bias)


def _swa_decode_kernel(q_ref, kc_ref, kn_ref, vc_ref, vn_ref, tab_ref, o_ref, *, kv_heads, group, hd, scale):
    bb, T, _ = q_ref.shape
    kw = kc_ref.shape[-1]
    npad = tab_ref.shape[-1] - kc_ref.shape[1] - T
    md = _mxu_dtype(group * T)
    per = LANE // hd
    zpad = jnp.zeros((npad, kw), F32)
    pairs = [(s, g) for s in range(bb) for g in range(kv_heads)]
    ks = [jnp.concatenate([kc_ref[s].astype(F32), kn_ref[s].astype(F32), zpad], axis=0).astype(md) for s in range(bb)]
    vs = [jnp.concatenate([vc_ref[s].astype(F32), vn_ref[s].astype(F32), zpad], axis=0).astype(md) for s in range(bb)]
    scs = []
    for s, g in pairs:
        qs = jnp.concatenate([q_ref[s, :, (g * group + j) * hd:(g * group + j + 1) * hd].astype(F32)
                              for j in range(group)], axis=0) * scale
        scs.append(_dot_nt(qs.astype(md), ks[s][:, g * hd:(g + 1) * hd]) + tab_ref[g])
    es = [jnp.exp(sc - jnp.max(sc, axis=-1, keepdims=True)) for sc in scs]
    dens = [jnp.sum(e, axis=-1, keepdims=True) for e in es]
    os_ = [_dot(e.astype(md), vs[s][:, g * hd:(g + 1) * hd]) / den for (s, g), e, den in zip(pairs, es, dens)]
    for s in range(bb):
        outs = [os_[s * kv_heads + g][j * T:(j + 1) * T] for g in range(kv_heads) for j in range(group)]
        for j in range(len(outs) // per):
            o_ref[s, :, j * LANE:(j + 1) * LANE] = jnp.concatenate(
                outs[j * per:(j + 1) * per], axis=-1).astype(o_ref.dtype)


def _swa_decode_table(rel_bias, sinks, T, p, kv_heads):
    depth, heads = sinks.shape
    group = heads // kv_heads
    base = _swa_bias_table(rel_bias, T, p, False)[0].reshape(kv_heads, group * T, p + T)
    npad = _round_up(p + T + 1, 16) - (p + T)
    sink_col = jnp.repeat(sinks.astype(F32).reshape(depth, kv_heads, group), T, axis=2)[..., None]
    parts = [jnp.broadcast_to(base[None], (depth,) + base.shape), sink_col,
             jnp.full((depth, kv_heads, group * T, npad - 1), NEG, F32)]
    return jnp.concatenate(parts, axis=-1)


def swa_decode(z3, cache_k, cache_v, table, layer, kv_heads, hd, out_dtype, seqs_per_step=8):
    B, T, zw = z3.shape
    kw = kv_heads * hd
    qw = zw - 2 * kw
    p = cache_k.shape[2]
    bb = _tile(B, seqs_per_step, 1)
    kern = functools.partial(_swa_decode_kernel, kv_heads=kv_heads, group=qw // kw, hd=hd, scale=hd ** -0.5)
    c_spec = pl.BlockSpec((None, bb, p, kw), lambda b: (layer, b, 0, 0))
    return pl.pallas_call(
        kern, grid=(B // bb,),
        in_specs=[pl.BlockSpec((bb, T, qw), lambda b: (b, 0, 0)),
                  c_spec, pl.BlockSpec((bb, T, kw), lambda b: (b, 0, qw // kw)),
                  c_spec, pl.BlockSpec((bb, T, kw), lambda b: (b, 0, qw // kw + 1)),
                  pl.BlockSpec((None,) + table.shape[1:], lambda b: (layer, 0, 0, 0))],
        out_specs=pl.BlockSpec((bb, T, qw), lambda b: (b, 0, 0)),
        out_shape=jax.ShapeDtypeStruct((B, T, qw), out_dtype),
        compiler_params=_cparams(("parallel",)), name="swa_decode",
    )(z3, cache_k, z3, cache_v, z3, table)


class _ConvTaps:
    def __init__(self, i, j, cols, tm, prev_ref, st_ref, carry_scr, seq_len):
        self.j, self.cols, self.seq_len = j, cols, seq_len
        self.prev_ref, self.st_ref, self.carry_scr = prev_ref, st_ref, carry_scr
        self.single = tm <= seq_len
        if self.single:
            @pl.when((i % (seq_len // tm)) == 0)
            def _():
                carry_scr[j, :, cols] = prev_ref[0, :, cols]

            self.before = carry_scr[j, :, cols]

    def chunk(self, x, r):
        n, w = x.shape
        if self.single:
            before = self.before
            row8 = lax.broadcasted_iota(jnp.int32, (SUBLANE, w), 0)

            def back(s):
                xr = pltpu.roll(x, s, 0)
                top = jnp.where(row8 < s, pltpu.roll(before, s, 0), xr[0:SUBLANE])
                return jnp.concatenate([top, xr[SUBLANE:]], axis=0)

            self.before = x[n - SUBLANE:n]
            return back(1), back(2)
        t = self.seq_len
        assert r % t == 0 and n % t == 0
        seqs = slice(r // t, (r + n) // t)
        tix = lax.rem(lax.broadcasted_iota(jnp.int32, (n, w), 0), t)
        s1 = jnp.broadcast_to(self.prev_ref[seqs, SUBLANE - 1:SUBLANE, self.cols], (n // t, t, w)).reshape(n, w)
        s0 = jnp.broadcast_to(self.prev_ref[seqs, SUBLANE - 2:SUBLANE - 1, self.cols], (n // t, t, w)).reshape(n, w)
        self.st_ref[seqs, :, self.cols] = x.reshape(n // t, t, w)[:, t - SUBLANE:, :]
        x1 = jnp.where(tix == 0, s1, pltpu.roll(x, 1, 0))
        x2 = jnp.where(tix == 0, s0, jnp.where(tix == 1, s1, pltpu.roll(x, 2, 0)))
        return x1, x2

    def finish(self):
        if self.single:
            self.carry_scr[self.j, :, self.cols] = self.before
            self.st_ref[0, :, self.cols] = self.before


def _row_blocking(B, T, tm):
    if T >= tm:
        tm = _tile(T, tm, SUBLANE)
        return tm, 1, (lambda i: i // (T // tm))
    assert T % SUBLANE == 0
    nseq = _tile(B, max(tm // T, 1), 1)
    return nseq * T, nseq, (lambda i: i)


def _conv_branch_kernel(cb_ref, cc_ref, ch_ref, prev_ref, w_ref, o_ref, st_ref, carry_scr, *, seq_len):
    i, j = pl.program_id(0), pl.program_id(1)
    taps = _ConvTaps(i, j, slice(None), o_ref.shape[0], prev_ref, st_ref, carry_scr, seq_len)
    for r, n in _row_chunks(o_ref.shape[0]):
        rows = slice(r, r + n)
        u = cc_ref[rows, :].astype(F32) * ch_ref[rows, :].astype(F32)
        u1, u2 = taps.chunk(u, r)
        y = u2 * w_ref[0:1, :] + u1 * w_ref[1:2, :] + u * w_ref[2:3, :]
        o_ref[rows, :] = (cb_ref[rows, :].astype(F32) * y).astype(o_ref.dtype)
    taps.finish()


def conv_branch(z, prev8, conv_w, layer, B, T, out_dtype, tm=1024, tc=512):
    M = B * T
    width = conv_w.shape[2]
    tc = _tile(width, tc, LANE)
    nc = width // tc
    tm, nseq, seq_of = _row_blocking(B, T, tm)
    n_st = (M // tm) * nseq
    z_specs = [pl.BlockSpec((tm, tc), (lambda i, j, o=n * nc: (i, o + j))) for n in range(3)]
    out, st = pl.pallas_call(
        functools.partial(_conv_branch_kernel, seq_len=T), grid=(M // tm, nc),
        in_specs=z_specs + [pl.BlockSpec((nseq, SUBLANE, tc), lambda i, j: (seq_of(i), 0, j)),
                            pl.BlockSpec((None, CONV_W, tc), lambda i, j: (layer, 0, j))],
        out_specs=[pl.BlockSpec((tm, tc), lambda i, j: (i, j)),
                   pl.BlockSpec((nseq, SUBLANE, tc), lambda i, j: (i, 0, j))],
        out_shape=[jax.ShapeDtypeStruct((M, width), out_dtype),
                   jax.ShapeDtypeStruct((n_st, SUBLANE, width), F32)],
        scratch_shapes=[pltpu.VMEM((nc, SUBLANE, tc), F32)],
        compiler_params=_cparams(("arbitrary", "arbitrary")), name="conv_branch",
    )(z, z, z, prev8, conv_w)
    return out, st.reshape(B, n_st // B, SUBLANE, width)[:, -1]


def _ffn_up_tile(i, j, h_ref, wu_ref, wg_ref, prev_ref, cw_ref, cb_ref, act_ref, st_ref, carry_scr, seq_len):
    tm, tn = act_ref.shape
    for c, w in _col_chunks(tn):
        cols = slice(c, c + w)
        wu = wu_ref[:, cols].astype(BF16)
        wg = wg_ref[:, cols].astype(BF16)
        taps = _ConvTaps(i, j, cols, tm, prev_ref, st_ref, carry_scr, seq_len)
        for r, n in _row_chunks(tm):
            h = h_ref[r:r + n, :]
            g = _dot(h, wg)
            g1, g2 = taps.chunk(g, r)
            gc = g2 * cw_ref[0:1, cols] + g1 * cw_ref[1:2, cols] + g * cw_ref[2:3, cols] + cb_ref[:, cols]
            act_ref[r:r + n, cols] = (gc * _sigmoid(gc) * _dot(h, wu)).astype(act_ref.dtype)
        taps.finish()


def _ffn_up_kernel(h_ref, h2_ref, wu_ref, wg_ref, prev_ref, prev2_ref, cw_ref, cb_ref,
                   act_ref, st_ref, act2_ref, st2_ref, carry_scr, *, seq_len, seq_len2):
    i, j = pl.program_id(0), pl.program_id(1)
    _ffn_up_tile(i, j, h_ref, wu_ref, wg_ref, prev_ref, cw_ref, cb_ref, act_ref, st_ref, carry_scr, seq_len)

    @pl.when(i == 0)
    def _():
        _ffn_up_tile(0, j, h2_ref, wu_ref, wg_ref, prev2_ref, cw_ref, cb_ref, act2_ref, st2_ref, None, seq_len2)


def ffn_first_half(h, h2, w_up, layer, col0, width, prev8, prev8_2, conv_w, conv_b, B, T, B2, T2, out_dtype,
                   tm=2048, tn=512):
    M, K = h.shape
    M2 = h2.shape[0]
    dff = w_up.shape[2] // 2
    tn = min(tn, width)
    assert col0 % LANE == 0 and dff % LANE == 0 and width % tn == 0 and M2 == B2 * T2 and T2 % SUBLANE == 0
    nj = width // tn
    tm, nseq, seq_of = _row_blocking(B, T, tm)
    n_st = (M // tm) * nseq

    def col(j, base=col0):
        return pl.multiple_of(base + j * tn, LANE)

    def parked(i, j):
        return jnp.where(i == 0, j, nj - 1)

    def win(rows):
        return pl.BlockSpec((pl.Element(rows), pl.Element(tn)), lambda i, j: (0, col(j)))

    def state_win(n, first, tile):
        return pl.BlockSpec((pl.Element(n), pl.Element(SUBLANE), pl.Element(tn)),
                            lambda i, j: (first(i), 0, col(tile(i, j))))

    act, st, act2, st2 = pl.pallas_call(
        functools.partial(_ffn_up_kernel, seq_len=T, seq_len2=T2), grid=(M // tm, nj),
        in_specs=[pl.BlockSpec((tm, K), lambda i, j: (i, 0)),
                  pl.BlockSpec((M2, K), lambda i, j: (0, 0)),
                  pl.BlockSpec((None, pl.Element(K), pl.Element(tn)), lambda i, j: (layer, 0, col(j))),
                  pl.BlockSpec((None, pl.Element(K), pl.Element(tn)), lambda i, j: (layer, 0, col(j, dff + col0))),
                  state_win(nseq, lambda i: seq_of(i) * nseq, lambda i, j: j),
                  state_win(B2, lambda i: 0, parked),
                  win(CONV_W), win(1)],
        out_specs=[pl.BlockSpec((tm, tn), lambda i, j: (i, j)),
                   pl.BlockSpec((nseq, SUBLANE, tn), lambda i, j: (i, 0, j)),
                   _second_group_tile(M2, tn, nj),
                   pl.BlockSpec((B2, SUBLANE, tn), lambda i, j: (0, 0, parked(i, j)))],
        out_shape=[jax.ShapeDtypeStruct((M, width), out_dtype),
                   jax.ShapeDtypeStruct((n_st, SUBLANE, width), F32),
                   jax.ShapeDtypeStruct((M2, width), out_dtype),
                   jax.ShapeDtypeStruct((B2, SUBLANE, width), F32)],
        scratch_shapes=[pltpu.VMEM((nj, SUBLANE, tn), F32)],
        compiler_params=_cparams(("arbitrary", "arbitrary")), name="ffn_up",
    )(h, h2, w_up, w_up, prev8, prev8_2, conv_w, conv_b)
    return (act, st.reshape(B, n_st // B, SUBLANE, width)[:, -1]), (act2, st2)


def _merge_tile(xs, gs, wb_ref, o_ref):
    for c, w in _col_chunks(o_ref.shape[1]):
        cols = slice(c, c + w)
        ws = [wb_ref[n, :, cols].astype(BF16) for n in range(3)]
        for r, m in _row_chunks(o_ref.shape[0]):
            rows = slice(r, r + m)

            def term(n):
                return _sigmoid(gs[n][rows, cols].astype(F32)) * _dot(xs[n][rows, :].astype(BF16), ws[n])

            o_ref[rows, cols] = ((term(0) + term(1)) + term(2)).astype(o_ref.dtype)


def _merge_kernel(a_ref, b_ref, c_ref, wb_ref, g0_ref, g1_ref, g2_ref,
                  a2_ref, b2_ref, c2_ref, h0_ref, h1_ref, h2_ref, o_ref, o2_ref):
    _merge_tile((a_ref, b_ref, c_ref), (g0_ref, g1_ref, g2_ref), wb_ref, o_ref)

    @pl.when(pl.program_id(0) == 0)
    def _():
        _merge_tile((a2_ref, b2_ref, c2_ref), (h0_ref, h1_ref, h2_ref), wb_ref, o2_ref)


def merge(brs, brs2, wb, layer, zg, zg2, gate_col0, out_dtype, tm=2048, tn=256):
    M, mix = brs[0].shape
    M2 = brs2[0].shape[0]
    d = wb.shape[3]
    tm = _tile(M, tm, SUBLANE)
    tn = _tile(d, tn, LANE)
    nj = d // tn
    assert gate_col0 % tn == 0
    br_spec = pl.BlockSpec((tm, mix), lambda i, j: (i, 0))
    br2_spec = pl.BlockSpec((M2, mix), lambda i, j: (0, 0))
    offs = [(gate_col0 + n * d) // tn for n in range(3)]
    g_specs = [pl.BlockSpec((tm, tn), (lambda i, j, o=o: (i, o + j))) for o in offs]
    g2_specs = [pl.BlockSpec((M2, tn), (lambda i, j, o=o: (0, o + jnp.where(i == 0, j, nj - 1)))) for o in offs]
    return pl.pallas_call(
        _merge_kernel, grid=(M // tm, nj),
        in_specs=[br_spec, br_spec, br_spec, pl.BlockSpec((None, 3, mix, tn), lambda i, j: (layer, 0, 0, j))]
        + g_specs + [br2_spec, br2_spec, br2_spec] + g2_specs,
        out_specs=[pl.BlockSpec((tm, tn), lambda i, j: (i, j)), _second_group_tile(M2, tn, nj)],
        out_shape=[jax.ShapeDtypeStruct((M, d), out_dtype), jax.ShapeDtypeStruct((M2, d), out_dtype)],
        compiler_params=_cparams(("arbitrary", "arbitrary")), name="merge",
    )(*brs, wb, zg, zg, zg, *brs2, zg2, zg2, zg2)


def _xattn_kernel(q_ref, k_ref, v_ref, o_ref, *, heads, hd, scale):
    md = _mxu_dtype(q_ref.shape[0])
    for h in range(heads):
        sl = slice(h * hd, (h + 1) * hd)
        s = _dot_nt(q_ref[:, sl].astype(md), k_ref[:, sl].astype(md)) * scale
        e = jnp.exp(s - jnp.max(s, axis=-1, keepdims=True))
        den = jnp.sum(e, axis=-1, keepdims=True)
        o_ref[:, sl] = (_dot(e.astype(md), v_ref[:, sl].astype(md)) / den).astype(o_ref.dtype)


def xattn(q3, mem_k, mem_v, layer, heads, out_dtype, tq=512):
    B, T, w = q3.shape
    nm = mem_k.shape[-2]
    hd = w // heads
    tq = _tile(T, tq, SUBLANE)
    kern = functools.partial(_xattn_kernel, heads=heads, hd=hd, scale=hd ** -0.5)
    if mem_k.ndim == 4:
        m_spec = pl.BlockSpec((None, None, nm, w), lambda b, t: (layer, b, 0, 0))
    else:
        m_spec = pl.BlockSpec((None, nm, w), lambda b, t: (b, 0, 0))
    return pl.pallas_call(
        kern, grid=(B, T // tq),
        in_specs=[pl.BlockSpec((None, tq, w), lambda b, t: (b, t, 0)), m_spec, m_spec],
        out_specs=pl.BlockSpec((None, tq, w), lambda b, t: (b, t, 0)),
        out_shape=jax.ShapeDtypeStruct((B, T, w), out_dtype),
        compiler_params=_cparams(("parallel", "arbitrary")), name="xattn",
    )(q3, mem_k, mem_v)


def _xattn_decode_kernel(q_ref, k_ref, v_ref, o_ref, *, heads, hd, scale):
    bb, T, _ = q_ref.shape
    nm = k_ref.shape[1] // heads
    rows = max(T, 16)
    pairs = [(s, h) for s in range(bb) for h in range(heads)]
    scs = [_dot_nt(_pad_rows(q_ref[s, :, h * hd:(h + 1) * hd].astype(F32), rows).astype(BF16),
                   k_ref[s, pl.ds(h, nm, stride=heads), :].astype(BF16)) * scale for s, h in pairs]
    es = [jnp.exp(sc - jnp.max(sc, axis=-1, keepdims=True)) for sc in scs]
    dens = [jnp.sum(e, axis=-1, keepdims=True) for e in es]
    for (s, h), e, den in zip(pairs, es, dens):
        v = v_ref[s, pl.ds(h, nm, stride=heads), :].astype(BF16)
        o_ref[s, :, h * hd:(h + 1) * hd] = (_dot(e.astype(BF16), v) / den)[:T].astype(o_ref.dtype)


def xattn_decode(q3, mem_k, mem_v, layer, heads, out_dtype, seqs_per_step=8):
    B, T, w = q3.shape
    hd = w // heads
    bb = _tile(B, seqs_per_step, 1)
    kern = functools.partial(_xattn_decode_kernel, heads=heads, hd=hd, scale=hd ** -0.5)
    m_spec = pl.BlockSpec((None, bb) + mem_k.shape[2:], lambda b: (layer, b, 0, 0))
    return pl.pallas_call(
        kern, grid=(B // bb,),
        in_specs=[pl.BlockSpec((bb, T, w), lambda b: (b, 0, 0)), m_spec, m_spec],
        out_specs=pl.BlockSpec((bb, T, w), lambda b: (b, 0, 0)),
        out_shape=jax.ShapeDtypeStruct((B, T, w), out_dtype),
        compiler_params=_cparams(("parallel",)), name="xattn_decode",
    )(q3, mem_k, mem_v)


def _round_up(n, m):
    return -(-n // m) * m


def kernel(x_prompt, x_sample, state_gla, cache_swa_k, cache_swa_v, state_conv, state_ffn, cache_mem_k, cache_mem_v, mem_prompt, norm_mix, w_in, gla_gate_up, gla_gate_b, gla_norm, swa_sinks, rel_bias, conv_w, w_branch, w_out, norm_x, wx_q, wx_k, wx_v, wx_o, norm_ffn, ffn_up, ffn_conv_w, ffn_conv_b, ffn_down, norm_final):
    Bp, Tp, D = x_prompt.shape
    Bs, Ts, _ = x_sample.shape
    depth = w_in.shape[0]
    gla_heads, gla_dk, gla_dv = state_gla.shape[2:]
    swa_kv, swa_hd = cache_swa_k.shape[3:]
    swa_heads = swa_sinks.shape[1]
    x_heads, x_hd = cache_mem_k.shape[3:]
    n_mem = mem_prompt.shape[1]
    mix = conv_w.shape[2]
    dff = ffn_down.shape[1]
    rank = gla_gate_up.shape[1]
    gqk = gla_heads * gla_dk
    sqw, skw = swa_heads * swa_hd, swa_kv * swa_hd
    win = cache_swa_k.shape[2]
    xw = x_heads * x_hd
    assert win == WINDOW and mix == gla_heads * gla_dv == sqw

    w_in_t = jnp.swapaxes(w_in, 1, 2)
    a_w = 2 * gqk + 2 * mix
    s_w = sqw + 2 * skw
    o_glr, o_swa = a_w, a_w + rank
    o_conv = o_swa + s_w
    o_gates = o_conv + 3 * mix
    ffn_main = dff // 512 * 512
    ffn_parts = [(0, ffn_main)] + ([(ffn_main, dff - ffn_main)] if dff > ffn_main else [])

    bias_p = _swa_bias_table(rel_bias, min(Tp, WINDOW), WINDOW, True)
    table_s = _swa_decode_table(rel_bias, swa_sinks, Ts, win, swa_kv)
    cache_k4 = cache_swa_k.reshape(depth, Bs, win, skw)
    cache_v4 = cache_swa_v.reshape(depth, Bs, win, skw)
    cmem_k4 = cache_mem_k.reshape(depth, Bs, n_mem * x_heads, x_hd)
    cmem_v4 = cache_mem_v.reshape(depth, Bs, n_mem * x_heads, x_hd)
    mem2 = mem_prompt.reshape(Bp * n_mem, D)

    def first8(st):
        return jnp.pad(st, ((0, 0), (SUBLANE - (CONV_W - 1), 0), (0, 0)))

    def in_proj(i, xa, xb, dt_a, dt_b):
        ha, hb = rmsnorm(xa, norm_mix[i], BF16), rmsnorm(xb, norm_mix[i], BF16)
        zg_a, lg_a, zg_b, lg_b = matmul_nt_gate(ha, hb, w_in_t, i, a_w, o_glr, gla_gate_up, gla_gate_b, dt_a, dt_b,
                                                name="in_proj_gla")
        zs_a, zs_b = matmul_nt(ha, hb, w_in_t, i, o_swa, s_w, dt_a, dt_b, tn=s_w, name="in_proj_swa")
        zc_a, zc_b = matmul_nt(ha, hb, w_in_t, i, o_conv, 3 * mix + 3 * D, dt_a, dt_b, name="in_proj_conv_gates")
        return (zg_a, lg_a, zs_a, zc_a), (zg_b, lg_b, zs_b, zc_b)

    def mixers(i, zs_all, B, T, decode, s0, conv_prev8, act_dtype):
        M = B * T
        zg, lg, zs, zc = zs_all
        zs3 = zs.reshape(B, T, s_w)
        br_a, gla_st = gla(zg.reshape(B, T, a_w), lg.reshape(B, T, gqk), gla_norm, s0, i,
                           gla_heads, gla_dk, gla_dv, act_dtype, seqs_per_step=4 if decode else 1)
        if decode:
            br_b = swa_decode(zs3, cache_k4, cache_v4, table_s, i, swa_kv, swa_hd, act_dtype)
        else:
            br_b = swa(zs3, None, None, swa_sinks, bias_p, i, swa_kv, swa_hd, min(T, WINDOW), act_dtype)
        br_c, conv_st = conv_branch(zc, conv_prev8, conv_w, i, B, T, act_dtype)
        k_new = zs3[:, T - min(T, WINDOW):, sqw:sqw + skw].astype(F32)
        v_new = zs3[:, T - min(T, WINDOW):, sqw + skw:].astype(F32)
        return ([br_a.reshape(M, mix), br_b.reshape(M, mix), br_c],
                (gla_st, k_new, v_new, conv_st[:, SUBLANE - (CONV_W - 1):]))

    def after_merge(i, x, merged, B, T, decode, mem_k, mem_v, act_dtype):
        M = B * T
        x, hx = res_matmul_norm(merged, w_out, x, norm_x[i], i, name="out_proj")
        q = matmul(hx, wx_q, i, act_dtype, name="xq_proj")
        if decode:
            xo = xattn_decode(q.reshape(B, T, xw), mem_k, mem_v, i, x_heads, act_dtype)
        else:
            xo = xattn(q.reshape(B, T, xw), mem_k, mem_v, i, x_heads, act_dtype)
        return res_matmul_norm(xo.reshape(M, xw), wx_o, x, norm_ffn[i], i, name="xo_proj")

    xp = x_prompt.reshape(Bp * Tp, D)
    xs = x_sample.reshape(Bs * Ts, D)
    act_p = BF16
    act_s = BF16 if Ts % 16 == 0 else F32
    st_p, st_s = [], []
    for i in range(depth):
        mem_k = matmul(mem2, wx_k, i, F32, name="mem_k").reshape(Bp, n_mem, xw)
        mem_v = matmul(mem2, wx_v, i, F32, name="mem_v").reshape(Bp, n_mem, xw)
        z_p, z_s = in_proj(i, xp, xs, act_p, act_s)
        brs_p, sp = mixers(i, z_p, Bp, Tp, False, None, jnp.zeros((Bp, SUBLANE, mix), F32), act_p)
        brs_s, ss = mixers(i, z_s, Bs, Ts, True, state_gla, first8(state_conv[i]), act_s)
        merged_p, merged_s = merge(brs_p, brs_s, w_branch, i, z_p[3], z_s[3], 3 * mix, BF16)
        xp, hf_p = after_merge(i, xp, merged_p, Bp, Tp, False, mem_k, mem_v, act_p)
        xs, hf_s = after_merge(i, xs, merged_s, Bs, Ts, True, cmem_k4, cmem_v4, act_s)
        ffn = [ffn_first_half(hf_p, hf_s, ffn_up, i, c0, wd, jnp.zeros((Bp, SUBLANE, dff), F32),
                              first8(state_ffn[i]), ffn_conv_w[i], ffn_conv_b[i].reshape(1, dff),
                              Bp, Tp, Bs, Ts, BF16) for c0, wd in ffn_parts]
        xp, xs = res_matmul([f[0][0] for f in ffn], ffn_down, xp, i, a2_parts=[f[1][0] for f in ffn], res2=xs,
                            tm=2048, tn=256, name="ffn_down")
        sp += (jnp.concatenate([f[0][1] for f in ffn], axis=-1)[:, SUBLANE - (CONV_W - 1):],)
        ss += (jnp.concatenate([f[1][1] for f in ffn], axis=-1)[:, SUBLANE - (CONV_W - 1):],)
        st_p.append(dict(gla=sp[0], k=sp[1].reshape(Bp, WINDOW, swa_kv, swa_hd),
                         v=sp[2].reshape(Bp, WINDOW, swa_kv, swa_hd), conv=sp[3], ffn=sp[4],
                         mem_k=mem_k.reshape(Bp, n_mem, x_heads, x_hd),
                         mem_v=mem_v.reshape(Bp, n_mem, x_heads, x_hd)))
        kc = jnp.concatenate([cache_k4[i], ss[1]], axis=1)[:, Ts:]
        vc = jnp.concatenate([cache_v4[i], ss[2]], axis=1)[:, Ts:]
        st_s.append(dict(gla=ss[0], k=kc.reshape(Bs, win, swa_kv, swa_hd), v=vc.reshape(Bs, win, swa_kv, swa_hd),
                         conv=ss[3], ffn=ss[4]))
    y_prompt = rmsnorm(xp, norm_final, F32).reshape(Bp, Tp, D)
    y_sample = rmsnorm(xs, norm_final, F32).reshape(Bs, Ts, D)

    def stack(lst, key):
        return jnp.stack([s[key] for s in lst])

    return (y_prompt, y_sample, stack(st_p, "gla"), stack(st_s, "gla"), stack(st_p, "k"), stack(st_p, "v"),
            stack(st_s, "k"), stack(st_s, "v"), stack(st_p, "conv"), stack(st_s, "conv"),
            stack(st_p, "ffn"), stack(st_s, "ffn"), stack(st_p, "mem_k"), stack(st_p, "mem_v"))
```
